```python
import math
import jax, jax.numpy as jnp
from jax import lax
import numpy as np

D_MODEL = 1024
BATCH = 8
SEQ = 4096
DEPTH = 2

CHUNK = 64
Q_BLOCK = 128
MEM_LEN = 256
ROPE_THETA = 10000.0
NORM_EPS = 1e-6

D_FF = 2816
DIFF_HEADS = 4
DIFF_HEAD_DIM = 64
DIFF_V_DIM = 2 * DIFF_HEAD_DIM
SB_HEADS = 8
SB_HEAD_DIM = 64
A_QK_WIDTH = 2 * DIFF_HEADS * DIFF_HEAD_DIM
A_V_WIDTH = DIFF_HEADS * DIFF_V_DIM
B_WIDTH = SB_HEADS * SB_HEAD_DIM
AB_SPLITS = (A_QK_WIDTH, 2 * A_QK_WIDTH, 2 * A_QK_WIDTH + A_V_WIDTH,
             2 * A_QK_WIDTH + A_V_WIDTH + B_WIDTH, 2 * A_QK_WIDTH + A_V_WIDTH + 2 * B_WIDTH)
AB_IN_WIDTH = 2 * A_QK_WIDTH + A_V_WIDTH + 3 * B_WIDTH
AB_MIX_WIDTH = A_V_WIDTH + B_WIDTH
MLA_HEADS = 16
MLA_Q_RANK = 512
MLA_KV_RANK = 256
MLA_NOPE_DIM = 64
MLA_ROPE_DIM = 32
MLA_V_DIM = 64
MLA_QK_DIM = MLA_NOPE_DIM + MLA_ROPE_DIM
XM_HEADS = 4
XM_HEAD_DIM = 128
XM_WIDTH = XM_HEADS * XM_HEAD_DIM

N_EVEN = (DEPTH + 1) // 2
N_ODD = DEPTH // 2

kernel_name = 'hybrid_diff_stickbreak_mla_macaron'


def rms_norm(x, gain):
    xf = x.astype(jnp.float32)
    y = xf * lax.rsqrt(jnp.mean(xf * xf, axis=-1, keepdims=True) + NORM_EPS)
    return (y * gain.astype(jnp.float32)).astype(x.dtype)


def rope(x):
    s, d = x.shape[-2], x.shape[-1]
    inv = 1.0 / (ROPE_THETA ** (jnp.arange(0, d, 2, dtype=jnp.float32) / d))
    ang = jnp.arange(s, dtype=jnp.float32)[:, None] * inv[None, :]
    cos, sin = jnp.cos(ang), jnp.sin(ang)
    xf = x.astype(jnp.float32)
    x1, x2 = xf[..., : d // 2], xf[..., d // 2:]
    return jnp.concatenate([x1 * cos - x2 * sin, x1 * sin + x2 * cos], axis=-1).astype(x.dtype)


def to_heads(t, n_heads):
    b, s, _ = t.shape
    return t.reshape(b, s, n_heads, -1).transpose(0, 2, 1, 3)


def from_heads(t):
    b, h, s, d = t.shape
    return t.transpose(0, 2, 1, 3).reshape(b, s, h * d)


def chunk_causal_mask(q_pos, k_pos):
    return (k_pos[None, :] // CHUNK) <= (q_pos[:, None] // CHUNK)


def sweep_query_blocks(block_fn, q):
    b, h, s, d = q.shape
    nb = s // Q_BLOCK
    qb = jnp.moveaxis(q.reshape(b, h, nb, Q_BLOCK, d), 2, 0)
    starts = jnp.arange(nb, dtype=jnp.int32) * Q_BLOCK
    out = lax.map(lambda a: block_fn(a[0], a[1]), (qb, starts))
    _, ob, oh, _, dv = out.shape
    return jnp.moveaxis(out, 0, 2).reshape(ob, oh, s, dv)


def chunk_causal_softmax_attention(q, k, v, scale):
    k_pos = jnp.arange(k.shape[2], dtype=jnp.int32)

    def block(qb, start):
        q_pos = start + jnp.arange(Q_BLOCK, dtype=jnp.int32)
        sc = jnp.einsum('bhqd,bhkd->bhqk', qb, k).astype(jnp.float32) * scale
        sc = jnp.where(chunk_causal_mask(q_pos, k_pos), sc, -jnp.inf)
        p = jax.nn.softmax(sc, axis=-1)
        return jnp.einsum('bhqk,bhkd->bhqd', p.astype(v.dtype), v)

    return sweep_query_blocks(block, q)


def differential_attention(q, k, v, lam, scale):
    n_h = v.shape[1]
    k_pos = jnp.arange(k.shape[2], dtype=jnp.int32)

    def block(qb, start):
        q_pos = start + jnp.arange(Q_BLOCK, dtype=jnp.int32)
        sc = jnp.einsum('bhqd,bhkd->bhqk', qb, k).astype(jnp.float32) * scale
        sc = jnp.where(chunk_causal_mask(q_pos, k_pos), sc, -jnp.inf)
        p = jax.nn.softmax(sc, axis=-1)
        p = p[:, :n_h] - lam * p[:, n_h:]
        return jnp.einsum('bhqk,bhkd->bhqd', p.astype(v.dtype), v)

    return sweep_query_blocks(block, q)


def stick_breaking_attention(q, k, v, scale):
    k_pos = jnp.arange(k.shape[2], dtype=jnp.int32)

    def block(qb, start):
        q_pos = start + jnp.arange(Q_BLOCK, dtype=jnp.int32)
        z = jnp.einsum('bhqd,bhkd->bhqk', qb, k).astype(jnp.float32) * scale
        strict = k_pos[None, :] < q_pos[:, None]
        log_beta = jax.nn.log_sigmoid(z)
        log_keep = jnp.where(strict, jax.nn.log_sigmoid(-z), 0.0)
        later = lax.cumsum(log_keep, axis=3, reverse=True) - log_keep
        w = jnp.where(strict, jnp.exp(log_beta + later), 0.0)
        return jnp.einsum('bhqk,bhkd->bhqd', w.astype(v.dtype), v)

    return sweep_query_blocks(block, q)


def swiglu(h, w_gate, w_up, w_down):
    return (jax.nn.silu(h @ w_gate) * (h @ w_up)) @ w_down


def diff_stickbreak_mixer(h, w_in, w_out, q_norm, k_norm, lq1, lk1, lq2, lk2, subln, lambda_init):
    proj = h @ w_in
    qa, ka, va, qb, kb, vb = jnp.split(proj, AB_SPLITS, axis=-1)
    qa = rope(rms_norm(to_heads(qa, 2 * DIFF_HEADS), q_norm))
    ka = rope(rms_norm(to_heads(ka, 2 * DIFF_HEADS), k_norm))
    va = to_heads(va, DIFF_HEADS)
    f32 = jnp.float32
    lam = (jnp.exp(jnp.sum(lq1.astype(f32) * lk1.astype(f32)))
           - jnp.exp(jnp.sum(lq2.astype(f32) * lk2.astype(f32))) + lambda_init)
    oa = differential_attention(qa, ka, va, lam, DIFF_HEAD_DIM ** -0.5)
    oa = rms_norm(oa, subln) * (1.0 - lambda_init)
    qb = to_heads(qb, SB_HEADS)
    kb = to_heads(kb, SB_HEADS)
    vb = to_heads(vb, SB_HEADS)
    ob = stick_breaking_attention(qb, kb, vb, SB_HEAD_DIM ** -0.5)
    mixed = jnp.concatenate([from_heads(oa), from_heads(ob)], axis=-1)
    return mixed @ w_out


def mla_mixer(h, w_dq, q_lat_norm, w_uq, w_dkv, kv_lat_norm, w_ukv, qk_norm_q, qk_norm_k, w_o):
    b, s, _ = h.shape
    c_q = rms_norm(h @ w_dq, q_lat_norm)
    q = to_heads(c_q @ w_uq, MLA_HEADS)
    dkv = h @ w_dkv
    c_kv = rms_norm(dkv[..., :MLA_KV_RANK], kv_lat_norm)
    k_rope = jnp.broadcast_to(dkv[:, None, :, MLA_KV_RANK:], (b, MLA_HEADS, s, MLA_ROPE_DIM))
    kv = to_heads(c_kv @ w_ukv, MLA_HEADS)
    k_nope, v = kv[..., :MLA_NOPE_DIM], kv[..., MLA_NOPE_DIM:]
    k = jnp.concatenate([k_nope, k_rope], axis=-1)
    q = rms_norm(q, qk_norm_q)
    k = rms_norm(k, qk_norm_k)
    q = jnp.concatenate([q[..., :MLA_NOPE_DIM], rope(q[..., MLA_NOPE_DIM:])], axis=-1)
    k = jnp.concatenate([k[..., :MLA_NOPE_DIM], rope(k[..., MLA_NOPE_DIM:])], axis=-1)
    o = chunk_causal_softmax_attention(q, k, v, MLA_QK_DIM ** -0.5)
    return from_heads(o) @ w_o


def memory_cross_attention(h, m, w_q, w_kv, q_norm, k_norm, w_o):
    q = rms_norm(to_heads(h @ w_q, XM_HEADS), q_norm)
    k, v = jnp.split(m @ w_kv, 2, axis=-1)
    k = rms_norm(to_heads(k, XM_HEADS), k_norm)
    v = to_heads(v, XM_HEADS)
    sc = jnp.einsum('bhqd,bhkd->bhqk', q, k).astype(jnp.float32) * (XM_HEAD_DIM ** -0.5)
    p = jax.nn.softmax(sc, axis=-1)
    o = jnp.einsum('bhqk,bhkd->bhqd', p.astype(v.dtype), v)
    return from_heads(o) @ w_o


def setup_inputs(seed: int = 0) -> dict:
    key = jax.random.key(seed)
    keys = jax.random.split(key, 40)
    counter = [0]

    def nk():
        k = keys[counter[0]]
        counter[0] += 1
        return k

    def dense(shape, fan_in):
        return jax.random.normal(nk(), shape, jnp.float32) * (fan_in ** -0.5)

    def gain(shape):
        return 1.0 + 0.02 * jax.random.normal(nk(), shape, jnp.float32)

    def small(shape, scale):
        return scale * jax.random.normal(nk(), shape, jnp.float32)

    inputs = {}
    inputs['x'] = jax.random.normal(nk(), (BATCH, SEQ, D_MODEL), jnp.float32)
    inputs['mem'] = jax.random.normal(nk(), (BATCH, MEM_LEN, D_MODEL), jnp.float32)
    inputs['ffn_norm'] = gain((DEPTH, 2, D_MODEL))
    inputs['ffn_w_gate'] = dense((DEPTH, 2, D_MODEL, D_FF), D_MODEL)
    inputs['ffn_w_up'] = dense((DEPTH, 2, D_MODEL, D_FF), D_MODEL)
    inputs['ffn_w_down'] = dense((DEPTH, 2, D_FF, D_MODEL), D_FF)
    inputs['mix_norm'] = gain((DEPTH, D_MODEL))
    inputs['ab_w_in'] = dense((N_EVEN, D_MODEL, AB_IN_WIDTH), D_MODEL)
    inputs['ab_w_out'] = dense((N_EVEN, AB_MIX_WIDTH, D_MODEL), AB_MIX_WIDTH)
    inputs['diff_q_norm'] = gain((N_EVEN, DIFF_HEAD_DIM))
    inputs['diff_k_norm'] = gain((N_EVEN, DIFF_HEAD_DIM))
    inputs['diff_lambda_q1'] = small((N_EVEN, DIFF_HEAD_DIM), 0.1)
    inputs['diff_lambda_k1'] = small((N_EVEN, DIFF_HEAD_DIM), 0.1)
    inputs['diff_lambda_q2'] = small((N_EVEN, DIFF_HEAD_DIM), 0.1)
    inputs['diff_lambda_k2'] = small((N_EVEN, DIFF_HEAD_DIM), 0.1)
    inputs['diff_subln'] = gain((N_EVEN, DIFF_V_DIM))
    inputs['mla_w_dq'] = dense((N_ODD, D_MODEL, MLA_Q_RANK), D_MODEL)
    inputs['mla_q_norm'] = gain((N_ODD, MLA_Q_RANK))
    inputs['mla_w_uq'] = dense((N_ODD, MLA_Q_RANK, MLA_HEADS * MLA_QK_DIM), MLA_Q_RANK)
    inputs['mla_w_dkv'] = dense((N_ODD, D_MODEL, MLA_KV_RANK + MLA_ROPE_DIM), D_MODEL)
    inputs['mla_kv_norm'] = gain((N_ODD, MLA_KV_RANK))
    inputs['mla_w_ukv'] = dense((N_ODD, MLA_KV_RANK, MLA_HEADS * (MLA_NOPE_DIM + MLA_V_DIM)), MLA_KV_RANK)
    inputs['mla_qk_norm_q'] = gain((N_ODD, MLA_QK_DIM))
    inputs['mla_qk_norm_k'] = gain((N_ODD, MLA_QK_DIM))
    inputs['mla_w_o'] = dense((N_ODD, MLA_HEADS * MLA_V_DIM, D_MODEL), MLA_HEADS * MLA_V_DIM)
    inputs['xm_norm'] = gain((DEPTH, D_MODEL))
    inputs['xm_mem_norm'] = gain((DEPTH, D_MODEL))
    inputs['xm_w_q'] = dense((DEPTH, D_MODEL, XM_WIDTH), D_MODEL)
    inputs['xm_w_kv'] = dense((DEPTH, D_MODEL, 2 * XM_WIDTH), D_MODEL)
    inputs['xm_q_norm'] = gain((DEPTH, XM_HEAD_DIM))
    inputs['xm_k_norm'] = gain((DEPTH, XM_HEAD_DIM))
    inputs['xm_w_o'] = dense((DEPTH, XM_WIDTH, D_MODEL), XM_WIDTH)
    return inputs


def reference(x, mem, ffn_norm, ffn_w_gate, ffn_w_up, ffn_w_down, mix_norm,
              ab_w_in, ab_w_out, diff_q_norm, diff_k_norm, diff_lambda_q1, diff_lambda_k1,
              diff_lambda_q2, diff_lambda_k2, diff_subln,
              mla_w_dq, mla_q_norm, mla_w_uq, mla_w_dkv, mla_kv_norm, mla_w_ukv,
              mla_qk_norm_q, mla_qk_norm_k, mla_w_o,
              xm_norm, xm_mem_norm, xm_w_q, xm_w_kv, xm_q_norm, xm_k_norm, xm_w_o):
    for layer in range(DEPTH):
        i = layer // 2
        x = x + 0.5 * swiglu(rms_norm(x, ffn_norm[layer, 0]),
                             ffn_w_gate[layer, 0], ffn_w_up[layer, 0], ffn_w_down[layer, 0])
        h = rms_norm(x, mix_norm[layer])
        if layer % 2 == 0:
            lambda_init = 0.8 - 0.6 * math.exp(-0.3 * layer)
            x = x + diff_stickbreak_mixer(h, ab_w_in[i], ab_w_out[i], diff_q_norm[i], diff_k_norm[i],
                                          diff_lambda_q1[i], diff_lambda_k1[i],
                                          diff_lambda_q2[i], diff_lambda_k2[i],
                                          diff_subln[i], lambda_init)
        else:
            x = x + mla_mixer(h, mla_w_dq[i], mla_q_norm[i], mla_w_uq[i], mla_w_dkv[i],
                              mla_kv_norm[i], mla_w_ukv[i], mla_qk_norm_q[i], mla_qk_norm_k[i],
                              mla_w_o[i])
        x = x + memory_cross_attention(rms_norm(x, xm_norm[layer]), rms_norm(mem, xm_mem_norm[layer]),
                                       xm_w_q[layer], xm_w_kv[layer], xm_q_norm[layer],
                                       xm_k_norm[layer], xm_w_o[layer])
        x = x + 0.5 * swiglu(rms_norm(x, ffn_norm[layer, 1]),
                             ffn_w_gate[layer, 1], ffn_w_up[layer, 1], ffn_w_down[layer, 1])
    return x
```

```python
import functools
import math

import jax
import jax.numpy as jnp
from jax import lax
from jax.experimental import pallas as pl
from jax.experimental.pallas import tpu as pltpu

F32 = jnp.float32
BF16 = jnp.bfloat16

LANES = 128
V7X_VMEM_LIMIT = 56 * 1024 * 1024

CHUNK = 64
ROPE_THETA = 10000.0
NORM_EPS = 1e-6
SB_LOG_ZERO = -104.0

DIFF_HEADS = 4
DIFF_HEAD_DIM = 64
SB_HEADS = 8
SB_HEAD_DIM = 64
MLA_HEADS = 16
MLA_KV_RANK = 256
MLA_NOPE_DIM = 64
MLA_ROPE_DIM = 32
MLA_V_DIM = 64
MLA_QK_DIM = MLA_NOPE_DIM + MLA_ROPE_DIM
XM_HEADS = 4
XM_HEAD_DIM = 128


def _params(n_grid, vmem_mb):
    return pltpu.CompilerParams(
        dimension_semantics=("arbitrary",) * n_grid,
        vmem_limit_bytes=min(vmem_mb * 1024 * 1024, V7X_VMEM_LIMIT))


def _resident(shape):
    nd = len(shape)
    return pl.BlockSpec(shape, lambda *_: (0,) * nd, pipeline_mode=pl.Buffered(1))


def _rms(x, gain):
    return x * lax.rsqrt(jnp.mean(x * x, axis=-1, keepdims=True) + NORM_EPS) * gain


def _dot(a, b):
    return jnp.dot(a, b, preferred_element_type=F32)


def _dot_nt(a, b):
    return lax.dot_general(a, b, (((1,), (1,)), ((), ())), preferred_element_type=F32)


def _lane_iota(shape):
    return lax.broadcasted_iota(jnp.int32, shape, len(shape) - 1)


def _rope(x, cos, sin_a, sin_b, half):
    return x * cos + pltpu.roll(x, half, 1) * sin_a + pltpu.roll(x, LANES - half, 1) * sin_b


def _ffn_kernel(x_ref, g_ref, wg_ref, wu_ref, wd_ref, o_ref, h_ref, acc_ref):
    x = x_ref[...]
    h_ref[...] = _rms(x, g_ref[...]).astype(BF16)
    acc_ref[...] = jnp.zeros_like(acc_ref)

    def body(j, carry):
        h = h_ref[...]
        g = _dot(h, wg_ref[j])
        u = _dot(h, wu_ref[j])
        a = (g * jax.nn.sigmoid(g) * u).astype(BF16)
        acc_ref[...] += _dot(a, wd_ref[j])
        return carry

    lax.fori_loop(0, wg_ref.shape[0], body, 0)
    o_ref[...] = x + 0.5 * acc_ref[...]


def _ffn(x, gain, w_gate, w_up, w_down, *, tm=512, tf=256):
    t, d = x.shape
    d_ff = w_gate.shape[1]
    nc = d_ff // tf
    wg = w_gate.astype(BF16).reshape(d, nc, tf).transpose(1, 0, 2)
    wu = w_up.astype(BF16).reshape(d, nc, tf).transpose(1, 0, 2)
    wd = w_down.astype(BF16).reshape(nc, tf, d)
    return pl.pallas_call(
        _ffn_kernel,
        grid=(t // tm,),
        in_specs=[
            pl.BlockSpec((tm, d), lambda i: (i, 0)),
            _resident((1, d)),
            _resident((nc, d, tf)),
            _resident((nc, d, tf)),
            _resident((nc, tf, d)),
        ],
        out_specs=pl.BlockSpec((tm, d), lambda i: (i, 0)),
        out_shape=jax.ShapeDtypeStruct((t, d), F32),
        scratch_shapes=[pltpu.VMEM((tm, d), BF16), pltpu.VMEM((tm, d), F32)],
        compiler_params=_params(1, 48),
        name="ffn",
    )(x, gain.reshape(1, d), wg, wu, wd)


def _ab_proj_kernel(x_ref, g_ref, w_ref, qn_ref, kn_ref, cos_ref, sa_ref, sb_ref, o_ref, *,
                    n_qk_groups, q_scale_groups):
    h = _rms(x_ref[...], g_ref[...]).astype(BF16)
    cos, sa, sb = cos_ref[...], sa_ref[...], sb_ref[...]
    n_groups = w_ref.shape[1] // LANES
    lane = _lane_iota((x_ref.shape[0], LANES))
    low = lane < DIFF_HEAD_DIM
    for c in range(0, n_groups, 2):
        y2 = _dot(h, w_ref[:, c * LANES:(c + 2) * LANES])
        for k in range(2):
            g = c + k
            y = y2[:, k * LANES:(k + 1) * LANES]
            if g < 2 * n_qk_groups:
                gain = qn_ref[...] if g < n_qk_groups else kn_ref[...]
                sq = y * y
                ss_lo = jnp.sum(jnp.where(low, sq, 0.0), axis=-1, keepdims=True)
                ss_hi = jnp.sum(jnp.where(low, 0.0, sq), axis=-1, keepdims=True)
                ms = jnp.where(low, ss_lo, ss_hi) * (1.0 / DIFF_HEAD_DIM)
                y = y * lax.rsqrt(ms + NORM_EPS) * gain
                y = _rope(y, cos, sa, sb, DIFF_HEAD_DIM // 2)
            if g in q_scale_groups:
                y = y * (DIFF_HEAD_DIM ** -0.5)
            o_ref[:, g * LANES:(g + 1) * LANES] = y.astype(BF16)


def _ab_proj(x, gain, w_in, q_norm, k_norm, tables, seq, *, tm=512):
    t, d = x.shape
    n = w_in.shape[1]
    n_qk_groups = 2 * DIFF_HEADS * DIFF_HEAD_DIM // LANES
    sb_q0 = (4 * DIFF_HEADS * DIFF_HEAD_DIM + DIFF_HEADS * 2 * DIFF_HEAD_DIM) // LANES
    q_scale_groups = tuple(range(n_qk_groups)) + tuple(
        range(sb_q0, sb_q0 + SB_HEADS * SB_HEAD_DIM // LANES))
    cos, sa, sb = tables
    n_seq_tiles = seq // tm
    tab_spec = pl.BlockSpec((tm, LANES), lambda i: (i % n_seq_tiles, 0))
    kern = functools.partial(_ab_proj_kernel, n_qk_groups=n_qk_groups,
                             q_scale_groups=q_scale_groups)
    return pl.pallas_call(
        kern,
        grid=(t // tm,),
        in_specs=[
            pl.BlockSpec((tm, d), lambda i: (i, 0)),
            _resident((1, d)),
            _resident((d, n)),
            _resident((1, LANES)),
            _resident((1, LANES)),
            tab_spec, tab_spec, tab_spec,
        ],
        out_specs=pl.BlockSpec((tm, n), lambda i: (i, 0)),
        out_shape=jax.ShapeDtypeStruct((t, n), BF16),
        compiler_params=_params(1, 40),
        name="ab_proj",
    )(x, gain.reshape(1, d), w_in.astype(BF16),
      jnp.tile(q_norm, 2).reshape(1, LANES), jnp.tile(k_norm, 2).reshape(1, LANES),
      cos, sa, sb)


def _softmax_step(s, m_ref, l_ref, acc_ref, v):
    m_old = m_ref[...]
    m_new = jnp.maximum(m_old, jnp.max(s, axis=-1, keepdims=True))
    alpha = jnp.exp(m_old - m_new)
    p = jnp.exp(s - m_new)
    l_ref[...] = alpha * l_ref[...] + jnp.sum(p, axis=-1, keepdims=True)
    acc_ref[...] = alpha * acc_ref[...] + _dot(p.astype(BF16), v)
    m_ref[...] = m_new


def _chunk_causal_mask(tq, tk):
    qc = lax.broadcasted_iota(jnp.int32, (tq, tk), 0) // CHUNK
    kc = lax.broadcasted_iota(jnp.int32, (tq, tk), 1) // CHUNK
    return kc <= qc


def _diff_attn_kernel(q1_ref, q2_ref, k1_ref, k2_ref, v_ref, lam_ref, sub_ref, o_ref,
                      m1_ref, l1_ref, a1_ref, m2_ref, l2_ref, a2_ref, *, tq, lambda_init):
    j = pl.program_id(1)
    qi = pl.program_id(2)
    lane = _lane_iota((tq, LANES))
    mine = (lane >= DIFF_HEAD_DIM).astype(jnp.int32) == j % 2
    q1 = jnp.where(mine, q1_ref[0].astype(F32), 0.0).astype(BF16)
    q2 = jnp.where(mine, q2_ref[0].astype(F32), 0.0).astype(BF16)
    for m_ref, l_ref, a_ref in ((m1_ref, l1_ref, a1_ref), (m2_ref, l2_ref, a2_ref)):
        m_ref[...] = jnp.full_like(m_ref, -jnp.inf)
        l_ref[...] = jnp.zeros_like(l_ref)
        a_ref[...] = jnp.zeros_like(a_ref)

    def block(kb, masked):
        rows = pl.ds(pl.multiple_of(kb * tq, tq), tq)
        v = v_ref[0, rows, :]
        s1 = _dot_nt(q1, k1_ref[0, rows, :])
        s2 = _dot_nt(q2, k2_ref[0, rows, :])
        if masked:
            keep = _chunk_causal_mask(tq, tq)
            s1 = jnp.where(keep, s1, -jnp.inf)
            s2 = jnp.where(keep, s2, -jnp.inf)
        _softmax_step(s1, m1_ref, l1_ref, a1_ref, v)
        _softmax_step(s2, m2_ref, l2_ref, a2_ref, v)

    def body(kb, carry):
        block(kb, False)
        return carry

    lax.fori_loop(0, qi, body, 0)
    block(qi, True)

    lam_v = lam_ref[...]
    lam = (jnp.exp(jnp.sum(lam_v[0:1] * lam_v[1:2], axis=-1, keepdims=True))
           - jnp.exp(jnp.sum(lam_v[2:3] * lam_v[3:4], axis=-1, keepdims=True)) + lambda_init)
    o = a1_ref[...] / l1_ref[...] - lam * (a2_ref[...] / l2_ref[...])
    o = _rms(o, sub_ref[...]) * (1.0 - lambda_init)
    o_ref[0] = o.astype(BF16)


def _diff_attn(proj, lam_rows, subln, lambda_init, *, tq=256):
    b, s, _ = proj.shape
    k_off = 2 * DIFF_HEADS * DIFF_HEAD_DIM // LANES
    v_off = 2 * k_off
    half = DIFF_HEADS // 2
    kern = functools.partial(_diff_attn_kernel, tq=tq, lambda_init=lambda_init)
    stat = pltpu.VMEM((tq, 1), F32)
    acc = pltpu.VMEM((tq, LANES), F32)
    return pl.pallas_call(
        kern,
        grid=(b, DIFF_HEADS, s // tq),
        in_specs=[
            pl.BlockSpec((1, tq, LANES), lambda bi, j, qi: (bi, qi, j // 2)),
            pl.BlockSpec((1, tq, LANES), lambda bi, j, qi: (bi, qi, half + j // 2)),
            pl.BlockSpec((1, s, LANES), lambda bi, j, qi: (bi, 0, k_off + j // 2)),
            pl.BlockSpec((1, s, LANES), lambda bi, j, qi: (bi, 0, k_off + half + j // 2)),
            pl.BlockSpec((1, s, LANES), lambda bi, j, qi: (bi, 0, v_off + j)),
            _resident((4, DIFF_HEAD_DIM)),
            _resident((1, LANES)),
        ],
        out_specs=pl.BlockSpec((1, tq, LANES), lambda bi, j, qi: (bi, qi, j)),
        out_shape=jax.ShapeDtypeStruct((b, s, DIFF_HEADS * LANES), BF16),
        scratch_shapes=[stat, stat, acc, stat, stat, acc],
        compiler_params=_params(3, 32),
        name="diff_attn",
    )(proj, proj, proj, proj, proj, lam_rows, subln.reshape(1, LANES))


def _split3(x):
    hi = x.astype(BF16)
    r = x - hi.astype(F32)
    mid = r.astype(BF16)
    lo = (r - mid.astype(F32)).astype(BF16)
    return hi, mid, lo


def _sb_attn_kernel(q_ref, k_ref, v_ref, o_ref, c0_ref, c1_ref, acc_ref, *, tq):
    qi = pl.program_id(2)
    lane = _lane_iota((tq, LANES))
    low = lane < SB_HEAD_DIM
    q = q_ref[0].astype(F32)
    q_heads = (jnp.where(low, q, 0.0).astype(BF16), jnp.where(low, 0.0, q).astype(BF16))
    c_refs = (c0_ref, c1_ref)
    c0_ref[...] = jnp.zeros_like(c0_ref)
    c1_ref[...] = jnp.zeros_like(c1_ref)
    acc_ref[...] = jnp.zeros_like(acc_ref)
    row = lax.broadcasted_iota(jnp.int32, (tq, tq), 0)
    col = lax.broadcasted_iota(jnp.int32, (tq, tq), 1)
    strict = col < row
    later_than = (row > col).astype(BF16)

    def block(kb, diagonal):
        rows = pl.ds(pl.multiple_of(kb * tq, tq), tq)
        k = k_ref[0, rows, :]
        v = v_ref[0, rows, :]
        outs = []
        for h in range(2):
            z = _dot_nt(q_heads[h], k)
            log_beta = jnp.minimum(z, 0.0) - jnp.log1p(jnp.exp(-jnp.abs(z)))
            log_keep = log_beta - z
            if diagonal:
                log_keep = jnp.where(strict, log_keep, 0.0)
            hi, mid, lo = _split3(log_keep)
            later = _dot(hi, later_than) + _dot(mid, later_than) + _dot(lo, later_than)
            c_old = c_refs[h][...]
            w = jnp.exp(log_beta + later + c_old)
            if diagonal:
                w = jnp.where(strict, w, 0.0)
            outs.append(_dot(w.astype(BF16), v))
            c_refs[h][...] = c_old + jnp.sum(log_keep, axis=-1, keepdims=True)
        acc_ref[...] += jnp.where(low, outs[0], outs[1])

    def live():
        return jnp.max(jnp.maximum(c0_ref[...], c1_ref[...])) > SB_LOG_ZERO

    block(qi, True)

    def cond(state):
        kb, go = state
        return jnp.logical_and(kb >= 0, go)

    def body(state):
        kb, _ = state
        block(kb, False)
        return kb - 1, live()

    lax.while_loop(cond, body, (qi - 1, live()))
    o_ref[0] = acc_ref[...].astype(BF16)


def _sb_attn(proj, *, tq=256):
    b, s, _ = proj.shape
    q_off = (4 * DIFF_HEADS * DIFF_HEAD_DIM + DIFF_HEADS * 2 * DIFF_HEAD_DIM) // LANES
    n_pairs = SB_HEADS * SB_HEAD_DIM // LANES
    kern = functools.partial(_sb_attn_kernel, tq=tq)
    return pl.pallas_call(
        kern,
        grid=(b, n_pairs, s // tq),
        in_specs=[
            pl.BlockSpec((1, tq, LANES), lambda bi, p, qi: (bi, qi, q_off + p)),
            pl.BlockSpec((1, s, LANES), lambda bi, p, qi: (bi, 0, q_off + n_pairs + p)),
            pl.BlockSpec((1, s, LANES), lambda bi, p, qi: (bi, 0, q_off + 2 * n_pairs + p)),
        ],
        out_specs=pl.BlockSpec((1, tq, LANES), lambda bi, p, qi: (bi, qi, p)),
        out_shape=jax.ShapeDtypeStruct((b, s, n_pairs * LANES), BF16),
        scratch_shapes=[pltpu.VMEM((tq, 1), F32), pltpu.VMEM((tq, 1), F32),
                        pltpu.VMEM((tq, LANES), F32)],
        compiler_params=_params(3, 32),
        name="sb_attn",
    )(proj, proj, proj)


def _mla_proj_kernel(x_ref, g_ref, wdq_ref, qln_ref, wuq_ref, wdkv_ref, kvln_ref, wuk_ref,
                     wuv_ref, gq_ref, gk_ref, cos_ref, sa_ref, sb_ref, q_ref, k_ref, v_ref):
    h = _rms(x_ref[...], g_ref[...]).astype(BF16)
    cos, sa, sb = cos_ref[...], sa_ref[...], sb_ref[...]
    half = MLA_ROPE_DIM // 2
    inv_dim = 1.0 / MLA_QK_DIM
    c_q = _rms(_dot(h, wdq_ref[...]), qln_ref[...]).astype(BF16)
    dkv = _dot(h, wdkv_ref[...])
    c_kv = _rms(dkv[:, :MLA_KV_RANK], kvln_ref[...]).astype(BF16)
    k_rope = dkv[:, MLA_KV_RANK:]
    gq, gk = gq_ref[...], gk_ref[...]
    k_rope_sq = jnp.sum(k_rope * k_rope, axis=-1, keepdims=True)
    k_rope_rot = _rope(k_rope * gk, cos, sa, sb, half)
    v_ref[...] = _dot(c_kv, wuv_ref[...]).astype(BF16)
    for hd in range(0, MLA_HEADS, 2):
        cols = slice(hd * LANES, (hd + 2) * LANES)
        q2 = _dot(c_q, wuq_ref[:, cols])
        k2 = _dot(c_kv, wuk_ref[:, cols])
        for i in range(2):
            lanes = slice(i * LANES, (i + 1) * LANES)
            out = slice((hd + i) * LANES, (hd + i + 1) * LANES)
            q = q2[:, lanes]
            q = q * lax.rsqrt(jnp.sum(q * q, axis=-1, keepdims=True) * inv_dim + NORM_EPS) * gq
            q = _rope(q, cos, sa, sb, half) * (MLA_QK_DIM ** -0.5)
            q_ref[:, out] = q.astype(BF16)
            kn = k2[:, lanes]
            ms = (jnp.sum(kn * kn, axis=-1, keepdims=True) + k_rope_sq) * inv_dim
            k = (kn * gk + k_rope_rot) * lax.rsqrt(ms + NORM_EPS)
            k_ref[:, out] = k.astype(BF16)


def _pad_heads(w, n_heads, width):
    r = w.shape[0]
    w = w.reshape(r, n_heads, width)
    return jnp.pad(w, ((0, 0), (0, 0), (0, LANES - width))).reshape(r, n_heads * LANES)


def _mla_proj(x, gain, w_dq, q_lat_norm, w_uq, w_dkv, kv_lat_norm, w_ukv, qk_norm_q, qk_norm_k,
              tables, seq, *, tm=512):
    t, d = x.shape
    q_rank = w_dq.shape[1]
    wuq = _pad_heads(w_uq, MLA_HEADS, MLA_QK_DIM).astype(BF16)
    ukv = w_ukv.reshape(MLA_KV_RANK, MLA_HEADS, MLA_NOPE_DIM + MLA_V_DIM)
    wuk = _pad_heads(ukv[:, :, :MLA_NOPE_DIM].reshape(MLA_KV_RANK, -1), MLA_HEADS,
                     MLA_NOPE_DIM).astype(BF16)
    wuv = ukv[:, :, MLA_NOPE_DIM:].reshape(MLA_KV_RANK, MLA_HEADS * MLA_V_DIM).astype(BF16)
    rope_cols = jnp.pad(w_dkv[:, MLA_KV_RANK:],
                        ((0, 0), (MLA_NOPE_DIM, LANES - MLA_QK_DIM)))
    wdkv = jnp.concatenate([w_dkv[:, :MLA_KV_RANK], rope_cols], axis=1).astype(BF16)
    pad_gain = lambda g: jnp.pad(g, (0, LANES - MLA_QK_DIM)).reshape(1, LANES)
    cos, sa, sb = tables
    n_seq_tiles = seq // tm
    tab_spec = pl.BlockSpec((tm, LANES), lambda i: (i % n_seq_tiles, 0))
    hw = MLA_HEADS * LANES
    vw = MLA_HEADS * MLA_V_DIM
    row = lambda w: pl.BlockSpec((tm, w), lambda i: (i, 0))
    return pl.pallas_call(
        _mla_proj_kernel,
        grid=(t // tm,),
        in_specs=[
            row(d), _resident((1, d)),
            _resident((d, q_rank)), _resident((1, q_rank)), _resident((q_rank, hw)),
            _resident((d, MLA_KV_RANK + LANES)), _resident((1, MLA_KV_RANK)),
            _resident((MLA_KV_RANK, hw)), _resident((MLA_KV_RANK, vw)),
            _resident((1, LANES)), _resident((1, LANES)),
            tab_spec, tab_spec, tab_spec,
        ],
        out_specs=[row(hw), row(hw), row(vw)],
        out_shape=[jax.ShapeDtypeStruct((t, hw), BF16), jax.ShapeDtypeStruct((t, hw), BF16),
                   jax.ShapeDtypeStruct((t, vw), BF16)],
        compiler_params=_params(1, 40),
        name="mla_proj",
    )(x, gain.reshape(1, d), w_dq.astype(BF16), q_lat_norm.reshape(1, q_rank), wuq, wdkv,
      kv_lat_norm.reshape(1, MLA_KV_RANK), wuk, wuv, pad_gain(qk_norm_q), pad_gain(qk_norm_k),
      cos, sa, sb)


def _mla_attn_kernel(q_ref, k_ref, v_ref, o_ref, m0_ref, m1_ref, l0_ref, l1_ref, acc_ref, *, tq):
    qi = pl.program_id(2)
    lane = _lane_iota((tq, LANES))
    low = lane < MLA_V_DIM
    m_refs, l_refs = (m0_ref, m1_ref), (l0_ref, l1_ref)
    for h in range(2):
        m_refs[h][...] = jnp.full_like(m0_ref, -jnp.inf)
        l_refs[h][...] = jnp.zeros_like(l0_ref)
    acc_ref[...] = jnp.zeros_like(acc_ref)

    def block(kb, masked):
        rows = pl.ds(pl.multiple_of(kb * tq, tq), tq)
        v = v_ref[0, rows, :]
        pvs, alphas = [], []
        for h in range(2):
            lanes = slice(h * LANES, (h + 1) * LANES)
            s = _dot_nt(q_ref[0, :, lanes], k_ref[0, rows, lanes])
            if masked:
                s = jnp.where(_chunk_causal_mask(tq, tq), s, -jnp.inf)
            m_old = m_refs[h][...]
            m_new = jnp.maximum(m_old, jnp.max(s, axis=-1, keepdims=True))
            alpha = jnp.exp(m_old - m_new)
            p = jnp.exp(s - m_new)
            l_refs[h][...] = alpha * l_refs[h][...] + jnp.sum(p, axis=-1, keepdims=True)
            m_refs[h][...] = m_new
            pvs.append(_dot(p.astype(BF16), v))
            alphas.append(alpha)
        acc_ref[...] = (jnp.where(low, alphas[0], alphas[1]) * acc_ref[...]
                        + jnp.where(low, pvs[0], pvs[1]))

    def body(kb, carry):
        block(kb, False)
        return carry

    lax.fori_loop(0, qi, body, 0)
    block(qi, True)
    o_ref[0] = (acc_ref[...] / jnp.where(low, l0_ref[...], l1_ref[...])).astype(BF16)


def _mla_attn(q, k, v, *, tq=256):
    b, s, _ = q.shape
    n_pairs = MLA_HEADS // 2
    kern = functools.partial(_mla_attn_kernel, tq=tq)
    stat = pltpu.VMEM((tq, 1), F32)
    return pl.pallas_call(
        kern,
        grid=(b, n_pairs, s // tq),
        in_specs=[
            pl.BlockSpec((1, tq, 2 * LANES), lambda bi, p, qi: (bi, qi, p)),
            pl.BlockSpec((1, s, 2 * LANES), lambda bi, p, qi: (bi, 0, p)),
            pl.BlockSpec((1, s, LANES), lambda bi, p, qi: (bi, 0, p)),
        ],
        out_specs=pl.BlockSpec((1, tq, LANES), lambda bi, p, qi: (bi, qi, p)),
        out_shape=jax.ShapeDtypeStruct((b, s, n_pairs * LANES), BF16),
        scratch_shapes=[stat, stat, stat, stat, pltpu.VMEM((tq, LANES), F32)],
        compiler_params=_params(3, 32),
        name="mla_attn",
    )(q, k, v)


def _mem_kv_kernel(m_ref, g_ref, w_ref, kn_ref, k_ref, v_ref):
    h = _rms(m_ref[0], g_ref[...]).astype(BF16)
    kv = _dot(h, w_ref[...])
    width = XM_HEADS * XM_HEAD_DIM
    for hd in range(XM_HEADS):
        lanes = slice(hd * LANES, (hd + 1) * LANES)
        k_ref[0, :, lanes] = _rms(kv[:, lanes], kn_ref[...]).astype(BF16)
    v_ref[0] = kv[:, width:].astype(BF16)


def _mem_kv(mem, gain, w_kv, k_norm):
    b, m, d = mem.shape
    width = XM_HEADS * XM_HEAD_DIM
    out = jax.ShapeDtypeStruct((b, m, width), BF16)
    blk = pl.BlockSpec((1, m, width), lambda i: (i, 0, 0))
    return pl.pallas_call(
        _mem_kv_kernel,
        grid=(b,),
        in_specs=[pl.BlockSpec((1, m, d), lambda i: (i, 0, 0)), _resident((1, d)),
                  _resident((d, 2 * width)), _resident((1, XM_HEAD_DIM))],
        out_specs=[blk, blk],
        out_shape=[out, out],
        compiler_params=_params(1, 32),
        name="mem_kv",
    )(mem, gain.reshape(1, d), w_kv.astype(BF16), k_norm.reshape(1, XM_HEAD_DIM))


def _post_kernel(*refs, n_mix):
    x_ref = refs[0]
    mix_refs = refs[1:1 + n_mix]
    wout_ref, g_ref, wq_ref, qn_ref, k_ref, v_ref, wo_ref, o_ref = refs[1 + n_mix:]
    x = x_ref[...]
    off = 0
    for a_ref in mix_refs:
        w = a_ref.shape[1]
        x = x + _dot(a_ref[...], wout_ref[off:off + w, :])
        off += w
    h = _rms(x, g_ref[...]).astype(BF16)
    q = _dot(h, wq_ref[...])
    heads = []
    for hd in range(XM_HEADS):
        lanes = slice(hd * LANES, (hd + 1) * LANES)
        qh = (_rms(q[:, lanes], qn_ref[...]) * (XM_HEAD_DIM ** -0.5)).astype(BF16)
        s = _dot_nt(qh, k_ref[0, :, lanes])
        p = jnp.exp(s - jnp.max(s, axis=-1, keepdims=True))
        l = jnp.sum(p, axis=-1, keepdims=True)
        heads.append((_dot(p.astype(BF16), v_ref[0, :, lanes]) / l).astype(BF16))
    o = jnp.concatenate(heads, axis=-1)
    o_ref[...] = x + _dot(o, wo_ref[...])


def _post(x, mixes, w_out, gain, w_q, q_norm, mem_k, mem_v, w_o, seq, *, tm=512):
    t, d = x.shape
    b, m, width = mem_k.shape
    n_seq_tiles = seq // tm
    row = lambda w: pl.BlockSpec((tm, w), lambda i: (i, 0))
    mem_spec = pl.BlockSpec((1, m, width), lambda i: (i // n_seq_tiles, 0, 0))
    kern = functools.partial(_post_kernel, n_mix=len(mixes))
    return pl.pallas_call(
        kern,
        grid=(t // tm,),
        in_specs=[row(d)] + [row(a.shape[1]) for a in mixes] + [
            _resident(w_out.shape), _resident((1, d)), _resident((d, width)),
            _resident((1, XM_HEAD_DIM)), mem_spec, mem_spec, _resident((width, d))],
        out_specs=row(d),
        out_shape=jax.ShapeDtypeStruct((t, d), F32),
        compiler_params=_params(1, 40),
        name="post",
    )(x, *mixes, w_out.astype(BF16), gain.reshape(1, d), w_q.astype(BF16),
      q_norm.reshape(1, XM_HEAD_DIM), mem_k, mem_v, w_o.astype(BF16))


def _rope_tables(seq, dim, start, group):
    half = dim // 2
    inv = 1.0 / (ROPE_THETA ** (jnp.arange(0, dim, 2, dtype=F32) / dim))
    ang = jnp.arange(seq, dtype=F32)[:, None] * inv[None, :]
    cos, sin = jnp.cos(ang), jnp.sin(ang)
    zeros = jnp.zeros_like(sin)
    pad = lambda a, fill: jnp.concatenate(
        [jnp.full((seq, start), fill, F32), a, jnp.full((seq, group - start - dim), fill, F32)],
        axis=1)
    cos_t = pad(jnp.concatenate([cos, cos], axis=1), 1.0)
    sin_a = pad(jnp.concatenate([zeros, sin], axis=1), 0.0)
    sin_b = pad(jnp.concatenate([-sin, zeros], axis=1), 0.0)
    reps = LANES // group
    return tuple(jnp.tile(a, (1, reps)) for a in (cos_t, sin_a, sin_b))


def kernel(x, mem, ffn_norm, ffn_w_gate, ffn_w_up, ffn_w_down, mix_norm, ab_w_in, ab_w_out, diff_q_norm, diff_k_norm, diff_lambda_q1, diff_lambda_k1, diff_lambda_q2, diff_lambda_k2, diff_subln, mla_w_dq, mla_q_norm, mla_w_uq, mla_w_dkv, mla_kv_norm, mla_w_ukv, mla_qk_norm_q, mla_qk_norm_k, mla_w_o, xm_norm, xm_mem_norm, xm_w_q, xm_w_kv, xm_q_norm, xm_k_norm, xm_w_o):
    b, s, d = x.shape
    depth = ffn_norm.shape[0]
    x = x.reshape(b * s, d)
    diff_tables = _rope_tables(s, DIFF_HEAD_DIM, 0, DIFF_HEAD_DIM)
    mla_tables = _rope_tables(s, MLA_ROPE_DIM, MLA_NOPE_DIM, LANES)
    for layer in range(depth):
        i = layer // 2
        x = _ffn(x, ffn_norm[layer, 0], ffn_w_gate[layer, 0], ffn_w_up[layer, 0],
                 ffn_w_down[layer, 0])
        if layer % 2 == 0:
            lambda_init = 0.8 - 0.6 * math.exp(-0.3 * layer)
            proj = _ab_proj(x, mix_norm[layer], ab_w_in[i], diff_q_norm[i], diff_k_norm[i],
                            diff_tables, s).reshape(b, s, -1)
            lam_rows = jnp.stack([diff_lambda_q1[i], diff_lambda_k1[i],
                                  diff_lambda_q2[i], diff_lambda_k2[i]])
            oa = _diff_attn(proj, lam_rows, diff_subln[i], lambda_init)
            ob = _sb_attn(proj)
            mixes = [oa.reshape(b * s, -1), ob.reshape(b * s, -1)]
            w_out = ab_w_out[i]
        else:
            q, k, v = _mla_proj(x, mix_norm[layer], mla_w_dq[i], mla_q_norm[i], mla_w_uq[i],
                                mla_w_dkv[i], mla_kv_norm[i], mla_w_ukv[i], mla_qk_norm_q[i],
                                mla_qk_norm_k[i], mla_tables, s)
            o = _mla_attn(q.reshape(b, s, -1), k.reshape(b, s, -1), v.reshape(b, s, -1))
            mixes = [o.reshape(b * s, -1)]
            w_out = mla_w_o[i]
        mem_k, mem_v = _mem_kv(mem, xm_mem_norm[layer], xm_w_kv[layer], xm_k_norm[layer])
        x = _post(x, mixes, w_out, xm_norm[layer], xm_w_q[layer], xm_q_norm[layer],
                  mem_k, mem_v, xm_w_o[layer], s)
        x = _ffn(x, ffn_norm[layer, 1], ffn_w_gate[layer, 1], ffn_w_up[layer, 1],
                 ffn_w_down[layer, 1])
    return x.reshape(b, s, d)
```

```python
import functools
import math

import jax
import jax.numpy as jnp
from jax import lax
from jax.experimental import pallas as pl
from jax.experimental.pallas import tpu as pltpu

F32 = jnp.float32
BF16 = jnp.bfloat16

LANES = 128
V7X_VMEM_LIMIT = 56 * 1024 * 1024

CHUNK = 64
ROPE_THETA = 10000.0
NORM_EPS = 1e-6
SB_LOG_ZERO = -104.0
MAX_SOFTMAX_SHIFT = 40.0

DIFF_HEADS = 4
DIFF_HEAD_DIM = 64
SB_HEADS = 8
SB_HEAD_DIM = 64
MLA_HEADS = 16
MLA_KV_RANK = 256
MLA_NOPE_DIM = 64
MLA_ROPE_DIM = 32
MLA_V_DIM = 64
MLA_QK_DIM = MLA_NOPE_DIM + MLA_ROPE_DIM
XM_HEADS = 4
XM_HEAD_DIM = 128


def _params(n_grid, vmem_mb):
    return pltpu.CompilerParams(
        dimension_semantics=("arbitrary",) * n_grid,
        vmem_limit_bytes=min(vmem_mb * 1024 * 1024, V7X_VMEM_LIMIT))


def _resident(shape):
    nd = len(shape)
    return pl.BlockSpec(shape, lambda *_: (0,) * nd, pipeline_mode=pl.Buffered(1))


def _rms(x, gain):
    return x * lax.rsqrt(jnp.mean(x * x, axis=-1, keepdims=True) + NORM_EPS) * gain


def _dot(a, b):
    return jnp.dot(a, b, preferred_element_type=F32)


def _dot_nt(a, b):
    return lax.dot_general(a, b, (((1,), (1,)), ((), ())), preferred_element_type=F32)


def _lane_iota(shape):
    return lax.broadcasted_iota(jnp.int32, shape, len(shape) - 1)


def _rope(x, cos, sin_a, sin_b, half):
    return x * cos + pltpu.roll(x, half, 1) * sin_a + pltpu.roll(x, LANES - half, 1) * sin_b


def _ffn_kernel(x_ref, g_ref, wg_ref, wu_ref, wd_ref, o_ref, h_ref, acc_ref):
    x = x_ref[...]
    h_ref[...] = _rms(x, g_ref[...]).astype(BF16)
    acc_ref[...] = jnp.zeros_like(acc_ref)

    def body(j, carry):
        h = h_ref[...]
        g = _dot(h, wg_ref[j])
        u = _dot(h, wu_ref[j])
        a = (g * jax.nn.sigmoid(g) * u).astype(BF16)
        acc_ref[...] += _dot(a, wd_ref[j])
        return carry

    lax.fori_loop(0, wg_ref.shape[0], body, 0)
    o_ref[...] = x + 0.5 * acc_ref[...]


def _ffn(x, gain, w_gate, w_up, w_down, *, tm=512, tf=256):
    t, d = x.shape
    d_ff = w_gate.shape[1]
    nc = d_ff // tf
    wg = w_gate.astype(BF16).reshape(d, nc, tf).transpose(1, 0, 2)
    wu = w_up.astype(BF16).reshape(d, nc, tf).transpose(1, 0, 2)
    wd = w_down.astype(BF16).reshape(nc, tf, d)
    return pl.pallas_call(
        _ffn_kernel,
        grid=(t // tm,),
        in_specs=[
            pl.BlockSpec((tm, d), lambda i: (i, 0)),
            _resident((1, d)),
            _resident((nc, d, tf)),
            _resident((nc, d, tf)),
            _resident((nc, tf, d)),
        ],
        out_specs=pl.BlockSpec((tm, d), lambda i: (i, 0)),
        out_shape=jax.ShapeDtypeStruct((t, d), F32),
        scratch_shapes=[pltpu.VMEM((tm, d), BF16), pltpu.VMEM((tm, d), F32)],
        compiler_params=_params(1, 48),
        name="ffn",
    )(x, gain.reshape(1, d), wg, wu, wd)


def _ab_proj_kernel(x_ref, g_ref, w_ref, qn_ref, kn_ref, cos_ref, sa_ref, sb_ref, o_ref, *,
                    n_qk_groups, q_scale_groups):
    h = _rms(x_ref[...], g_ref[...]).astype(BF16)
    cos, sa, sb = cos_ref[...], sa_ref[...], sb_ref[...]
    n_groups = w_ref.shape[1] // LANES
    lane = _lane_iota((x_ref.shape[0], LANES))
    low = lane < DIFF_HEAD_DIM
    for c in range(0, n_groups, 2):
        y2 = _dot(h, w_ref[:, c * LANES:(c + 2) * LANES])
        for k in range(2):
            g = c + k
            y = y2[:, k * LANES:(k + 1) * LANES]
            if g < 2 * n_qk_groups:
                gain = qn_ref[...] if g < n_qk_groups else kn_ref[...]
                sq = y * y
                ss_lo = jnp.sum(jnp.where(low, sq, 0.0), axis=-1, keepdims=True)
                ss_hi = jnp.sum(jnp.where(low, 0.0, sq), axis=-1, keepdims=True)
                ms = jnp.where(low, ss_lo, ss_hi) * (1.0 / DIFF_HEAD_DIM)
                y = y * lax.rsqrt(ms + NORM_EPS) * gain
                y = _rope(y, cos, sa, sb, DIFF_HEAD_DIM // 2)
            if g in q_scale_groups:
                y = y * (DIFF_HEAD_DIM ** -0.5)
            o_ref[:, g * LANES:(g + 1) * LANES] = y.astype(BF16)


def _ab_proj(x, gain, w_in, q_norm, k_norm, tables, seq, *, tm=512):
    t, d = x.shape
    n = w_in.shape[1]
    n_qk_groups = 2 * DIFF_HEADS * DIFF_HEAD_DIM // LANES
    sb_q0 = (4 * DIFF_HEADS * DIFF_HEAD_DIM + DIFF_HEADS * 2 * DIFF_HEAD_DIM) // LANES
    q_scale_groups = tuple(range(n_qk_groups)) + tuple(
        range(sb_q0, sb_q0 + SB_HEADS * SB_HEAD_DIM // LANES))
    cos, sa, sb = tables
    n_seq_tiles = seq // tm
    tab_spec = pl.BlockSpec((tm, LANES), lambda i: (i % n_seq_tiles, 0))
    kern = functools.partial(_ab_proj_kernel, n_qk_groups=n_qk_groups,
                             q_scale_groups=q_scale_groups)
    return pl.pallas_call(
        kern,
        grid=(t // tm,),
        in_specs=[
            pl.BlockSpec((tm, d), lambda i: (i, 0)),
            _resident((1, d)),
            _resident((d, n)),
            _resident((1, LANES)),
            _resident((1, LANES)),
            tab_spec, tab_spec, tab_spec,
        ],
        out_specs=pl.BlockSpec((tm, n), lambda i: (i, 0)),
        out_shape=jax.ShapeDtypeStruct((t, n), BF16),
        compiler_params=_params(1, 40),
        name="ab_proj",
    )(x, gain.reshape(1, d), w_in.astype(BF16),
      jnp.tile(q_norm, 2).reshape(1, LANES), jnp.tile(k_norm, 2).reshape(1, LANES),
      cos, sa, sb)


def _chunk_causal_mask(tq, tk):
    qc = lax.broadcasted_iota(jnp.int32, (tq, tk), 0) // CHUNK
    kc = lax.broadcasted_iota(jnp.int32, (tq, tk), 1) // CHUNK
    return kc <= qc


def _fold_lanes(p):
    out = p[:, :LANES]
    for c in range(1, p.shape[1] // LANES):
        out = out + p[:, c * LANES:(c + 1) * LANES]
    return out


def _attend_pair(bound, qs, k_srcs, v_ref, acc_refs, l_refs, m_refs, qi, tq):
    def key_block(h, rows):
        ref, lanes = k_srcs[h]
        return ref[0, rows, lanes]

    def sweep(block):
        def body(kb, carry):
            block(kb, False)
            return carry
        lax.fori_loop(0, qi, body, 0)
        block(qi, True)

    for h in range(2):
        acc_refs[h][...] = jnp.zeros_like(acc_refs[h])
        l_refs[h][...] = jnp.zeros_like(l_refs[h])

    @pl.when(bound <= MAX_SOFTMAX_SHIFT)
    def _():
        def block(kb, masked):
            rows = pl.ds(pl.multiple_of(kb * tq, tq), tq)
            v = v_ref[0, rows, :]
            for h in range(2):
                p = jnp.exp(_dot_nt(qs[h], key_block(h, rows)) - bound)
                if masked:
                    p = jnp.where(_chunk_causal_mask(tq, tq), p, 0.0)
                l_refs[h][...] += _fold_lanes(p)
                acc_refs[h][...] += _dot(p.astype(BF16), v)

        sweep(block)
        for h in range(2):
            acc_refs[h][...] = acc_refs[h][...] / jnp.sum(l_refs[h][...], axis=-1, keepdims=True)

    @pl.when(bound > MAX_SOFTMAX_SHIFT)
    def _():
        for h in range(2):
            m_refs[h][...] = jnp.full_like(m_refs[h], -jnp.inf)

        def block(kb, masked):
            rows = pl.ds(pl.multiple_of(kb * tq, tq), tq)
            v = v_ref[0, rows, :]
            for h in range(2):
                s = _dot_nt(qs[h], key_block(h, rows))
                if masked:
                    s = jnp.where(_chunk_causal_mask(tq, tq), s, -jnp.inf)
                m_old = m_refs[h][...]
                m_new = jnp.maximum(m_old, jnp.max(s, axis=-1, keepdims=True))
                alpha = jnp.exp(m_old - m_new)
                p = jnp.exp(s - m_new)
                l_refs[h][...] = alpha * l_refs[h][...] + jnp.sum(p, axis=-1, keepdims=True)
                acc_refs[h][...] = alpha * acc_refs[h][...] + _dot(p.astype(BF16), v)
                m_refs[h][...] = m_new

        sweep(block)
        for h in range(2):
            acc_refs[h][...] = acc_refs[h][...] / l_refs[h][...]


def _pair_scratch(tq):
    acc = pltpu.VMEM((tq, LANES), F32)
    stat = pltpu.VMEM((tq, 1), F32)
    return [acc, acc, acc, acc, stat, stat]


def _score_bound(q_gain, k_gain, dim):
    return (jnp.max(jnp.abs(q_gain)) * jnp.max(jnp.abs(k_gain)) * math.sqrt(dim)).reshape(1)


def _diff_attn_kernel(bound_ref, q1_ref, q2_ref, k1_ref, k2_ref, v_ref, lam_ref, sub_ref, o_ref,
                      a1_ref, a2_ref, l1_ref, l2_ref, m1_ref, m2_ref, *, tq, lambda_init):
    j = pl.program_id(1)
    qi = pl.program_id(2)
    lane = _lane_iota((tq, LANES))
    mine = (lane >= DIFF_HEAD_DIM).astype(jnp.int32) == j % 2
    q1 = jnp.where(mine, q1_ref[0].astype(F32), 0.0).astype(BF16)
    q2 = jnp.where(mine, q2_ref[0].astype(F32), 0.0).astype(BF16)
    whole = slice(None)
    _attend_pair(bound_ref[0], (q1, q2), ((k1_ref, whole), (k2_ref, whole)), v_ref,
                 (a1_ref, a2_ref), (l1_ref, l2_ref), (m1_ref, m2_ref), qi, tq)
    lam_v = lam_ref[...]
    lam = (jnp.exp(jnp.sum(lam_v[0:1] * lam_v[1:2], axis=-1, keepdims=True))
           - jnp.exp(jnp.sum(lam_v[2:3] * lam_v[3:4], axis=-1, keepdims=True)) + lambda_init)
    o = a1_ref[...] - lam * a2_ref[...]
    o = _rms(o, sub_ref[...]) * (1.0 - lambda_init)
    o_ref[0] = o.astype(BF16)


def _diff_attn(proj, bound, lam_rows, subln, lambda_init, *, tq=512):
    b, s, _ = proj.shape
    k_off = 2 * DIFF_HEADS * DIFF_HEAD_DIM // LANES
    v_off = 2 * k_off
    half = DIFF_HEADS // 2
    kern = functools.partial(_diff_attn_kernel, tq=tq, lambda_init=lambda_init)
    return pl.pallas_call(
        kern,
        grid=(b, DIFF_HEADS, s // tq),
        in_specs=[
            pl.BlockSpec(memory_space=pltpu.SMEM),
            pl.BlockSpec((1, tq, LANES), lambda bi, j, qi: (bi, qi, j // 2)),
            pl.BlockSpec((1, tq, LANES), lambda bi, j, qi: (bi, qi, half + j // 2)),
            pl.BlockSpec((1, s, LANES), lambda bi, j, qi: (bi, 0, k_off + j // 2)),
            pl.BlockSpec((1, s, LANES), lambda bi, j, qi: (bi, 0, k_off + half + j // 2)),
            pl.BlockSpec((1, s, LANES), lambda bi, j, qi: (bi, 0, v_off + j)),
            _resident((4, DIFF_HEAD_DIM)),
            _resident((1, LANES)),
        ],
        out_specs=pl.BlockSpec((1, tq, LANES), lambda bi, j, qi: (bi, qi, j)),
        out_shape=jax.ShapeDtypeStruct((b, s, DIFF_HEADS * LANES), BF16),
        scratch_shapes=_pair_scratch(tq),
        compiler_params=_params(3, 40),
        name="diff_attn",
    )(bound, proj, proj, proj, proj, proj, lam_rows, subln.reshape(1, LANES))


def _split3(x):
    hi = x.astype(BF16)
    r = x - hi.astype(F32)
    mid = r.astype(BF16)
    lo = (r - mid.astype(F32)).astype(BF16)
    return hi, mid, lo


def _sb_attn_kernel(q_ref, k_ref, v_ref, o_ref, c0_ref, c1_ref, acc_ref, *, tq):
    qi = pl.program_id(2)
    lane = _lane_iota((tq, LANES))
    low = lane < SB_HEAD_DIM
    q = q_ref[0].astype(F32)
    q_heads = (jnp.where(low, q, 0.0).astype(BF16), jnp.where(low, 0.0, q).astype(BF16))
    c_refs = (c0_ref, c1_ref)
    c0_ref[...] = jnp.zeros_like(c0_ref)
    c1_ref[...] = jnp.zeros_like(c1_ref)
    acc_ref[...] = jnp.zeros_like(acc_ref)
    row = lax.broadcasted_iota(jnp.int32, (tq, tq), 0)
    col = lax.broadcasted_iota(jnp.int32, (tq, tq), 1)
    strict = col < row
    later_than = (row > col).astype(BF16)

    def block(kb, diagonal):
        rows = pl.ds(pl.multiple_of(kb * tq, tq), tq)
        k = k_ref[0, rows, :]
        v = v_ref[0, rows, :]
        outs = []
        for h in range(2):
            z = _dot_nt(q_heads[h], k)
            log_beta = jnp.minimum(z, 0.0) - jnp.log1p(jnp.exp(-jnp.abs(z)))
            log_keep = log_beta - z
            if diagonal:
                log_keep = jnp.where(strict, log_keep, 0.0)
            hi, mid, lo = _split3(log_keep)
            later = _dot(hi, later_than) + _dot(mid, later_than) + _dot(lo, later_than)
            c_old = c_refs[h][...]
            w = jnp.exp(log_beta + later + c_old)
            if diagonal:
                w = jnp.where(strict, w, 0.0)
            outs.append(_dot(w.astype(BF16), v))
            c_refs[h][...] = c_old + jnp.sum(log_keep, axis=-1, keepdims=True)
        acc_ref[...] += jnp.where(low, outs[0], outs[1])

    def live():
        return jnp.max(jnp.maximum(c0_ref[...], c1_ref[...])) > SB_LOG_ZERO

    block(qi, True)

    def cond(state):
        kb, go = state
        return jnp.logical_and(kb >= 0, go)

    def body(state):
        kb, _ = state
        block(kb, False)
        return kb - 1, live()

    lax.while_loop(cond, body, (qi - 1, live()))
    o_ref[0] = acc_ref[...].astype(BF16)


def _sb_attn(proj, *, tq=256):
    b, s, _ = proj.shape
    q_off = (4 * DIFF_HEADS * DIFF_HEAD_DIM + DIFF_HEADS * 2 * DIFF_HEAD_DIM) // LANES
    n_pairs = SB_HEADS * SB_HEAD_DIM // LANES
    kern = functools.partial(_sb_attn_kernel, tq=tq)
    return pl.pallas_call(
        kern,
        grid=(b, n_pairs, s // tq),
        in_specs=[
            pl.BlockSpec((1, tq, LANES), lambda bi, p, qi: (bi, qi, q_off + p)),
            pl.BlockSpec((1, s, LANES), lambda bi, p, qi: (bi, 0, q_off + n_pairs + p)),
            pl.BlockSpec((1, s, LANES), lambda bi, p, qi: (bi, 0, q_off + 2 * n_pairs + p)),
        ],
        out_specs=pl.BlockSpec((1, tq, LANES), lambda bi, p, qi: (bi, qi, p)),
        out_shape=jax.ShapeDtypeStruct((b, s, n_pairs * LANES), BF16),
        scratch_shapes=[pltpu.VMEM((tq, 1), F32), pltpu.VMEM((tq, 1), F32),
                        pltpu.VMEM((tq, LANES), F32)],
        compiler_params=_params(3, 32),
        name="sb_attn",
    )(proj, proj, proj)


def _mla_proj_kernel(x_ref, g_ref, wdq_ref, qln_ref, wuq_ref, wdkv_ref, kvln_ref, wuk_ref,
                     wuv_ref, gq_ref, gk_ref, cos_ref, sa_ref, sb_ref, q_ref, k_ref, v_ref):
    h = _rms(x_ref[...], g_ref[...]).astype(BF16)
    cos, sa, sb = cos_ref[...], sa_ref[...], sb_ref[...]
    half = MLA_ROPE_DIM // 2
    inv_dim = 1.0 / MLA_QK_DIM
    c_q = _rms(_dot(h, wdq_ref[...]), qln_ref[...]).astype(BF16)
    dkv = _dot(h, wdkv_ref[...])
    c_kv = _rms(dkv[:, :MLA_KV_RANK], kvln_ref[...]).astype(BF16)
    k_rope = dkv[:, MLA_KV_RANK:]
    gq, gk = gq_ref[...], gk_ref[...]
    k_rope_sq = jnp.sum(k_rope * k_rope, axis=-1, keepdims=True)
    k_rope_rot = _rope(k_rope * gk, cos, sa, sb, half)
    v_ref[...] = _dot(c_kv, wuv_ref[...]).astype(BF16)
    for hd in range(0, MLA_HEADS, 2):
        cols = slice(hd * LANES, (hd + 2) * LANES)
        q2 = _dot(c_q, wuq_ref[:, cols])
        k2 = _dot(c_kv, wuk_ref[:, cols])
        for i in range(2):
            lanes = slice(i * LANES, (i + 1) * LANES)
            out = slice((hd + i) * LANES, (hd + i + 1) * LANES)
            q = q2[:, lanes]
            q = q * lax.rsqrt(jnp.sum(q * q, axis=-1, keepdims=True) * inv_dim + NORM_EPS) * gq
            q = _rope(q, cos, sa, sb, half) * (MLA_QK_DIM ** -0.5)
            q_ref[:, out] = q.astype(BF16)
            kn = k2[:, lanes]
            ms = (jnp.sum(kn * kn, axis=-1, keepdims=True) + k_rope_sq) * inv_dim
            k = (kn * gk + k_rope_rot) * lax.rsqrt(ms + NORM_EPS)
            k_ref[:, out] = k.astype(BF16)


def _pad_heads(w, n_heads, width):
    r = w.shape[0]
    w = w.reshape(r, n_heads, width)
    return jnp.pad(w, ((0, 0), (0, 0), (0, LANES - width))).reshape(r, n_heads * LANES)


def _mla_proj(x, gain, w_dq, q_lat_norm, w_uq, w_dkv, kv_lat_norm, w_ukv, qk_norm_q, qk_norm_k,
              tables, seq, *, tm=512):
    t, d = x.shape
    q_rank = w_dq.shape[1]
    wuq = _pad_heads(w_uq, MLA_HEADS, MLA_QK_DIM).astype(BF16)
    ukv = w_ukv.reshape(MLA_KV_RANK, MLA_HEADS, MLA_NOPE_DIM + MLA_V_DIM)
    wuk = _pad_heads(ukv[:, :, :MLA_NOPE_DIM].reshape(MLA_KV_RANK, -1), MLA_HEADS,
                     MLA_NOPE_DIM).astype(BF16)
    wuv = ukv[:, :, MLA_NOPE_DIM:].reshape(MLA_KV_RANK, MLA_HEADS * MLA_V_DIM).astype(BF16)
    rope_cols = jnp.pad(w_dkv[:, MLA_KV_RANK:],
                        ((0, 0), (MLA_NOPE_DIM, LANES - MLA_QK_DIM)))
    wdkv = jnp.concatenate([w_dkv[:, :MLA_KV_RANK], rope_cols], axis=1).astype(BF16)
    pad_gain = lambda g: jnp.pad(g, (0, LANES - MLA_QK_DIM)).reshape(1, LANES)
    cos, sa, sb = tables
    n_seq_tiles = seq // tm
    tab_spec = pl.BlockSpec((tm, LANES), lambda i: (i % n_seq_tiles, 0))
    hw = MLA_HEADS * LANES
    vw = MLA_HEADS * MLA_V_DIM
    row = lambda w: pl.BlockSpec((tm, w), lambda i: (i, 0))
    return pl.pallas_call(
        _mla_proj_kernel,
        grid=(t // tm,),
        in_specs=[
            row(d), _resident((1, d)),
            _resident((d, q_rank)), _resident((1, q_rank)), _resident((q_rank, hw)),
            _resident((d, MLA_KV_RANK + LANES)), _resident((1, MLA_KV_RANK)),
            _resident((MLA_KV_RANK, hw)), _resident((MLA_KV_RANK, vw)),
            _resident((1, LANES)), _resident((1, LANES)),
            tab_spec, tab_spec, tab_spec,
        ],
        out_specs=[row(hw), row(hw), row(vw)],
        out_shape=[jax.ShapeDtypeStruct((t, hw), BF16), jax.ShapeDtypeStruct((t, hw), BF16),
                   jax.ShapeDtypeStruct((t, vw), BF16)],
        compiler_params=_params(1, 40),
        name="mla_proj",
    )(x, gain.reshape(1, d), w_dq.astype(BF16), q_lat_norm.reshape(1, q_rank), wuq, wdkv,
      kv_lat_norm.reshape(1, MLA_KV_RANK), wuk, wuv, pad_gain(qk_norm_q), pad_gain(qk_norm_k),
      cos, sa, sb)


def _mla_attn_kernel(bound_ref, q_ref, k_ref, v_ref, o_ref,
                     a0_ref, a1_ref, l0_ref, l1_ref, m0_ref, m1_ref, *, tq):
    qi = pl.program_id(2)
    low = _lane_iota((tq, LANES)) < MLA_V_DIM
    lanes = (slice(0, LANES), slice(LANES, 2 * LANES))
    qs = tuple(q_ref[0, :, ln] for ln in lanes)
    _attend_pair(bound_ref[0], qs, ((k_ref, lanes[0]), (k_ref, lanes[1])), v_ref,
                 (a0_ref, a1_ref), (l0_ref, l1_ref), (m0_ref, m1_ref), qi, tq)
    o_ref[0] = jnp.where(low, a0_ref[...], a1_ref[...]).astype(BF16)


def _mla_attn(q, k, v, bound, *, tq=512):
    b, s, _ = q.shape
    n_pairs = MLA_HEADS // 2
    kern = functools.partial(_mla_attn_kernel, tq=tq)
    return pl.pallas_call(
        kern,
        grid=(b, n_pairs, s // tq),
        in_specs=[
            pl.BlockSpec(memory_space=pltpu.SMEM),
            pl.BlockSpec((1, tq, 2 * LANES), lambda bi, p, qi: (bi, qi, p)),
            pl.BlockSpec((1, s, 2 * LANES), lambda bi, p, qi: (bi, 0, p)),
            pl.BlockSpec((1, s, LANES), lambda bi, p, qi: (bi, 0, p)),
        ],
        out_specs=pl.BlockSpec((1, tq, LANES), lambda bi, p, qi: (bi, qi, p)),
        out_shape=jax.ShapeDtypeStruct((b, s, n_pairs * LANES), BF16),
        scratch_shapes=_pair_scratch(tq),
        compiler_params=_params(3, 40),
        name="mla_attn",
    )(bound, q, k, v)


def _mem_kv_kernel(m_ref, g_ref, w_ref, kn_ref, k_ref, v_ref):
    h = _rms(m_ref[0], g_ref[...]).astype(BF16)
    kv = _dot(h, w_ref[...])
    width = XM_HEADS * XM_HEAD_DIM
    for hd in range(XM_HEADS):
        lanes = slice(hd * LANES, (hd + 1) * LANES)
        k_ref[0, :, lanes] = _rms(kv[:, lanes], kn_ref[...]).astype(BF16)
    v_ref[0] = kv[:, width:].astype(BF16)


def _mem_kv(mem, gain, w_kv, k_norm):
    b, m, d = mem.shape
    width = XM_HEADS * XM_HEAD_DIM
    out = jax.ShapeDtypeStruct((b, m, width), BF16)
    blk = pl.BlockSpec((1, m, width), lambda i: (i, 0, 0))
    return pl.pallas_call(
        _mem_kv_kernel,
        grid=(b,),
        in_specs=[pl.BlockSpec((1, m, d), lambda i: (i, 0, 0)), _resident((1, d)),
                  _resident((d, 2 * width)), _resident((1, XM_HEAD_DIM))],
        out_specs=[blk, blk],
        out_shape=[out, out],
        compiler_params=_params(1, 32),
        name="mem_kv",
    )(mem, gain.reshape(1, d), w_kv.astype(BF16), k_norm.reshape(1, XM_HEAD_DIM))


def _post_kernel(*refs, n_mix):
    x_ref = refs[0]
    mix_refs = refs[1:1 + n_mix]
    wout_ref, g_ref, wq_ref, qn_ref, k_ref, v_ref, wo_ref, o_ref = refs[1 + n_mix:]
    x = x_ref[...]
    off = 0
    for a_ref in mix_refs:
        w = a_ref.shape[1]
        x = x + _dot(a_ref[...], wout_ref[off:off + w, :])
        off += w
    h = _rms(x, g_ref[...]).astype(BF16)
    q = _dot(h, wq_ref[...])
    heads = []
    for hd in range(XM_HEADS):
        lanes = slice(hd * LANES, (hd + 1) * LANES)
        qh = (_rms(q[:, lanes], qn_ref[...]) * (XM_HEAD_DIM ** -0.5)).astype(BF16)
        s = _dot_nt(qh, k_ref[0, :, lanes])
        p = jnp.exp(s - jnp.max(s, axis=-1, keepdims=True))
        l = jnp.sum(p, axis=-1, keepdims=True)
        heads.append((_dot(p.astype(BF16), v_ref[0, :, lanes]) / l).astype(BF16))
    o = jnp.concatenate(heads, axis=-1)
    o_ref[...] = x + _dot(o, wo_ref[...])


def _post(x, mixes, w_out, gain, w_q, q_norm, mem_k, mem_v, w_o, seq, *, tm=512):
    t, d = x.shape
    b, m, width = mem_k.shape
    n_seq_tiles = seq // tm
    row = lambda w: pl.BlockSpec((tm, w), lambda i: (i, 0))
    mem_spec = pl.BlockSpec((1, m, width), lambda i: (i // n_seq_tiles, 0, 0))
    kern = functools.partial(_post_kernel, n_mix=len(mixes))
    return pl.pallas_call(
        kern,
        grid=(t // tm,),
        in_specs=[row(d)] + [row(a.shape[1]) for a in mixes] + [
            _resident(w_out.shape), _resident((1, d)), _resident((d, width)),
            _resident((1, XM_HEAD_DIM)), mem_spec, mem_spec, _resident((width, d))],
        out_specs=row(d),
        out_shape=jax.ShapeDtypeStruct((t, d), F32),
        compiler_params=_params(1, 40),
        name="post",
    )(x, *mixes, w_out.astype(BF16), gain.reshape(1, d), w_q.astype(BF16),
      q_norm.reshape(1, XM_HEAD_DIM), mem_k, mem_v, w_o.astype(BF16))


def _rope_tables(seq, dim, start, group):
    half = dim // 2
    inv = 1.0 / (ROPE_THETA ** (jnp.arange(0, dim, 2, dtype=F32) / dim))
    ang = jnp.arange(seq, dtype=F32)[:, None] * inv[None, :]
    cos, sin = jnp.cos(ang), jnp.sin(ang)
    zeros = jnp.zeros_like(sin)
    pad = lambda a, fill: jnp.concatenate(
        [jnp.full((seq, start), fill, F32), a, jnp.full((seq, group - start - dim), fill, F32)],
        axis=1)
    cos_t = pad(jnp.concatenate([cos, cos], axis=1), 1.0)
    sin_a = pad(jnp.concatenate([zeros, sin], axis=1), 0.0)
    sin_b = pad(jnp.concatenate([-sin, zeros], axis=1), 0.0)
    reps = LANES // group
    return tuple(jnp.tile(a, (1, reps)) for a in (cos_t, sin_a, sin_b))


def kernel(x, mem, ffn_norm, ffn_w_gate, ffn_w_up, ffn_w_down, mix_norm, ab_w_in, ab_w_out, diff_q_norm, diff_k_norm, diff_lambda_q1, diff_lambda_k1, diff_lambda_q2, diff_lambda_k2, diff_subln, mla_w_dq, mla_q_norm, mla_w_uq, mla_w_dkv, mla_kv_norm, mla_w_ukv, mla_qk_norm_q, mla_qk_norm_k, mla_w_o, xm_norm, xm_mem_norm, xm_w_q, xm_w_kv, xm_q_norm, xm_k_norm, xm_w_o):
    b, s, d = x.shape
    depth = ffn_norm.shape[0]
    x = x.reshape(b * s, d)
    diff_tables = _rope_tables(s, DIFF_HEAD_DIM, 0, DIFF_HEAD_DIM)
    mla_tables = _rope_tables(s, MLA_ROPE_DIM, MLA_NOPE_DIM, LANES)
    for layer in range(depth):
        i = layer // 2
        x = _ffn(x, ffn_norm[layer, 0], ffn_w_gate[layer, 0], ffn_w_up[layer, 0],
                 ffn_w_down[layer, 0])
        if layer % 2 == 0:
            lambda_init = 0.8 - 0.6 * math.exp(-0.3 * layer)
            proj = _ab_proj(x, mix_norm[layer], ab_w_in[i], diff_q_norm[i], diff_k_norm[i],
                            diff_tables, s).reshape(b, s, -1)
            lam_rows = jnp.stack([diff_lambda_q1[i], diff_lambda_k1[i],
                                  diff_lambda_q2[i], diff_lambda_k2[i]])
            bound = _score_bound(diff_q_norm[i], diff_k_norm[i], DIFF_HEAD_DIM)
            oa = _diff_attn(proj, bound, lam_rows, diff_subln[i], lambda_init)
            ob = _sb_attn(proj)
            mixes = [oa.reshape(b * s, -1), ob.reshape(b * s, -1)]
            w_out = ab_w_out[i]
        else:
            q, k, v = _mla_proj(x, mix_norm[layer], mla_w_dq[i], mla_q_norm[i], mla_w_uq[i],
                                mla_w_dkv[i], mla_kv_norm[i], mla_w_ukv[i], mla_qk_norm_q[i],
                                mla_qk_norm_k[i], mla_tables, s)
            bound = _score_bound(mla_qk_norm_q[i], mla_qk_norm_k[i], MLA_QK_DIM)
            o = _mla_attn(q.reshape(b, s, -1), k.reshape(b, s, -1), v.reshape(b, s, -1), bound)
            mixes = [o.reshape(b * s, -1)]
            w_out = mla_w_o[i]
        mem_k, mem_v = _mem_kv(mem, xm_mem_norm[layer], xm_w_kv[layer], xm_k_norm[layer])
        x = _post(x, mixes, w_out, xm_norm[layer], xm_w_q[layer], xm_q_norm[layer],
                  mem_k, mem_v, xm_w_o[layer], s)
        x = _ffn(x, ffn_norm[layer, 1], ffn_w_gate[layer, 1], ffn_w_up[layer, 1],
                 ffn_w_down[layer, 1])
    return x.reshape(b, s, d)
```

```python
import functools
import math

import jax
import jax.numpy as jnp
from jax import lax
from jax.experimental import pallas as pl
from jax.experimental.pallas import tpu as pltpu

F32 = jnp.float32
BF16 = jnp.bfloat16

LANES = 128
V7X_VMEM_LIMIT = 56 * 1024 * 1024

CHUNK = 64
ROPE_THETA = 10000.0
NORM_EPS = 1e-6
SB_LOG_ZERO = -104.0
MAX_SOFTMAX_SHIFT = 40.0

DIFF_HEADS = 4
DIFF_HEAD_DIM = 64
SB_HEADS = 8
SB_HEAD_DIM = 64
MLA_HEADS = 16
MLA_KV_RANK = 256
MLA_NOPE_DIM = 64
MLA_ROPE_DIM = 32
MLA_V_DIM = 64
MLA_QK_DIM = MLA_NOPE_DIM + MLA_ROPE_DIM
XM_HEADS = 4
XM_HEAD_DIM = 128


def _params(n_grid, vmem_mb):
    return pltpu.CompilerParams(
        dimension_semantics=("arbitrary",) * n_grid,
        vmem_limit_bytes=min(vmem_mb * 1024 * 1024, V7X_VMEM_LIMIT))


def _resident(shape):
    nd = len(shape)
    return pl.BlockSpec(shape, lambda *_: (0,) * nd, pipeline_mode=pl.Buffered(1))


def _rms(x, gain):
    return x * lax.rsqrt(jnp.mean(x * x, axis=-1, keepdims=True) + NORM_EPS) * gain


def _dot(a, b):
    return jnp.dot(a, b, preferred_element_type=F32)


def _dot_nt(a, b):
    return lax.dot_general(a, b, (((1,), (1,)), ((), ())), preferred_element_type=F32)


def _lane_iota(shape):
    return lax.broadcasted_iota(jnp.int32, shape, len(shape) - 1)


def _rope(x, cos, sin_a, sin_b, half):
    return x * cos + pltpu.roll(x, half, 1) * sin_a + pltpu.roll(x, LANES - half, 1) * sin_b


def _ffn_kernel(x_ref, g_ref, wg_ref, wu_ref, wd_ref, o_ref, h_ref, acc_ref):
    x = x_ref[...]
    h_ref[...] = _rms(x, g_ref[...]).astype(BF16)
    for j in range(wg_ref.shape[0]):
        h = h_ref[...]
        g = _dot(h, wg_ref[j])
        u = _dot(h, wu_ref[j])
        a = (g * jax.nn.sigmoid(g) * u).astype(BF16)
        down = _dot(a, wd_ref[j])
        if j == 0:
            acc_ref[...] = down
        else:
            acc_ref[...] += down
    o_ref[...] = x + 0.5 * acc_ref[...]


def _ffn(x, gain, w_gate, w_up, w_down, *, tm=512, tf=256):
    t, d = x.shape
    d_ff = w_gate.shape[1]
    nc = d_ff // tf
    wg = w_gate.astype(BF16).reshape(d, nc, tf).transpose(1, 0, 2)
    wu = w_up.astype(BF16).reshape(d, nc, tf).transpose(1, 0, 2)
    wd = w_down.astype(BF16).reshape(nc, tf, d)
    return pl.pallas_call(
        _ffn_kernel,
        grid=(t // tm,),
        in_specs=[
            pl.BlockSpec((tm, d), lambda i: (i, 0)),
            _resident((1, d)),
            _resident((nc, d, tf)),
            _resident((nc, d, tf)),
            _resident((nc, tf, d)),
        ],
        out_specs=pl.BlockSpec((tm, d), lambda i: (i, 0)),
        out_shape=jax.ShapeDtypeStruct((t, d), F32),
        scratch_shapes=[pltpu.VMEM((tm, d), BF16), pltpu.VMEM((tm, d), F32)],
        compiler_params=_params(1, 48),
        name="ffn",
    )(x, gain.reshape(1, d), wg, wu, wd)


def _ab_proj_kernel(x_ref, g_ref, w_ref, qn_ref, kn_ref, cos_ref, sa_ref, sb_ref, o_ref, *,
                    n_qk_groups, q_scale_groups):
    h = _rms(x_ref[...], g_ref[...]).astype(BF16)
    cos, sa, sb = cos_ref[...], sa_ref[...], sb_ref[...]
    n_groups = w_ref.shape[1] // LANES
    lane = _lane_iota((x_ref.shape[0], LANES))
    low = lane < DIFF_HEAD_DIM
    for c in range(0, n_groups, 2):
        y2 = _dot(h, w_ref[:, c * LANES:(c + 2) * LANES])
        for k in range(2):
            g = c + k
            y = y2[:, k * LANES:(k + 1) * LANES]
            if g < 2 * n_qk_groups:
                gain = qn_ref[...] if g < n_qk_groups else kn_ref[...]
                sq = y * y
                ss_lo = jnp.sum(jnp.where(low, sq, 0.0), axis=-1, keepdims=True)
                ss_hi = jnp.sum(jnp.where(low, 0.0, sq), axis=-1, keepdims=True)
                ms = jnp.where(low, ss_lo, ss_hi) * (1.0 / DIFF_HEAD_DIM)
                y = y * lax.rsqrt(ms + NORM_EPS) * gain
                y = _rope(y, cos, sa, sb, DIFF_HEAD_DIM // 2)
            if g in q_scale_groups:
                y = y * (DIFF_HEAD_DIM ** -0.5)
            o_ref[:, g * LANES:(g + 1) * LANES] = y.astype(BF16)


def _ab_proj(x, gain, w_in, q_norm, k_norm, tables, seq, *, tm=512):
    t, d = x.shape
    n = w_in.shape[1]
    n_qk_groups = 2 * DIFF_HEADS * DIFF_HEAD_DIM // LANES
    sb_q0 = (4 * DIFF_HEADS * DIFF_HEAD_DIM + DIFF_HEADS * 2 * DIFF_HEAD_DIM) // LANES
    q_scale_groups = tuple(range(n_qk_groups)) + tuple(
        range(sb_q0, sb_q0 + SB_HEADS * SB_HEAD_DIM // LANES))
    cos, sa, sb = tables
    n_seq_tiles = seq // tm
    tab_spec = pl.BlockSpec((tm, LANES), lambda i: (i % n_seq_tiles, 0))
    kern = functools.partial(_ab_proj_kernel, n_qk_groups=n_qk_groups,
                             q_scale_groups=q_scale_groups)
    return pl.pallas_call(
        kern,
        grid=(t // tm,),
        in_specs=[
            pl.BlockSpec((tm, d), lambda i: (i, 0)),
            _resident((1, d)),
            _resident((d, n)),
            _resident((1, LANES)),
            _resident((1, LANES)),
            tab_spec, tab_spec, tab_spec,
        ],
        out_specs=pl.BlockSpec((tm, n), lambda i: (i, 0)),
        out_shape=jax.ShapeDtypeStruct((t, n), BF16),
        compiler_params=_params(1, 40),
        name="ab_proj",
    )(x, gain.reshape(1, d), w_in.astype(BF16),
      jnp.tile(q_norm, 2).reshape(1, LANES), jnp.tile(k_norm, 2).reshape(1, LANES),
      cos, sa, sb)


def _chunk_causal_mask(tq, tk):
    qc = lax.broadcasted_iota(jnp.int32, (tq, tk), 0) // CHUNK
    kc = lax.broadcasted_iota(jnp.int32, (tq, tk), 1) // CHUNK
    return kc <= qc


def _fold_lanes(p):
    out = p[:, :LANES]
    for c in range(1, p.shape[1] // LANES):
        out = out + p[:, c * LANES:(c + 1) * LANES]
    return out


def _attend_pair(bound, load_q, k_srcs, v_ref, store_out, acc_refs, l_refs, m_refs, n_q, tq):
    def key_block(h, rows):
        ref, lanes = k_srcs[h]
        return ref[0, rows, lanes]

    @pl.when(bound <= MAX_SOFTMAX_SHIFT)
    def _():
        for qi in range(n_q):
            q_rows = slice(qi * tq, (qi + 1) * tq)
            qs = load_q(q_rows)
            acc, lsum = [None, None], [None, None]
            for kb in range(qi + 1):
                rows = slice(kb * tq, (kb + 1) * tq)
                v = v_ref[0, rows, :]
                for h in range(2):
                    p = jnp.exp(_dot_nt(qs[h], key_block(h, rows)) - bound)
                    if kb == qi:
                        p = jnp.where(_chunk_causal_mask(tq, tq), p, 0.0)
                    part, pv = _fold_lanes(p), _dot(p.astype(BF16), v)
                    lsum[h] = part if kb == 0 else lsum[h] + part
                    acc[h] = pv if kb == 0 else acc[h] + pv
            store_out(q_rows, *(acc[h] / jnp.sum(lsum[h], axis=-1, keepdims=True)
                                for h in range(2)))

    @pl.when(bound > MAX_SOFTMAX_SHIFT)
    def _():
        def q_tile(qi, carry):
            q_rows = pl.ds(pl.multiple_of(qi * tq, tq), tq)
            qs = load_q(q_rows)
            for h in range(2):
                acc_refs[h][...] = jnp.zeros_like(acc_refs[h])
                l_refs[h][...] = jnp.zeros_like(l_refs[h])
                m_refs[h][...] = jnp.full_like(m_refs[h], -jnp.inf)

            def block(kb, masked):
                rows = pl.ds(pl.multiple_of(kb * tq, tq), tq)
                v = v_ref[0, rows, :]
                for h in range(2):
                    s = _dot_nt(qs[h], key_block(h, rows))
                    if masked:
                        s = jnp.where(_chunk_causal_mask(tq, tq), s, -jnp.inf)
                    m_old = m_refs[h][...]
                    m_new = jnp.maximum(m_old, jnp.max(s, axis=-1, keepdims=True))
                    alpha = jnp.exp(m_old - m_new)
                    p = jnp.exp(s - m_new)
                    l_refs[h][...] = alpha * l_refs[h][...] + jnp.sum(p, axis=-1, keepdims=True)
                    acc_refs[h][...] = alpha * acc_refs[h][...] + _dot(p.astype(BF16), v)
                    m_refs[h][...] = m_new

            def body(kb, c):
                block(kb, False)
                return c

            lax.fori_loop(0, qi, body, 0)
            block(qi, True)
            store_out(q_rows, *(acc_refs[h][...] / l_refs[h][...] for h in range(2)))
            return carry

        lax.fori_loop(0, n_q, q_tile, 0)


def _pair_scratch(tq):
    acc = pltpu.VMEM((tq, LANES), F32)
    stat = pltpu.VMEM((tq, 1), F32)
    return [acc, acc, stat, stat, stat, stat]


def _score_bound(q_gain, k_gain, dim):
    return (jnp.max(jnp.abs(q_gain)) * jnp.max(jnp.abs(k_gain)) * math.sqrt(dim)).reshape(1)


def _diff_attn_kernel(bound_ref, q1_ref, q2_ref, k1_ref, k2_ref, v_ref, lam_ref, sub_ref, o_ref,
                      a1_ref, a2_ref, l1_ref, l2_ref, m1_ref, m2_ref, *, tq, lambda_init):
    j = pl.program_id(1)
    mine = (_lane_iota((tq, LANES)) >= DIFF_HEAD_DIM).astype(jnp.int32) == j % 2
    lam_v = lam_ref[...]
    lam = (jnp.exp(jnp.sum(lam_v[0:1] * lam_v[1:2], axis=-1, keepdims=True))
           - jnp.exp(jnp.sum(lam_v[2:3] * lam_v[3:4], axis=-1, keepdims=True)) + lambda_init)

    def load_q(rows):
        return tuple(jnp.where(mine, q_ref[0, rows, :].astype(F32), 0.0).astype(BF16)
                     for q_ref in (q1_ref, q2_ref))

    def store_out(rows, o1, o2):
        o = _rms(o1 - lam * o2, sub_ref[...]) * (1.0 - lambda_init)
        o_ref[0, rows, :] = o.astype(BF16)

    whole = slice(None)
    _attend_pair(bound_ref[0], load_q, ((k1_ref, whole), (k2_ref, whole)), v_ref, store_out,
                 (a1_ref, a2_ref), (l1_ref, l2_ref), (m1_ref, m2_ref), q1_ref.shape[1] // tq, tq)


def _diff_attn(proj, bound, lam_rows, subln, lambda_init, *, tq=512):
    b, s, _ = proj.shape
    k_off = 2 * DIFF_HEADS * DIFF_HEAD_DIM // LANES
    v_off = 2 * k_off
    half = DIFF_HEADS // 2
    kern = functools.partial(_diff_attn_kernel, tq=tq, lambda_init=lambda_init)
    cols = lambda f: pl.BlockSpec((1, s, LANES), lambda bi, j: (bi, 0, f(j)))
    return pl.pallas_call(
        kern,
        grid=(b, DIFF_HEADS),
        in_specs=[
            pl.BlockSpec(memory_space=pltpu.SMEM),
            cols(lambda j: j // 2),
            cols(lambda j: half + j // 2),
            cols(lambda j: k_off + j // 2),
            cols(lambda j: k_off + half + j // 2),
            cols(lambda j: v_off + j),
            _resident((4, DIFF_HEAD_DIM)),
            _resident((1, LANES)),
        ],
        out_specs=cols(lambda j: j),
        out_shape=jax.ShapeDtypeStruct((b, s, DIFF_HEADS * LANES), BF16),
        scratch_shapes=_pair_scratch(tq),
        compiler_params=_params(2, 48),
        name="diff_attn",
    )(bound, proj, proj, proj, proj, proj, lam_rows, subln.reshape(1, LANES))


def _split3(x):
    hi = x.astype(BF16)
    r = x - hi.astype(F32)
    mid = r.astype(BF16)
    lo = (r - mid.astype(F32)).astype(BF16)
    return hi, mid, lo


def _sb_attn_kernel(q_ref, k_ref, v_ref, o_ref, c0_ref, c1_ref, acc_ref, *, tq):
    qi = pl.program_id(2)
    lane = _lane_iota((tq, LANES))
    low = lane < SB_HEAD_DIM
    q = q_ref[0].astype(F32)
    q_heads = (jnp.where(low, q, 0.0).astype(BF16), jnp.where(low, 0.0, q).astype(BF16))
    c_refs = (c0_ref, c1_ref)
    c0_ref[...] = jnp.zeros_like(c0_ref)
    c1_ref[...] = jnp.zeros_like(c1_ref)
    acc_ref[...] = jnp.zeros_like(acc_ref)
    row = lax.broadcasted_iota(jnp.int32, (tq, tq), 0)
    col = lax.broadcasted_iota(jnp.int32, (tq, tq), 1)
    strict = col < row
    later_than = (row > col).astype(BF16)

    def block(kb, diagonal):
        rows = pl.ds(pl.multiple_of(kb * tq, tq), tq)
        k = k_ref[0, rows, :]
        v = v_ref[0, rows, :]
        outs = []
        for h in range(2):
            z = _dot_nt(q_heads[h], k)
            log_beta = jnp.minimum(z, 0.0) - jnp.log1p(jnp.exp(-jnp.abs(z)))
            log_keep = log_beta - z
            if diagonal:
                log_keep = jnp.where(strict, log_keep, 0.0)
            hi, mid, lo = _split3(log_keep)
            later = _dot(hi, later_than) + _dot(mid, later_than) + _dot(lo, later_than)
            c_old = c_refs[h][...]
            w = jnp.exp(log_beta + later + c_old)
            if diagonal:
                w = jnp.where(strict, w, 0.0)
            outs.append(_dot(w.astype(BF16), v))
            c_refs[h][...] = c_old + jnp.sum(log_keep, axis=-1, keepdims=True)
        acc_ref[...] += jnp.where(low, outs[0], outs[1])

    def live():
        return jnp.max(jnp.maximum(c0_ref[...], c1_ref[...])) > SB_LOG_ZERO

    block(qi, True)

    def cond(state):
        kb, go = state
        return jnp.logical_and(kb >= 0, go)

    def body(state):
        kb, _ = state
        block(kb, False)
        return kb - 1, live()

    lax.while_loop(cond, body, (qi - 1, live()))
    o_ref[0] = acc_ref[...].astype(BF16)


def _sb_attn(proj, *, tq=256):
    b, s, _ = proj.shape
    q_off = (4 * DIFF_HEADS * DIFF_HEAD_DIM + DIFF_HEADS * 2 * DIFF_HEAD_DIM) // LANES
    n_pairs = SB_HEADS * SB_HEAD_DIM // LANES
    kern = functools.partial(_sb_attn_kernel, tq=tq)
    return pl.pallas_call(
        kern,
        grid=(b, n_pairs, s // tq),
        in_specs=[
            pl.BlockSpec((1, tq, LANES), lambda bi, p, qi: (bi, qi, q_off + p)),
            pl.BlockSpec((1, s, LANES), lambda bi, p, qi: (bi, 0, q_off + n_pairs + p)),
            pl.BlockSpec((1, s, LANES), lambda bi, p, qi: (bi, 0, q_off + 2 * n_pairs + p)),
        ],
        out_specs=pl.BlockSpec((1, tq, LANES), lambda bi, p, qi: (bi, qi, p)),
        out_shape=jax.ShapeDtypeStruct((b, s, n_pairs * LANES), BF16),
        scratch_shapes=[pltpu.VMEM((tq, 1), F32), pltpu.VMEM((tq, 1), F32),
                        pltpu.VMEM((tq, LANES), F32)],
        compiler_params=_params(3, 32),
        name="sb_attn",
    )(proj, proj, proj)


def _mla_proj_kernel(x_ref, g_ref, wdq_ref, qln_ref, wuq_ref, wdkv_ref, kvln_ref, wuk_ref,
                     wuv_ref, gq_ref, gk_ref, cos_ref, sa_ref, sb_ref, q_ref, k_ref, v_ref):
    h = _rms(x_ref[...], g_ref[...]).astype(BF16)
    cos, sa, sb = cos_ref[...], sa_ref[...], sb_ref[...]
    half = MLA_ROPE_DIM // 2
    inv_dim = 1.0 / MLA_QK_DIM
    c_q = _rms(_dot(h, wdq_ref[...]), qln_ref[...]).astype(BF16)
    dkv = _dot(h, wdkv_ref[...])
    c_kv = _rms(dkv[:, :MLA_KV_RANK], kvln_ref[...]).astype(BF16)
    k_rope = dkv[:, MLA_KV_RANK:]
    gq, gk = gq_ref[...], gk_ref[...]
    k_rope_sq = jnp.sum(k_rope * k_rope, axis=-1, keepdims=True)
    k_rope_rot = _rope(k_rope * gk, cos, sa, sb, half)
    v_ref[...] = _dot(c_kv, wuv_ref[...]).astype(BF16)
    for hd in range(0, MLA_HEADS, 2):
        cols = slice(hd * LANES, (hd + 2) * LANES)
        q2 = _dot(c_q, wuq_ref[:, cols])
        k2 = _dot(c_kv, wuk_ref[:, cols])
        for i in range(2):
            lanes = slice(i * LANES, (i + 1) * LANES)
            out = slice((hd + i) * LANES, (hd + i + 1) * LANES)
            q = q2[:, lanes]
            q = q * lax.rsqrt(jnp.sum(q * q, axis=-1, keepdims=True) * inv_dim + NORM_EPS) * gq
            q = _rope(q, cos, sa, sb, half) * (MLA_QK_DIM ** -0.5)
            q_ref[:, out] = q.astype(BF16)
            kn = k2[:, lanes]
            ms = (jnp.sum(kn * kn, axis=-1, keepdims=True) + k_rope_sq) * inv_dim
            k = (kn * gk + k_rope_rot) * lax.rsqrt(ms + NORM_EPS)
            k_ref[:, out] = k.astype(BF16)


def _pad_heads(w, n_heads, width):
    r = w.shape[0]
    w = w.reshape(r, n_heads, width)
    return jnp.pad(w, ((0, 0), (0, 0), (0, LANES - width))).reshape(r, n_heads * LANES)


def _mla_proj(x, gain, w_dq, q_lat_norm, w_uq, w_dkv, kv_lat_norm, w_ukv, qk_norm_q, qk_norm_k,
              tables, seq, *, tm=512):
    t, d = x.shape
    q_rank = w_dq.shape[1]
    wuq = _pad_heads(w_uq, MLA_HEADS, MLA_QK_DIM).astype(BF16)
    ukv = w_ukv.reshape(MLA_KV_RANK, MLA_HEADS, MLA_NOPE_DIM + MLA_V_DIM)
    wuk = _pad_heads(ukv[:, :, :MLA_NOPE_DIM].reshape(MLA_KV_RANK, -1), MLA_HEADS,
                     MLA_NOPE_DIM).astype(BF16)
    wuv = ukv[:, :, MLA_NOPE_DIM:].reshape(MLA_KV_RANK, MLA_HEADS * MLA_V_DIM).astype(BF16)
    rope_cols = jnp.pad(w_dkv[:, MLA_KV_RANK:],
                        ((0, 0), (MLA_NOPE_DIM, LANES - MLA_QK_DIM)))
    wdkv = jnp.concatenate([w_dkv[:, :MLA_KV_RANK], rope_cols], axis=1).astype(BF16)
    pad_gain = lambda g: jnp.pad(g, (0, LANES - MLA_QK_DIM)).reshape(1, LANES)
    cos, sa, sb = tables
    n_seq_tiles = seq // tm
    tab_spec = pl.BlockSpec((tm, LANES), lambda i: (i % n_seq_tiles, 0))
    hw = MLA_HEADS * LANES
    vw = MLA_HEADS * MLA_V_DIM
    row = lambda w: pl.BlockSpec((tm, w), lambda i: (i, 0))
    return pl.pallas_call(
        _mla_proj_kernel,
        grid=(t // tm,),
        in_specs=[
            row(d), _resident((1, d)),
            _resident((d, q_rank)), _resident((1, q_rank)), _resident((q_rank, hw)),
            _resident((d, MLA_KV_RANK + LANES)), _resident((1, MLA_KV_RANK)),
            _resident((MLA_KV_RANK, hw)), _resident((MLA_KV_RANK, vw)),
            _resident((1, LANES)), _resident((1, LANES)),
            tab_spec, tab_spec, tab_spec,
        ],
        out_specs=[row(hw), row(hw), row(vw)],
        out_shape=[jax.ShapeDtypeStruct((t, hw), BF16), jax.ShapeDtypeStruct((t, hw), BF16),
                   jax.ShapeDtypeStruct((t, vw), BF16)],
        compiler_params=_params(1, 40),
        name="mla_proj",
    )(x, gain.reshape(1, d), w_dq.astype(BF16), q_lat_norm.reshape(1, q_rank), wuq, wdkv,
      kv_lat_norm.reshape(1, MLA_KV_RANK), wuk, wuv, pad_gain(qk_norm_q), pad_gain(qk_norm_k),
      cos, sa, sb)


def _mla_attn_kernel(bound_ref, q_ref, k_ref, v_ref, o_ref,
                     a0_ref, a1_ref, l0_ref, l1_ref, m0_ref, m1_ref, *, tq):
    low = _lane_iota((tq, LANES)) < MLA_V_DIM
    lanes = (slice(0, LANES), slice(LANES, 2 * LANES))

    def load_q(rows):
        return tuple(q_ref[0, rows, ln] for ln in lanes)

    def store_out(rows, o0, o1):
        o_ref[0, rows, :] = jnp.where(low, o0, o1).astype(BF16)

    _attend_pair(bound_ref[0], load_q, ((k_ref, lanes[0]), (k_ref, lanes[1])), v_ref, store_out,
                 (a0_ref, a1_ref), (l0_ref, l1_ref), (m0_ref, m1_ref), q_ref.shape[1] // tq, tq)


def _mla_attn(q, k, v, bound, *, tq=512):
    b, s, _ = q.shape
    n_pairs = MLA_HEADS // 2
    kern = functools.partial(_mla_attn_kernel, tq=tq)
    return pl.pallas_call(
        kern,
        grid=(b, n_pairs),
        in_specs=[
            pl.BlockSpec(memory_space=pltpu.SMEM),
            pl.BlockSpec((1, s, 2 * LANES), lambda bi, p: (bi, 0, p)),
            pl.BlockSpec((1, s, 2 * LANES), lambda bi, p: (bi, 0, p)),
            pl.BlockSpec((1, s, LANES), lambda bi, p: (bi, 0, p)),
        ],
        out_specs=pl.BlockSpec((1, s, LANES), lambda bi, p: (bi, 0, p)),
        out_shape=jax.ShapeDtypeStruct((b, s, n_pairs * LANES), BF16),
        scratch_shapes=_pair_scratch(tq),
        compiler_params=_params(2, 48),
        name="mla_attn",
    )(bound, q, k, v)


def _mem_kv_kernel(m_ref, g_ref, w_ref, kn_ref, k_ref, v_ref):
    h = _rms(m_ref[0], g_ref[...]).astype(BF16)
    kv = _dot(h, w_ref[...])
    width = XM_HEADS * XM_HEAD_DIM
    for hd in range(XM_HEADS):
        lanes = slice(hd * LANES, (hd + 1) * LANES)
        k_ref[0, :, lanes] = _rms(kv[:, lanes], kn_ref[...]).astype(BF16)
    v_ref[0] = kv[:, width:].astype(BF16)


def _mem_kv(mem, gain, w_kv, k_norm):
    b, m, d = mem.shape
    width = XM_HEADS * XM_HEAD_DIM
    out = jax.ShapeDtypeStruct((b, m, width), BF16)
    blk = pl.BlockSpec((1, m, width), lambda i: (i, 0, 0))
    return pl.pallas_call(
        _mem_kv_kernel,
        grid=(b,),
        in_specs=[pl.BlockSpec((1, m, d), lambda i: (i, 0, 0)), _resident((1, d)),
                  _resident((d, 2 * width)), _resident((1, XM_HEAD_DIM))],
        out_specs=[blk, blk],
        out_shape=[out, out],
        compiler_params=_params(1, 32),
        name="mem_kv",
    )(mem, gain.reshape(1, d), w_kv.astype(BF16), k_norm.reshape(1, XM_HEAD_DIM))


def _post_kernel(*refs, n_mix):
    x_ref = refs[0]
    mix_refs = refs[1:1 + n_mix]
    wout_ref, g_ref, wq_ref, qn_ref, k_ref, v_ref, wo_ref, o_ref = refs[1 + n_mix:]
    x = x_ref[...]
    off = 0
    for a_ref in mix_refs:
        w = a_ref.shape[1]
        x = x + _dot(a_ref[...], wout_ref[off:off + w, :])
        off += w
    h = _rms(x, g_ref[...]).astype(BF16)
    q = _dot(h, wq_ref[...])
    heads = []
    for hd in range(XM_HEADS):
        lanes = slice(hd * LANES, (hd + 1) * LANES)
        qh = (_rms(q[:, lanes], qn_ref[...]) * (XM_HEAD_DIM ** -0.5)).astype(BF16)
        s = _dot_nt(qh, k_ref[0, :, lanes])
        p = jnp.exp(s - jnp.max(s, axis=-1, keepdims=True))
        l = jnp.sum(p, axis=-1, keepdims=True)
        heads.append((_dot(p.astype(BF16), v_ref[0, :, lanes]) / l).astype(BF16))
    o = jnp.concatenate(heads, axis=-1)
    o_ref[...] = x + _dot(o, wo_ref[...])


def _post(x, mixes, w_out, gain, w_q, q_norm, mem_k, mem_v, w_o, seq, *, tm=512):
    t, d = x.shape
    b, m, width = mem_k.shape
    n_seq_tiles = seq // tm
    row = lambda w: pl.BlockSpec((tm, w), lambda i: (i, 0))
    mem_spec = pl.BlockSpec((1, m, width), lambda i: (i // n_seq_tiles, 0, 0))
    kern = functools.partial(_post_kernel, n_mix=len(mixes))
    return pl.pallas_call(
        kern,
        grid=(t // tm,),
        in_specs=[row(d)] + [row(a.shape[1]) for a in mixes] + [
            _resident(w_out.shape), _resident((1, d)), _resident((d, width)),
            _resident((1, XM_HEAD_DIM)), mem_spec, mem_spec, _resident((width, d))],
        out_specs=row(d),
        out_shape=jax.ShapeDtypeStruct((t, d), F32),
        compiler_params=_params(1, 40),
        name="post",
    )(x, *mixes, w_out.astype(BF16), gain.reshape(1, d), w_q.astype(BF16),
      q_norm.reshape(1, XM_HEAD_DIM), mem_k, mem_v, w_o.astype(BF16))


def _rope_tables(seq, dim, start, group):
    half = dim // 2
    inv = 1.0 / (ROPE_THETA ** (jnp.arange(0, dim, 2, dtype=F32) / dim))
    ang = jnp.arange(seq, dtype=F32)[:, None] * inv[None, :]
    cos, sin = jnp.cos(ang), jnp.sin(ang)
    zeros = jnp.zeros_like(sin)
    pad = lambda a, fill: jnp.concatenate(
        [jnp.full((seq, start), fill, F32), a, jnp.full((seq, group - start - dim), fill, F32)],
        axis=1)
    cos_t = pad(jnp.concatenate([cos, cos], axis=1), 1.0)
    sin_a = pad(jnp.concatenate([zeros, sin], axis=1), 0.0)
    sin_b = pad(jnp.concatenate([-sin, zeros], axis=1), 0.0)
    reps = LANES // group
    return tuple(jnp.tile(a, (1, reps)) for a in (cos_t, sin_a, sin_b))


def kernel(x, mem, ffn_norm, ffn_w_gate, ffn_w_up, ffn_w_down, mix_norm, ab_w_in, ab_w_out, diff_q_norm, diff_k_norm, diff_lambda_q1, diff_lambda_k1, diff_lambda_q2, diff_lambda_k2, diff_subln, mla_w_dq, mla_q_norm, mla_w_uq, mla_w_dkv, mla_kv_norm, mla_w_ukv, mla_qk_norm_q, mla_qk_norm_k, mla_w_o, xm_norm, xm_mem_norm, xm_w_q, xm_w_kv, xm_q_norm, xm_k_norm, xm_w_o):
    b, s, d = x.shape
    depth = ffn_norm.shape[0]
    x = x.reshape(b * s, d)
    diff_tables = _rope_tables(s, DIFF_HEAD_DIM, 0, DIFF_HEAD_DIM)
    mla_tables = _rope_tables(s, MLA_ROPE_DIM, MLA_NOPE_DIM, LANES)
    for layer in range(depth):
        i = layer // 2
        x = _ffn(x, ffn_norm[layer, 0], ffn_w_gate[layer, 0], ffn_w_up[layer, 0],
                 ffn_w_down[layer, 0])
        if layer % 2 == 0:
            lambda_init = 0.8 - 0.6 * math.exp(-0.3 * layer)
            proj = _ab_proj(x, mix_norm[layer], ab_w_in[i], diff_q_norm[i], diff_k_norm[i],
                            diff_tables, s).reshape(b, s, -1)
            lam_rows = jnp.stack([diff_lambda_q1[i], diff_lambda_k1[i],
                                  diff_lambda_q2[i], diff_lambda_k2[i]])
            bound = _score_bound(diff_q_norm[i], diff_k_norm[i], DIFF_HEAD_DIM)
            oa = _diff_attn(proj, bound, lam_rows, diff_subln[i], lambda_init)
            ob = _sb_attn(proj)
            mixes = [oa.reshape(b * s, -1), ob.reshape(b * s, -1)]
            w_out = ab_w_out[i]
        else:
            q, k, v = _mla_proj(x, mix_norm[layer], mla_w_dq[i], mla_q_norm[i], mla_w_uq[i],
                                mla_w_dkv[i], mla_kv_norm[i], mla_w_ukv[i], mla_qk_norm_q[i],
                                mla_qk_norm_k[i], mla_tables, s)
            bound = _score_bound(mla_qk_norm_q[i], mla_qk_norm_k[i], MLA_QK_DIM)
            o = _mla_attn(q.reshape(b, s, -1), k.reshape(b, s, -1), v.reshape(b, s, -1), bound)
            mixes = [o.reshape(b * s, -1)]
            w_out = mla_w_o[i]
        mem_k, mem_v = _mem_kv(mem, xm_mem_norm[layer], xm_w_kv[layer], xm_k_norm[layer])
        x = _post(x, mixes, w_out, xm_norm[layer], xm_w_q[layer], xm_q_norm[layer],
                  mem_k, mem_v, xm_w_o[layer], s)
        x = _ffn(x, ffn_norm[layer, 1], ffn_w_gate[layer, 1], ffn_w_up[layer, 1],
                 ffn_w_down[layer, 1])
    return x.reshape(b, s, d)
```

```python
import functools
import math

import jax
import jax.numpy as jnp
from jax import lax
from jax.experimental import pallas as pl
from jax.experimental.pallas import tpu as pltpu

F32 = jnp.float32
BF16 = jnp.bfloat16

LANES = 128
V7X_VMEM_LIMIT = 56 * 1024 * 1024

CHUNK = 64
ROPE_THETA = 10000.0
NORM_EPS = 1e-6
SB_LOG_ZERO = -104.0
MAX_SOFTMAX_SHIFT = 40.0

DIFF_HEADS = 4
DIFF_HEAD_DIM = 64
SB_HEADS = 8
SB_HEAD_DIM = 64
MLA_HEADS = 16
MLA_KV_RANK = 256
MLA_NOPE_DIM = 64
MLA_ROPE_DIM = 32
MLA_V_DIM = 64
MLA_QK_DIM = MLA_NOPE_DIM + MLA_ROPE_DIM
XM_HEADS = 4
XM_HEAD_DIM = 128


def _params(n_grid, vmem_mb):
    return pltpu.CompilerParams(
        dimension_semantics=("arbitrary",) * n_grid,
        vmem_limit_bytes=min(vmem_mb * 1024 * 1024, V7X_VMEM_LIMIT))


def _resident(shape):
    nd = len(shape)
    return pl.BlockSpec(shape, lambda *_: (0,) * nd, pipeline_mode=pl.Buffered(1))


def _rms(x, gain):
    return x * lax.rsqrt(jnp.mean(x * x, axis=-1, keepdims=True) + NORM_EPS) * gain


def _dot(a, b):
    return jnp.dot(a, b, preferred_element_type=F32)


def _dot_nt(a, b):
    return lax.dot_general(a, b, (((1,), (1,)), ((), ())), preferred_element_type=F32)


def _lane_iota(shape):
    return lax.broadcasted_iota(jnp.int32, shape, len(shape) - 1)


def _rope(x, cos, sin_a, sin_b, half):
    return x * cos + pltpu.roll(x, half, 1) * sin_a + pltpu.roll(x, LANES - half, 1) * sin_b


def _ffn_kernel(x_ref, g_ref, wg_ref, wu_ref, wd_ref, o_ref, h_ref, acc_ref):
    x = x_ref[...]
    h_ref[...] = _rms(x, g_ref[...]).astype(BF16)
    for j in range(wg_ref.shape[0]):
        h = h_ref[...]
        g = _dot(h, wg_ref[j])
        u = _dot(h, wu_ref[j])
        a = (g * jax.nn.sigmoid(g) * u).astype(BF16)
        down = _dot(a, wd_ref[j])
        if j == 0:
            acc_ref[...] = down
        else:
            acc_ref[...] += down
    o_ref[...] = x + 0.5 * acc_ref[...]


def _ffn(x, gain, w_gate, w_up, w_down, *, tm=512, tf=256):
    t, d = x.shape
    d_ff = w_gate.shape[1]
    nc = d_ff // tf
    wg = w_gate.astype(BF16).reshape(d, nc, tf).transpose(1, 0, 2)
    wu = w_up.astype(BF16).reshape(d, nc, tf).transpose(1, 0, 2)
    wd = w_down.astype(BF16).reshape(nc, tf, d)
    return pl.pallas_call(
        _ffn_kernel,
        grid=(t // tm,),
        in_specs=[
            pl.BlockSpec((tm, d), lambda i: (i, 0)),
            _resident((1, d)),
            _resident((nc, d, tf)),
            _resident((nc, d, tf)),
            _resident((nc, tf, d)),
        ],
        out_specs=pl.BlockSpec((tm, d), lambda i: (i, 0)),
        out_shape=jax.ShapeDtypeStruct((t, d), F32),
        scratch_shapes=[pltpu.VMEM((tm, d), BF16), pltpu.VMEM((tm, d), F32)],
        compiler_params=_params(1, 48),
        name="ffn",
    )(x, gain.reshape(1, d), wg, wu, wd)


def _ab_proj_kernel(x_ref, g_ref, w_ref, qn_ref, kn_ref, cos_ref, sa_ref, sb_ref, o_ref, *,
                    n_qk_groups, q_scale_groups):
    h = _rms(x_ref[...], g_ref[...]).astype(BF16)
    cos, sa, sb = cos_ref[...], sa_ref[...], sb_ref[...]
    n_groups = w_ref.shape[1] // LANES
    lane = _lane_iota((x_ref.shape[0], LANES))
    low = lane < DIFF_HEAD_DIM
    for c in range(0, n_groups, 2):
        y2 = _dot(h, w_ref[:, c * LANES:(c + 2) * LANES])
        for k in range(2):
            g = c + k
            y = y2[:, k * LANES:(k + 1) * LANES]
            if g < 2 * n_qk_groups:
                gain = qn_ref[...] if g < n_qk_groups else kn_ref[...]
                sq = y * y
                ss_lo = jnp.sum(jnp.where(low, sq, 0.0), axis=-1, keepdims=True)
                ss_hi = jnp.sum(jnp.where(low, 0.0, sq), axis=-1, keepdims=True)
                ms = jnp.where(low, ss_lo, ss_hi) * (1.0 / DIFF_HEAD_DIM)
                y = y * lax.rsqrt(ms + NORM_EPS) * gain
                y = _rope(y, cos, sa, sb, DIFF_HEAD_DIM // 2)
            if g in q_scale_groups:
                y = y * (DIFF_HEAD_DIM ** -0.5)
            o_ref[:, g * LANES:(g + 1) * LANES] = y.astype(BF16)


def _ab_proj(x, gain, w_in, q_norm, k_norm, tables, seq, *, tm=512):
    t, d = x.shape
    n = w_in.shape[1]
    n_qk_groups = 2 * DIFF_HEADS * DIFF_HEAD_DIM // LANES
    sb_q0 = (4 * DIFF_HEADS * DIFF_HEAD_DIM + DIFF_HEADS * 2 * DIFF_HEAD_DIM) // LANES
    q_scale_groups = tuple(range(n_qk_groups)) + tuple(
        range(sb_q0, sb_q0 + SB_HEADS * SB_HEAD_DIM // LANES))
    cos, sa, sb = tables
    n_seq_tiles = seq // tm
    tab_spec = pl.BlockSpec((tm, LANES), lambda i: (i % n_seq_tiles, 0))
    kern = functools.partial(_ab_proj_kernel, n_qk_groups=n_qk_groups,
                             q_scale_groups=q_scale_groups)
    return pl.pallas_call(
        kern,
        grid=(t // tm,),
        in_specs=[
            pl.BlockSpec((tm, d), lambda i: (i, 0)),
            _resident((1, d)),
            _resident((d, n)),
            _resident((1, LANES)),
            _resident((1, LANES)),
            tab_spec, tab_spec, tab_spec,
        ],
        out_specs=pl.BlockSpec((tm, n), lambda i: (i, 0)),
        out_shape=jax.ShapeDtypeStruct((t, n), BF16),
        compiler_params=_params(1, 40),
        name="ab_proj",
    )(x, gain.reshape(1, d), w_in.astype(BF16),
      jnp.tile(q_norm, 2).reshape(1, LANES), jnp.tile(k_norm, 2).reshape(1, LANES),
      cos, sa, sb)


def _chunk_causal_mask(tq, tk):
    qc = lax.broadcasted_iota(jnp.int32, (tq, tk), 0) // CHUNK
    kc = lax.broadcasted_iota(jnp.int32, (tq, tk), 1) // CHUNK
    return kc <= qc


def _fold_lanes(p):
    out = p[:, :LANES]
    for c in range(1, p.shape[1] // LANES):
        out = out + p[:, c * LANES:(c + 1) * LANES]
    return out


def _attend_pair(bound, load_q, k_srcs, v_ref, store_out, acc_refs, l_refs, m_refs, n_q, tq):
    def key_block(h, rows):
        ref, lanes = k_srcs[h]
        return ref[0, rows, lanes]

    @pl.when(bound <= MAX_SOFTMAX_SHIFT)
    def _():
        for qi in range(n_q):
            q_rows = slice(qi * tq, (qi + 1) * tq)
            qs = load_q(q_rows)
            acc, lsum = [None, None], [None, None]
            for kb in range(qi + 1):
                rows = slice(kb * tq, (kb + 1) * tq)
                v = v_ref[0, rows, :]
                for h in range(2):
                    p = jnp.exp(_dot_nt(qs[h], key_block(h, rows)) - bound)
                    if kb == qi:
                        p = jnp.where(_chunk_causal_mask(tq, tq), p, 0.0)
                    part, pv = _fold_lanes(p), _dot(p.astype(BF16), v)
                    lsum[h] = part if kb == 0 else lsum[h] + part
                    acc[h] = pv if kb == 0 else acc[h] + pv
            store_out(q_rows, *(acc[h] / jnp.sum(lsum[h], axis=-1, keepdims=True)
                                for h in range(2)))

    @pl.when(bound > MAX_SOFTMAX_SHIFT)
    def _():
        def q_tile(qi, carry):
            q_rows = pl.ds(pl.multiple_of(qi * tq, tq), tq)
            qs = load_q(q_rows)
            for h in range(2):
                acc_refs[h][...] = jnp.zeros_like(acc_refs[h])
                l_refs[h][...] = jnp.zeros_like(l_refs[h])
                m_refs[h][...] = jnp.full_like(m_refs[h], -jnp.inf)

            def block(kb, masked):
                rows = pl.ds(pl.multiple_of(kb * tq, tq), tq)
                v = v_ref[0, rows, :]
                for h in range(2):
                    s = _dot_nt(qs[h], key_block(h, rows))
                    if masked:
                        s = jnp.where(_chunk_causal_mask(tq, tq), s, -jnp.inf)
                    m_old = m_refs[h][...]
                    m_new = jnp.maximum(m_old, jnp.max(s, axis=-1, keepdims=True))
                    alpha = jnp.exp(m_old - m_new)
                    p = jnp.exp(s - m_new)
                    l_refs[h][...] = alpha * l_refs[h][...] + jnp.sum(p, axis=-1, keepdims=True)
                    acc_refs[h][...] = alpha * acc_refs[h][...] + _dot(p.astype(BF16), v)
                    m_refs[h][...] = m_new

            def body(kb, c):
                block(kb, False)
                return c

            lax.fori_loop(0, qi, body, 0)
            block(qi, True)
            store_out(q_rows, *(acc_refs[h][...] / l_refs[h][...] for h in range(2)))
            return carry

        lax.fori_loop(0, n_q, q_tile, 0)


def _pair_scratch(tq):
    acc = pltpu.VMEM((tq, LANES), F32)
    stat = pltpu.VMEM((tq, 1), F32)
    return [acc, acc, stat, stat, stat, stat]


def _score_bound(q_gain, k_gain, dim):
    return (jnp.max(jnp.abs(q_gain)) * jnp.max(jnp.abs(k_gain)) * math.sqrt(dim)).reshape(1)


def _diff_attn_kernel(bound_ref, q1_ref, q2_ref, k1_ref, k2_ref, v_ref, lam_ref, sub_ref, o_ref,
                      a1_ref, a2_ref, l1_ref, l2_ref, m1_ref, m2_ref, *, tq, lambda_init):
    j = pl.program_id(1)
    mine = (_lane_iota((tq, LANES)) >= DIFF_HEAD_DIM).astype(jnp.int32) == j % 2
    lam_v = lam_ref[...]
    lam = (jnp.exp(jnp.sum(lam_v[0:1] * lam_v[1:2], axis=-1, keepdims=True))
           - jnp.exp(jnp.sum(lam_v[2:3] * lam_v[3:4], axis=-1, keepdims=True)) + lambda_init)

    def load_q(rows):
        return tuple(jnp.where(mine, q_ref[0, rows, :].astype(F32), 0.0).astype(BF16)
                     for q_ref in (q1_ref, q2_ref))

    def store_out(rows, o1, o2):
        o = _rms(o1 - lam * o2, sub_ref[...]) * (1.0 - lambda_init)
        o_ref[0, rows, :] = o.astype(BF16)

    whole = slice(None)
    _attend_pair(bound_ref[0], load_q, ((k1_ref, whole), (k2_ref, whole)), v_ref, store_out,
                 (a1_ref, a2_ref), (l1_ref, l2_ref), (m1_ref, m2_ref), q1_ref.shape[1] // tq, tq)


def _diff_attn(proj, bound, lam_rows, subln, lambda_init, *, tq=512):
    b, s, _ = proj.shape
    k_off = 2 * DIFF_HEADS * DIFF_HEAD_DIM // LANES
    v_off = 2 * k_off
    half = DIFF_HEADS // 2
    kern = functools.partial(_diff_attn_kernel, tq=tq, lambda_init=lambda_init)
    cols = lambda f: pl.BlockSpec((1, s, LANES), lambda bi, j: (bi, 0, f(j)))
    return pl.pallas_call(
        kern,
        grid=(b, DIFF_HEADS),
        in_specs=[
            pl.BlockSpec(memory_space=pltpu.SMEM),
            cols(lambda j: j // 2),
            cols(lambda j: half + j // 2),
            cols(lambda j: k_off + j // 2),
            cols(lambda j: k_off + half + j // 2),
            cols(lambda j: v_off + j),
            _resident((4, DIFF_HEAD_DIM)),
            _resident((1, LANES)),
        ],
        out_specs=cols(lambda j: j),
        out_shape=jax.ShapeDtypeStruct((b, s, DIFF_HEADS * LANES), BF16),
        scratch_shapes=_pair_scratch(tq),
        compiler_params=_params(2, 48),
        name="diff_attn",
    )(bound, proj, proj, proj, proj, proj, lam_rows, subln.reshape(1, LANES))


def _split2(x):
    hi = x.astype(BF16)
    return hi, (x - hi.astype(F32)).astype(BF16)


def _sb_block(q, k, v, later_than, c_in, strict):
    z = _dot_nt(q, k)
    log_beta = jnp.minimum(z, 0.0) - jnp.log(1.0 + jnp.exp(-jnp.abs(z)))
    log_keep = log_beta - z
    if strict is not None:
        log_keep = jnp.where(strict, log_keep, 0.0)
    hi, mid = _split2(log_keep)
    later = _dot(hi, later_than) + _dot(mid, later_than)
    expo = log_beta + later
    if c_in is not None:
        expo = expo + c_in
    w = jnp.exp(expo)
    if strict is not None:
        w = jnp.where(strict, w, 0.0)
    return _dot(w.astype(BF16), v), jnp.sum(log_keep, axis=-1, keepdims=True)


def _sb_attn_kernel(q_ref, k_ref, v_ref, o_ref, c0_ref, c1_ref, acc_ref, *, tq):
    qi = pl.program_id(2)
    low = _lane_iota((tq, LANES)) < SB_HEAD_DIM
    q = q_ref[0].astype(F32)
    q_heads = (jnp.where(low, q, 0.0).astype(BF16), jnp.where(low, 0.0, q).astype(BF16))
    c_refs = (c0_ref, c1_ref)
    row = lax.broadcasted_iota(jnp.int32, (tq, tq), 0)
    col = lax.broadcasted_iota(jnp.int32, (tq, tq), 1)
    strict = col < row
    later_than = (row > col).astype(BF16)

    def key_rows(kb):
        return pl.ds(pl.multiple_of(kb * tq, tq), tq)

    has_prev = qi > 0
    rows_a, rows_b = key_rows(qi), key_rows(jnp.maximum(qi - 1, 0))
    k_a, v_a, k_b, v_b = k_ref[0, rows_a, :], v_ref[0, rows_a, :], k_ref[0, rows_b, :], v_ref[0, rows_b, :]
    outs = []
    for h in range(2):
        pv_a, sum_a = _sb_block(q_heads[h], k_a, v_a, later_than, None, strict)
        pv_b, sum_b = _sb_block(q_heads[h], k_b, v_b, later_than, sum_a, None)
        outs.append(pv_a + jnp.where(has_prev, pv_b, 0.0))
        c_refs[h][...] = sum_a + jnp.where(has_prev, sum_b, 0.0)
    acc_ref[...] = jnp.where(low, outs[0], outs[1])

    def live():
        return jnp.max(jnp.maximum(c0_ref[...], c1_ref[...])) > SB_LOG_ZERO

    def cond(state):
        kb, go = state
        return jnp.logical_and(kb >= 0, go)

    def body(state):
        kb, _ = state
        rows = key_rows(kb)
        k, v = k_ref[0, rows, :], v_ref[0, rows, :]
        outs = []
        for h in range(2):
            c_old = c_refs[h][...]
            pv, row_sum = _sb_block(q_heads[h], k, v, later_than, c_old, None)
            outs.append(pv)
            c_refs[h][...] = c_old + row_sum
        acc_ref[...] += jnp.where(low, outs[0], outs[1])
        return kb - 1, live()

    lax.while_loop(cond, body, (qi - 2, live()))
    o_ref[0] = acc_ref[...].astype(BF16)


def _sb_attn(proj, *, tq=256):
    b, s, _ = proj.shape
    q_off = (4 * DIFF_HEADS * DIFF_HEAD_DIM + DIFF_HEADS * 2 * DIFF_HEAD_DIM) // LANES
    n_pairs = SB_HEADS * SB_HEAD_DIM // LANES
    kern = functools.partial(_sb_attn_kernel, tq=tq)
    return pl.pallas_call(
        kern,
        grid=(b, n_pairs, s // tq),
        in_specs=[
            pl.BlockSpec((1, tq, LANES), lambda bi, p, qi: (bi, qi, q_off + p)),
            pl.BlockSpec((1, s, LANES), lambda bi, p, qi: (bi, 0, q_off + n_pairs + p)),
            pl.BlockSpec((1, s, LANES), lambda bi, p, qi: (bi, 0, q_off + 2 * n_pairs + p)),
        ],
        out_specs=pl.BlockSpec((1, tq, LANES), lambda bi, p, qi: (bi, qi, p)),
        out_shape=jax.ShapeDtypeStruct((b, s, n_pairs * LANES), BF16),
        scratch_shapes=[pltpu.VMEM((tq, 1), F32), pltpu.VMEM((tq, 1), F32),
                        pltpu.VMEM((tq, LANES), F32)],
        compiler_params=_params(3, 32),
        name="sb_attn",
    )(proj, proj, proj)


def _mla_proj_kernel(x_ref, g_ref, wdq_ref, qln_ref, wuq_ref, wdkv_ref, kvln_ref, wuk_ref,
                     wuv_ref, gq_ref, gk_ref, cos_ref, sa_ref, sb_ref, q_ref, k_ref, v_ref):
    h = _rms(x_ref[...], g_ref[...]).astype(BF16)
    cos, sa, sb = cos_ref[...], sa_ref[...], sb_ref[...]
    half = MLA_ROPE_DIM // 2
    inv_dim = 1.0 / MLA_QK_DIM
    c_q = _rms(_dot(h, wdq_ref[...]), qln_ref[...]).astype(BF16)
    dkv = _dot(h, wdkv_ref[...])
    c_kv = _rms(dkv[:, :MLA_KV_RANK], kvln_ref[...]).astype(BF16)
    k_rope = dkv[:, MLA_KV_RANK:]
    gq, gk = gq_ref[...], gk_ref[...]
    k_rope_sq = jnp.sum(k_rope * k_rope, axis=-1, keepdims=True)
    k_rope_rot = _rope(k_rope * gk, cos, sa, sb, half)
    v_ref[...] = _dot(c_kv, wuv_ref[...]).astype(BF16)
    for hd in range(0, MLA_HEADS, 2):
        cols = slice(hd * LANES, (hd + 2) * LANES)
        q2 = _dot(c_q, wuq_ref[:, cols])
        k2 = _dot(c_kv, wuk_ref[:, cols])
        for i in range(2):
            lanes = slice(i * LANES, (i + 1) * LANES)
            out = slice((hd + i) * LANES, (hd + i + 1) * LANES)
            q = q2[:, lanes]
            q = q * lax.rsqrt(jnp.sum(q * q, axis=-1, keepdims=True) * inv_dim + NORM_EPS) * gq
            q = _rope(q, cos, sa, sb, half) * (MLA_QK_DIM ** -0.5)
            q_ref[:, out] = q.astype(BF16)
            kn = k2[:, lanes]
            ms = (jnp.sum(kn * kn, axis=-1, keepdims=True) + k_rope_sq) * inv_dim
            k = (kn * gk + k_rope_rot) * lax.rsqrt(ms + NORM_EPS)
            k_ref[:, out] = k.astype(BF16)


def _pad_heads(w, n_heads, width):
    r = w.shape[0]
    w = w.reshape(r, n_heads, width)
    return jnp.pad(w, ((0, 0), (0, 0), (0, LANES - width))).reshape(r, n_heads * LANES)


def _mla_proj(x, gain, w_dq, q_lat_norm, w_uq, w_dkv, kv_lat_norm, w_ukv, qk_norm_q, qk_norm_k,
              tables, seq, *, tm=512):
    t, d = x.shape
    q_rank = w_dq.shape[1]
    wuq = _pad_heads(w_uq, MLA_HEADS, MLA_QK_DIM).astype(BF16)
    ukv = w_ukv.reshape(MLA_KV_RANK, MLA_HEADS, MLA_NOPE_DIM + MLA_V_DIM)
    wuk = _pad_heads(ukv[:, :, :MLA_NOPE_DIM].reshape(MLA_KV_RANK, -1), MLA_HEADS,
                     MLA_NOPE_DIM).astype(BF16)
    wuv = ukv[:, :, MLA_NOPE_DIM:].reshape(MLA_KV_RANK, MLA_HEADS * MLA_V_DIM).astype(BF16)
    rope_cols = jnp.pad(w_dkv[:, MLA_KV_RANK:],
                        ((0, 0), (MLA_NOPE_DIM, LANES - MLA_QK_DIM)))
    wdkv = jnp.concatenate([w_dkv[:, :MLA_KV_RANK], rope_cols], axis=1).astype(BF16)
    pad_gain = lambda g: jnp.pad(g, (0, LANES - MLA_QK_DIM)).reshape(1, LANES)
    cos, sa, sb = tables
    n_seq_tiles = seq // tm
    tab_spec = pl.BlockSpec((tm, LANES), lambda i: (i % n_seq_tiles, 0))
    hw = MLA_HEADS * LANES
    vw = MLA_HEADS * MLA_V_DIM
    row = lambda w: pl.BlockSpec((tm, w), lambda i: (i, 0))
    return pl.pallas_call(
        _mla_proj_kernel,
        grid=(t // tm,),
        in_specs=[
            row(d), _resident((1, d)),
            _resident((d, q_rank)), _resident((1, q_rank)), _resident((q_rank, hw)),
            _resident((d, MLA_KV_RANK + LANES)), _resident((1, MLA_KV_RANK)),
            _resident((MLA_KV_RANK, hw)), _resident((MLA_KV_RANK, vw)),
            _resident((1, LANES)), _resident((1, LANES)),
            tab_spec, tab_spec, tab_spec,
        ],
        out_specs=[row(hw), row(hw), row(vw)],
        out_shape=[jax.ShapeDtypeStruct((t, hw), BF16), jax.ShapeDtypeStruct((t, hw), BF16),
                   jax.ShapeDtypeStruct((t, vw), BF16)],
        compiler_params=_params(1, 40),
        name="mla_proj",
    )(x, gain.reshape(1, d), w_dq.astype(BF16), q_lat_norm.reshape(1, q_rank), wuq, wdkv,
      kv_lat_norm.reshape(1, MLA_KV_RANK), wuk, wuv, pad_gain(qk_norm_q), pad_gain(qk_norm_k),
      cos, sa, sb)


def _mla_attn_kernel(bound_ref, q_ref, k_ref, v_ref, o_ref,
                     a0_ref, a1_ref, l0_ref, l1_ref, m0_ref, m1_ref, *, tq):
    low = _lane_iota((tq, LANES)) < MLA_V_DIM
    lanes = (slice(0, LANES), slice(LANES, 2 * LANES))

    def load_q(rows):
        return tuple(q_ref[0, rows, ln] for ln in lanes)

    def store_out(rows, o0, o1):
        o_ref[0, rows, :] = jnp.where(low, o0, o1).astype(BF16)

    _attend_pair(bound_ref[0], load_q, ((k_ref, lanes[0]), (k_ref, lanes[1])), v_ref, store_out,
                 (a0_ref, a1_ref), (l0_ref, l1_ref), (m0_ref, m1_ref), q_ref.shape[1] // tq, tq)


def _mla_attn(q, k, v, bound, *, tq=512):
    b, s, _ = q.shape
    n_pairs = MLA_HEADS // 2
    kern = functools.partial(_mla_attn_kernel, tq=tq)
    return pl.pallas_call(
        kern,
        grid=(b, n_pairs),
        in_specs=[
            pl.BlockSpec(memory_space=pltpu.SMEM),
            pl.BlockSpec((1, s, 2 * LANES), lambda bi, p: (bi, 0, p)),
            pl.BlockSpec((1, s, 2 * LANES), lambda bi, p: (bi, 0, p)),
            pl.BlockSpec((1, s, LANES), lambda bi, p: (bi, 0, p)),
        ],
        out_specs=pl.BlockSpec((1, s, LANES), lambda bi, p: (bi, 0, p)),
        out_shape=jax.ShapeDtypeStruct((b, s, n_pairs * LANES), BF16),
        scratch_shapes=_pair_scratch(tq),
        compiler_params=_params(2, 48),
        name="mla_attn",
    )(bound, q, k, v)


def _mem_kv_kernel(m_ref, g_ref, w_ref, kn_ref, k_ref, v_ref):
    h = _rms(m_ref[0], g_ref[...]).astype(BF16)
    kv = _dot(h, w_ref[...])
    width = XM_HEADS * XM_HEAD_DIM
    for hd in range(XM_HEADS):
        lanes = slice(hd * LANES, (hd + 1) * LANES)
        k_ref[0, :, lanes] = _rms(kv[:, lanes], kn_ref[...]).astype(BF16)
    v_ref[0] = kv[:, width:].astype(BF16)


def _mem_kv(mem, gain, w_kv, k_norm):
    b, m, d = mem.shape
    width = XM_HEADS * XM_HEAD_DIM
    out = jax.ShapeDtypeStruct((b, m, width), BF16)
    blk = pl.BlockSpec((1, m, width), lambda i: (i, 0, 0))
    return pl.pallas_call(
        _mem_kv_kernel,
        grid=(b,),
        in_specs=[pl.BlockSpec((1, m, d), lambda i: (i, 0, 0)), _resident((1, d)),
                  _resident((d, 2 * width)), _resident((1, XM_HEAD_DIM))],
        out_specs=[blk, blk],
        out_shape=[out, out],
        compiler_params=_params(1, 32),
        name="mem_kv",
    )(mem, gain.reshape(1, d), w_kv.astype(BF16), k_norm.reshape(1, XM_HEAD_DIM))


def _post_kernel(*refs, n_mix):
    x_ref = refs[0]
    mix_refs = refs[1:1 + n_mix]
    wout_ref, g_ref, wq_ref, qn_ref, k_ref, v_ref, wo_ref, o_ref = refs[1 + n_mix:]
    x = x_ref[...]
    off = 0
    for a_ref in mix_refs:
        w = a_ref.shape[1]
        x = x + _dot(a_ref[...], wout_ref[off:off + w, :])
        off += w
    h = _rms(x, g_ref[...]).astype(BF16)
    q = _dot(h, wq_ref[...])
    heads = []
    for hd in range(XM_HEADS):
        lanes = slice(hd * LANES, (hd + 1) * LANES)
        qh = (_rms(q[:, lanes], qn_ref[...]) * (XM_HEAD_DIM ** -0.5)).astype(BF16)
        s = _dot_nt(qh, k_ref[0, :, lanes])
        p = jnp.exp(s - jnp.max(s, axis=-1, keepdims=True))
        l = jnp.sum(p, axis=-1, keepdims=True)
        heads.append((_dot(p.astype(BF16), v_ref[0, :, lanes]) / l).astype(BF16))
    o = jnp.concatenate(heads, axis=-1)
    o_ref[...] = x + _dot(o, wo_ref[...])


def _post(x, mixes, w_out, gain, w_q, q_norm, mem_k, mem_v, w_o, seq, *, tm=512):
    t, d = x.shape
    b, m, width = mem_k.shape
    n_seq_tiles = seq // tm
    row = lambda w: pl.BlockSpec((tm, w), lambda i: (i, 0))
    mem_spec = pl.BlockSpec((1, m, width), lambda i: (i // n_seq_tiles, 0, 0))
    kern = functools.partial(_post_kernel, n_mix=len(mixes))
    return pl.pallas_call(
        kern,
        grid=(t // tm,),
        in_specs=[row(d)] + [row(a.shape[1]) for a in mixes] + [
            _resident(w_out.shape), _resident((1, d)), _resident((d, width)),
            _resident((1, XM_HEAD_DIM)), mem_spec, mem_spec, _resident((width, d))],
        out_specs=row(d),
        out_shape=jax.ShapeDtypeStruct((t, d), F32),
        compiler_params=_params(1, 40),
        name="post",
    )(x, *mixes, w_out.astype(BF16), gain.reshape(1, d), w_q.astype(BF16),
      q_norm.reshape(1, XM_HEAD_DIM), mem_k, mem_v, w_o.astype(BF16))


def _rope_tables(seq, dim, start, group):
    half = dim // 2
    inv = 1.0 / (ROPE_THETA ** (jnp.arange(0, dim, 2, dtype=F32) / dim))
    ang = jnp.arange(seq, dtype=F32)[:, None] * inv[None, :]
    cos, sin = jnp.cos(ang), jnp.sin(ang)
    zeros = jnp.zeros_like(sin)
    pad = lambda a, fill: jnp.concatenate(
        [jnp.full((seq, start), fill, F32), a, jnp.full((seq, group - start - dim), fill, F32)],
        axis=1)
    cos_t = pad(jnp.concatenate([cos, cos], axis=1), 1.0)
    sin_a = pad(jnp.concatenate([zeros, sin], axis=1), 0.0)
    sin_b = pad(jnp.concatenate([-sin, zeros], axis=1), 0.0)
    reps = LANES // group
    return tuple(jnp.tile(a, (1, reps)) for a in (cos_t, sin_a, sin_b))


def kernel(x, mem, ffn_norm, ffn_w_gate, ffn_w_up, ffn_w_down, mix_norm, ab_w_in, ab_w_out, diff_q_norm, diff_k_norm, diff_lambda_q1, diff_lambda_k1, diff_lambda_q2, diff_lambda_k2, diff_subln, mla_w_dq, mla_q_norm, mla_w_uq, mla_w_dkv, mla_kv_norm, mla_w_ukv, mla_qk_norm_q, mla_qk_norm_k, mla_w_o, xm_norm, xm_mem_norm, xm_w_q, xm_w_kv, xm_q_norm, xm_k_norm, xm_w_o):
    b, s, d = x.shape
    depth = ffn_norm.shape[0]
    x = x.reshape(b * s, d)
    diff_tables = _rope_tables(s, DIFF_HEAD_DIM, 0, DIFF_HEAD_DIM)
    mla_tables = _rope_tables(s, MLA_ROPE_DIM, MLA_NOPE_DIM, LANES)
    for layer in range(depth):
        i = layer // 2
        x = _ffn(x, ffn_norm[layer, 0], ffn_w_gate[layer, 0], ffn_w_up[layer, 0],
                 ffn_w_down[layer, 0])
        if layer % 2 == 0:
            lambda_init = 0.8 - 0.6 * math.exp(-0.3 * layer)
            proj = _ab_proj(x, mix_norm[layer], ab_w_in[i], diff_q_norm[i], diff_k_norm[i],
                            diff_tables, s).reshape(b, s, -1)
            lam_rows = jnp.stack([diff_lambda_q1[i], diff_lambda_k1[i],
                                  diff_lambda_q2[i], diff_lambda_k2[i]])
            bound = _score_bound(diff_q_norm[i], diff_k_norm[i], DIFF_HEAD_DIM)
            oa = _diff_attn(proj, bound, lam_rows, diff_subln[i], lambda_init)
            ob = _sb_attn(proj)
            mixes = [oa.reshape(b * s, -1), ob.reshape(b * s, -1)]
            w_out = ab_w_out[i]
        else:
            q, k, v = _mla_proj(x, mix_norm[layer], mla_w_dq[i], mla_q_norm[i], mla_w_uq[i],
                                mla_w_dkv[i], mla_kv_norm[i], mla_w_ukv[i], mla_qk_norm_q[i],
                                mla_qk_norm_k[i], mla_tables, s)
            bound = _score_bound(mla_qk_norm_q[i], mla_qk_norm_k[i], MLA_QK_DIM)
            o = _mla_attn(q.reshape(b, s, -1), k.reshape(b, s, -1), v.reshape(b, s, -1), bound)
            mixes = [o.reshape(b * s, -1)]
            w_out = mla_w_o[i]
        mem_k, mem_v = _mem_kv(mem, xm_mem_norm[layer], xm_w_kv[layer], xm_k_norm[layer])
        x = _post(x, mixes, w_out, xm_norm[layer], xm_w_q[layer], xm_q_norm[layer],
                  mem_k, mem_v, xm_w_o[layer], s)
        x = _ffn(x, ffn_norm[layer, 1], ffn_w_gate[layer, 1], ffn_w_up[layer, 1],
                 ffn_w_down[layer, 1])
    return x.reshape(b, s, d)
```

```python
import functools
import math

import jax
import jax.numpy as jnp
from jax import lax
from jax.experimental import pallas as pl
from jax.experimental.pallas import tpu as pltpu

F32 = jnp.float32
BF16 = jnp.bfloat16

LANES = 128
V7X_VMEM_LIMIT = 56 * 1024 * 1024

CHUNK = 64
ROPE_THETA = 10000.0
NORM_EPS = 1e-6
SB_LOG_ZERO = -104.0
MAX_SOFTMAX_SHIFT = 40.0

DIFF_HEADS = 4
DIFF_HEAD_DIM = 64
SB_HEADS = 8
SB_HEAD_DIM = 64
MLA_HEADS = 16
MLA_KV_RANK = 256
MLA_NOPE_DIM = 64
MLA_ROPE_DIM = 32
MLA_V_DIM = 64
MLA_QK_DIM = MLA_NOPE_DIM + MLA_ROPE_DIM
XM_HEADS = 4
XM_HEAD_DIM = 128


def _params(n_grid, vmem_mb):
    return pltpu.CompilerParams(
        dimension_semantics=("arbitrary",) * n_grid,
        vmem_limit_bytes=min(vmem_mb * 1024 * 1024, V7X_VMEM_LIMIT))


def _resident(shape):
    nd = len(shape)
    return pl.BlockSpec(shape, lambda *_: (0,) * nd, pipeline_mode=pl.Buffered(1))


def _rms(x, gain):
    return x * lax.rsqrt(jnp.mean(x * x, axis=-1, keepdims=True) + NORM_EPS) * gain


def _dot(a, b):
    return jnp.dot(a, b, preferred_element_type=F32)


def _dot_nt(a, b):
    return lax.dot_general(a, b, (((1,), (1,)), ((), ())), preferred_element_type=F32)


def _lane_iota(shape):
    return lax.broadcasted_iota(jnp.int32, shape, len(shape) - 1)


def _rope(x, cos, sin_a, sin_b, half):
    return x * cos + pltpu.roll(x, half, 1) * sin_a + pltpu.roll(x, LANES - half, 1) * sin_b


def _ffn_kernel(x_ref, g_ref, wg_ref, wu_ref, wd_ref, o_ref, h_ref, acc_ref):
    x = x_ref[...]
    h_ref[...] = _rms(x, g_ref[...]).astype(BF16)
    for j in range(wg_ref.shape[0]):
        h = h_ref[...]
        g = _dot(h, wg_ref[j])
        u = _dot(h, wu_ref[j])
        a = (g * jax.nn.sigmoid(g) * u).astype(BF16)
        down = _dot(a, wd_ref[j])
        if j == 0:
            acc_ref[...] = down
        else:
            acc_ref[...] += down
    o_ref[...] = x + 0.5 * acc_ref[...]


def _ffn(x, gain, w_gate, w_up, w_down, *, tm=512, tf=256):
    t, d = x.shape
    d_ff = w_gate.shape[1]
    nc = d_ff // tf
    wg = w_gate.astype(BF16).reshape(d, nc, tf).transpose(1, 0, 2)
    wu = w_up.astype(BF16).reshape(d, nc, tf).transpose(1, 0, 2)
    wd = w_down.astype(BF16).reshape(nc, tf, d)
    return pl.pallas_call(
        _ffn_kernel,
        grid=(t // tm,),
        in_specs=[
            pl.BlockSpec((tm, d), lambda i: (i, 0)),
            _resident((1, d)),
            _resident((nc, d, tf)),
            _resident((nc, d, tf)),
            _resident((nc, tf, d)),
        ],
        out_specs=pl.BlockSpec((tm, d), lambda i: (i, 0)),
        out_shape=jax.ShapeDtypeStruct((t, d), F32),
        scratch_shapes=[pltpu.VMEM((tm, d), BF16), pltpu.VMEM((tm, d), F32)],
        compiler_params=_params(1, 48),
        name="ffn",
    )(x, gain.reshape(1, d), wg, wu, wd)


def _ab_proj_kernel(x_ref, g_ref, w_ref, qn_ref, kn_ref, cos_ref, sa_ref, sb_ref, o_ref, *,
                    n_qk_groups, q_scale_groups):
    h = _rms(x_ref[...], g_ref[...]).astype(BF16)
    cos, sa, sb = cos_ref[...], sa_ref[...], sb_ref[...]
    n_groups = w_ref.shape[1] // LANES
    lane = _lane_iota((x_ref.shape[0], LANES))
    low = lane < DIFF_HEAD_DIM
    for c in range(0, n_groups, 2):
        y2 = _dot(h, w_ref[:, c * LANES:(c + 2) * LANES])
        for k in range(2):
            g = c + k
            y = y2[:, k * LANES:(k + 1) * LANES]
            if g < 2 * n_qk_groups:
                gain = qn_ref[...] if g < n_qk_groups else kn_ref[...]
                sq = y * y
                ss_lo = jnp.sum(jnp.where(low, sq, 0.0), axis=-1, keepdims=True)
                ss_hi = jnp.sum(jnp.where(low, 0.0, sq), axis=-1, keepdims=True)
                ms = jnp.where(low, ss_lo, ss_hi) * (1.0 / DIFF_HEAD_DIM)
                y = y * lax.rsqrt(ms + NORM_EPS) * gain
                y = _rope(y, cos, sa, sb, DIFF_HEAD_DIM // 2)
            if g in q_scale_groups:
                y = y * (DIFF_HEAD_DIM ** -0.5)
            o_ref[:, g * LANES:(g + 1) * LANES] = y.astype(BF16)


def _ab_proj(x, gain, w_in, q_norm, k_norm, tables, seq, *, tm=512):
    t, d = x.shape
    n = w_in.shape[1]
    n_qk_groups = 2 * DIFF_HEADS * DIFF_HEAD_DIM // LANES
    sb_q0 = (4 * DIFF_HEADS * DIFF_HEAD_DIM + DIFF_HEADS * 2 * DIFF_HEAD_DIM) // LANES
    q_scale_groups = tuple(range(n_qk_groups)) + tuple(
        range(sb_q0, sb_q0 + SB_HEADS * SB_HEAD_DIM // LANES))
    cos, sa, sb = tables
    n_seq_tiles = seq // tm
    tab_spec = pl.BlockSpec((tm, LANES), lambda i: (i % n_seq_tiles, 0))
    kern = functools.partial(_ab_proj_kernel, n_qk_groups=n_qk_groups,
                             q_scale_groups=q_scale_groups)
    return pl.pallas_call(
        kern,
        grid=(t // tm,),
        in_specs=[
            pl.BlockSpec((tm, d), lambda i: (i, 0)),
            _resident((1, d)),
            _resident((d, n)),
            _resident((1, LANES)),
            _resident((1, LANES)),
            tab_spec, tab_spec, tab_spec,
        ],
        out_specs=pl.BlockSpec((tm, n), lambda i: (i, 0)),
        out_shape=jax.ShapeDtypeStruct((t, n), BF16),
        compiler_params=_params(1, 40),
        name="ab_proj",
    )(x, gain.reshape(1, d), w_in.astype(BF16),
      jnp.tile(q_norm, 2).reshape(1, LANES), jnp.tile(k_norm, 2).reshape(1, LANES),
      cos, sa, sb)


def _chunk_causal_mask(tq, tk):
    qc = lax.broadcasted_iota(jnp.int32, (tq, tk), 0) // CHUNK
    kc = lax.broadcasted_iota(jnp.int32, (tq, tk), 1) // CHUNK
    return kc <= qc


def _fold_lanes(p):
    out = p[:, :LANES]
    for c in range(1, p.shape[1] // LANES):
        out = out + p[:, c * LANES:(c + 1) * LANES]
    return out


def _attend_pair(bound, load_q, k_srcs, v_ref, store_out, acc_refs, l_refs, m_refs, n_q, tq):
    def key_block(h, rows):
        ref, lanes = k_srcs[h]
        return ref[0, rows, lanes]

    @pl.when(bound <= MAX_SOFTMAX_SHIFT)
    def _():
        for qi in range(n_q):
            q_rows = slice(qi * tq, (qi + 1) * tq)
            qs = load_q(q_rows)
            acc, lsum = [None, None], [None, None]
            for kb in range(qi + 1):
                rows = slice(kb * tq, (kb + 1) * tq)
                v = v_ref[0, rows, :]
                for h in range(2):
                    p = jnp.exp(_dot_nt(qs[h], key_block(h, rows)) - bound)
                    if kb == qi:
                        p = jnp.where(_chunk_causal_mask(tq, tq), p, 0.0)
                    part, pv = _fold_lanes(p), _dot(p.astype(BF16), v)
                    lsum[h] = part if kb == 0 else lsum[h] + part
                    acc[h] = pv if kb == 0 else acc[h] + pv
            store_out(q_rows, *(acc[h] / jnp.sum(lsum[h], axis=-1, keepdims=True)
                                for h in range(2)))

    @pl.when(bound > MAX_SOFTMAX_SHIFT)
    def _():
        def q_tile(qi, carry):
            q_rows = pl.ds(pl.multiple_of(qi * tq, tq), tq)
            qs = load_q(q_rows)
            for h in range(2):
                acc_refs[h][...] = jnp.zeros_like(acc_refs[h])
                l_refs[h][...] = jnp.zeros_like(l_refs[h])
                m_refs[h][...] = jnp.full_like(m_refs[h], -jnp.inf)

            def block(kb, masked):
                rows = pl.ds(pl.multiple_of(kb * tq, tq), tq)
                v = v_ref[0, rows, :]
                for h in range(2):
                    s = _dot_nt(qs[h], key_block(h, rows))
                    if masked:
                        s = jnp.where(_chunk_causal_mask(tq, tq), s, -jnp.inf)
                    m_old = m_refs[h][...]
                    m_new = jnp.maximum(m_old, jnp.max(s, axis=-1, keepdims=True))
                    alpha = jnp.exp(m_old - m_new)
                    p = jnp.exp(s - m_new)
                    l_refs[h][...] = alpha * l_refs[h][...] + jnp.sum(p, axis=-1, keepdims=True)
                    acc_refs[h][...] = alpha * acc_refs[h][...] + _dot(p.astype(BF16), v)
                    m_refs[h][...] = m_new

            def body(kb, c):
                block(kb, False)
                return c

            lax.fori_loop(0, qi, body, 0)
            block(qi, True)
            store_out(q_rows, *(acc_refs[h][...] / l_refs[h][...] for h in range(2)))
            return carry

        lax.fori_loop(0, n_q, q_tile, 0)


def _pair_scratch(tq):
    acc = pltpu.VMEM((tq, LANES), F32)
    stat = pltpu.VMEM((tq, 1), F32)
    return [acc, acc, stat, stat, stat, stat]


def _score_bound(q_gain, k_gain, dim):
    return (jnp.max(jnp.abs(q_gain)) * jnp.max(jnp.abs(k_gain)) * math.sqrt(dim)).reshape(1)


def _diff_attn_kernel(bound_ref, q1_ref, q2_ref, k1_ref, k2_ref, v_ref, lam_ref, sub_ref, o_ref,
                      a1_ref, a2_ref, l1_ref, l2_ref, m1_ref, m2_ref, *, tq, lambda_init):
    j = pl.program_id(1)
    mine = (_lane_iota((tq, LANES)) >= DIFF_HEAD_DIM).astype(jnp.int32) == j % 2
    lam_v = lam_ref[...]
    lam = (jnp.exp(jnp.sum(lam_v[0:1] * lam_v[1:2], axis=-1, keepdims=True))
           - jnp.exp(jnp.sum(lam_v[2:3] * lam_v[3:4], axis=-1, keepdims=True)) + lambda_init)

    def load_q(rows):
        return tuple(jnp.where(mine, q_ref[0, rows, :].astype(F32), 0.0).astype(BF16)
                     for q_ref in (q1_ref, q2_ref))

    def store_out(rows, o1, o2):
        o = _rms(o1 - lam * o2, sub_ref[...]) * (1.0 - lambda_init)
        o_ref[0, rows, :] = o.astype(BF16)

    whole = slice(None)
    _attend_pair(bound_ref[0], load_q, ((k1_ref, whole), (k2_ref, whole)), v_ref, store_out,
                 (a1_ref, a2_ref), (l1_ref, l2_ref), (m1_ref, m2_ref), q1_ref.shape[1] // tq, tq)


def _diff_attn(proj, bound, lam_rows, subln, lambda_init, *, tq=512):
    b, s, _ = proj.shape
    k_off = 2 * DIFF_HEADS * DIFF_HEAD_DIM // LANES
    v_off = 2 * k_off
    half = DIFF_HEADS // 2
    kern = functools.partial(_diff_attn_kernel, tq=tq, lambda_init=lambda_init)
    cols = lambda f: pl.BlockSpec((1, s, LANES), lambda bi, j: (bi, 0, f(j)))
    return pl.pallas_call(
        kern,
        grid=(b, DIFF_HEADS),
        in_specs=[
            pl.BlockSpec(memory_space=pltpu.SMEM),
            cols(lambda j: j // 2),
            cols(lambda j: half + j // 2),
            cols(lambda j: k_off + j // 2),
            cols(lambda j: k_off + half + j // 2),
            cols(lambda j: v_off + j),
            _resident((4, DIFF_HEAD_DIM)),
            _resident((1, LANES)),
        ],
        out_specs=cols(lambda j: j),
        out_shape=jax.ShapeDtypeStruct((b, s, DIFF_HEADS * LANES), BF16),
        scratch_shapes=_pair_scratch(tq),
        compiler_params=_params(2, 48),
        name="diff_attn",
    )(bound, proj, proj, proj, proj, proj, lam_rows, subln.reshape(1, LANES))


def _split2(x):
    hi = x.astype(BF16)
    return hi, (x - hi.astype(F32)).astype(BF16)


def _sb_block(q, k, v, later_than, c_in, strict):
    z = _dot_nt(q, k)
    log_beta = jnp.minimum(z, 0.0) - jnp.log(1.0 + jnp.exp(-jnp.abs(z)))
    log_keep = log_beta - z
    if strict is not None:
        log_keep = jnp.where(strict, log_keep, 0.0)
    hi, mid = _split2(log_keep)
    later = _dot(hi, later_than) + _dot(mid, later_than)
    expo = log_beta + later
    if c_in is not None:
        expo = expo + c_in
    w = jnp.exp(expo)
    if strict is not None:
        w = jnp.where(strict, w, 0.0)
    return _dot(w.astype(BF16), v), jnp.sum(log_keep, axis=-1, keepdims=True)


def _sb_attn_kernel(q_ref, k_ref, v_ref, o_ref, c0_ref, c1_ref, acc_ref, *, tq):
    qi = pl.program_id(2)
    low = _lane_iota((tq, LANES)) < SB_HEAD_DIM
    q = q_ref[0].astype(F32)
    q_heads = (jnp.where(low, q, 0.0).astype(BF16), jnp.where(low, 0.0, q).astype(BF16))
    c_refs = (c0_ref, c1_ref)
    row = lax.broadcasted_iota(jnp.int32, (tq, tq), 0)
    col = lax.broadcasted_iota(jnp.int32, (tq, tq), 1)
    strict = col < row
    later_than = (row > col).astype(BF16)

    def key_rows(kb):
        return pl.ds(pl.multiple_of(kb * tq, tq), tq)

    has_prev = qi > 0
    rows_a, rows_b = key_rows(qi), key_rows(jnp.maximum(qi - 1, 0))
    k_a, v_a, k_b, v_b = k_ref[0, rows_a, :], v_ref[0, rows_a, :], k_ref[0, rows_b, :], v_ref[0, rows_b, :]
    outs = []
    for h in range(2):
        pv_a, sum_a = _sb_block(q_heads[h], k_a, v_a, later_than, None, strict)
        pv_b, sum_b = _sb_block(q_heads[h], k_b, v_b, later_than, sum_a, None)
        outs.append(pv_a + jnp.where(has_prev, pv_b, 0.0))
        c_refs[h][...] = sum_a + jnp.where(has_prev, sum_b, 0.0)
    acc_ref[...] = jnp.where(low, outs[0], outs[1])

    def live():
        return jnp.max(jnp.maximum(c0_ref[...], c1_ref[...])) > SB_LOG_ZERO

    def cond(state):
        kb, go = state
        return jnp.logical_and(kb >= 0, go)

    def body(state):
        kb, _ = state
        rows = key_rows(kb)
        k, v = k_ref[0, rows, :], v_ref[0, rows, :]
        outs = []
        for h in range(2):
            c_old = c_refs[h][...]
            pv, row_sum = _sb_block(q_heads[h], k, v, later_than, c_old, None)
            outs.append(pv)
            c_refs[h][...] = c_old + row_sum
        acc_ref[...] += jnp.where(low, outs[0], outs[1])
        return kb - 1, live()

    lax.while_loop(cond, body, (qi - 2, live()))
    o_ref[0] = acc_ref[...].astype(BF16)


def _sb_attn(proj, *, tq=256):
    b, s, _ = proj.shape
    q_off = (4 * DIFF_HEADS * DIFF_HEAD_DIM + DIFF_HEADS * 2 * DIFF_HEAD_DIM) // LANES
    n_pairs = SB_HEADS * SB_HEAD_DIM // LANES
    kern = functools.partial(_sb_attn_kernel, tq=tq)
    return pl.pallas_call(
        kern,
        grid=(b, n_pairs, s // tq),
        in_specs=[
            pl.BlockSpec((1, tq, LANES), lambda bi, p, qi: (bi, qi, q_off + p)),
            pl.BlockSpec((1, s, LANES), lambda bi, p, qi: (bi, 0, q_off + n_pairs + p)),
            pl.BlockSpec((1, s, LANES), lambda bi, p, qi: (bi, 0, q_off + 2 * n_pairs + p)),
        ],
        out_specs=pl.BlockSpec((1, tq, LANES), lambda bi, p, qi: (bi, qi, p)),
        out_shape=jax.ShapeDtypeStruct((b, s, n_pairs * LANES), BF16),
        scratch_shapes=[pltpu.VMEM((tq, 1), F32), pltpu.VMEM((tq, 1), F32),
                        pltpu.VMEM((tq, LANES), F32)],
        compiler_params=_params(3, 32),
        name="sb_attn",
    )(proj, proj, proj)


def _mla_proj_kernel(x_ref, g_ref, wdq_ref, qln_ref, wuq_ref, wuqs_ref, wdkv_ref, kvln_ref,
                     wuk_ref, wuv_ref, qsum_ref, ksum_ref, gkn_ref, qc_ref, qs_ref,
                     kc_ref, ksa_ref, ksb_ref, q_ref, k_ref, v_ref):
    tm = x_ref.shape[0]
    h = _rms(x_ref[...], g_ref[...]).astype(BF16)
    inv_dim = 1.0 / MLA_QK_DIM
    c_q = _rms(_dot(h, wdq_ref[...]), qln_ref[...]).astype(BF16)
    dkv = _dot(h, wdkv_ref[...])
    c_kv = _rms(dkv[:, :MLA_KV_RANK], kvln_ref[...]).astype(BF16)
    kr = dkv[:, MLA_KV_RANK:]
    kr_sq = 0.5 * jnp.sum(kr * kr, axis=-1, keepdims=True)
    half = MLA_ROPE_DIM // 2
    kr_rot = (kr * kc_ref[...] + pltpu.roll(kr, half, 1) * ksa_ref[...]
              + pltpu.roll(kr, LANES - half, 1) * ksb_ref[...])
    v_ref[...] = _dot(c_kv, wuv_ref[...]).astype(BF16)
    low = _lane_iota((tm, LANES)) < MLA_NOPE_DIM
    qc, qs, gkn = qc_ref[...], qs_ref[...], gkn_ref[...]
    for p2 in range(0, MLA_HEADS // 2, 2):
        kn4 = _dot(c_kv, wuk_ref[:, p2 * LANES:(p2 + 2) * LANES])
        for i in range(2):
            p = p2 + i
            cols = slice(p * 2 * LANES, (p + 1) * 2 * LANES)
            q2 = _dot(c_q, wuq_ref[:, cols])
            q2s = _dot(c_q, wuqs_ref[:, cols])
            q_ms = _dot((q2 * q2).astype(BF16), qsum_ref[...]) * inv_dim
            q_ref[:, cols] = ((q2 * qc + q2s * qs) * lax.rsqrt(q_ms + NORM_EPS)).astype(BF16)
            kn2 = kn4[:, i * LANES:(i + 1) * LANES]
            k_ms = (_dot((kn2 * kn2).astype(BF16), ksum_ref[...]) + kr_sq) * inv_dim
            rk = lax.rsqrt(k_ms + NORM_EPS)
            kg = kn2 * gkn
            k_ref[:, p * 2 * LANES:p * 2 * LANES + LANES] = (
                jnp.where(low, kg, kr_rot) * rk[:, :LANES]).astype(BF16)
            k_ref[:, p * 2 * LANES + LANES:(p + 1) * 2 * LANES] = (
                jnp.where(low, kr_rot, kg) * rk[:, LANES:]).astype(BF16)


def _mla_pair_layout(even, odd_first, odd_last, n_rows):
    z = lambda w: jnp.zeros((n_rows, w), F32)
    return jnp.concatenate([even, z(LANES - MLA_QK_DIM), odd_first, z(MLA_ROPE_DIM), odd_last], axis=1)


def _mla_proj(x, gain, w_dq, q_lat_norm, w_uq, w_dkv, kv_lat_norm, w_ukv, qk_norm_q, qk_norm_k,
              seq, *, tm=512):
    t, d = x.shape
    q_rank = w_dq.shape[1]
    half = MLA_ROPE_DIM // 2
    n_pairs = MLA_HEADS // 2
    swap = lambda a: jnp.concatenate([a[..., half:], a[..., :half]], axis=-1)

    uq = w_uq.reshape(q_rank, MLA_HEADS, MLA_QK_DIM)
    nope, rope = uq[:, :, :MLA_NOPE_DIM], uq[:, :, MLA_NOPE_DIM:]
    rope_sw = jnp.concatenate([-rope[..., half:], rope[..., :half]], axis=-1)
    zeros = jnp.zeros_like
    pack = lambda ev_n, ev_r, od_r, od_n: jnp.concatenate(
        [ev_n, ev_r, zeros(ev_r), od_r, zeros(od_r), od_n], axis=-1).reshape(q_rank, -1)
    wuq = pack(nope[:, 0::2], rope[:, 0::2], rope[:, 1::2], nope[:, 1::2]).astype(BF16)
    wuqs = pack(zeros(nope[:, 0::2]), rope_sw[:, 0::2], rope_sw[:, 1::2],
                zeros(nope[:, 1::2])).astype(BF16)

    ukv = w_ukv.reshape(MLA_KV_RANK, MLA_HEADS, MLA_NOPE_DIM + MLA_V_DIM)
    wuk = ukv[:, :, :MLA_NOPE_DIM].reshape(MLA_KV_RANK, -1).astype(BF16)
    wuv = ukv[:, :, MLA_NOPE_DIM:].reshape(MLA_KV_RANK, -1).astype(BF16)
    kr_w = w_dkv[:, MLA_KV_RANK:]
    kr_pad = jnp.zeros((d, MLA_ROPE_DIM), F32)
    wdkv = jnp.concatenate([w_dkv[:, :MLA_KV_RANK], kr_w, kr_pad, kr_w, kr_pad], axis=1).astype(BF16)

    r256 = jnp.arange(2 * LANES)
    qsum = (r256[:, None] // LANES == r256[None, :] // LANES).astype(BF16)
    r128 = jnp.arange(LANES)
    ksum = (r128[:, None] // MLA_NOPE_DIM == r256[None, :] // LANES).astype(BF16)

    inv = 1.0 / (ROPE_THETA ** (jnp.arange(0, MLA_ROPE_DIM, 2, dtype=F32) / MLA_ROPE_DIM))
    ang = jnp.arange(seq, dtype=F32)[:, None] * inv[None, :]
    cos, sin = jnp.tile(jnp.cos(ang), (1, 2)), jnp.tile(jnp.sin(ang), (1, 2))
    scale = MLA_QK_DIM ** -0.5
    gq_n, gq_r = qk_norm_q[:MLA_NOPE_DIM] * scale, qk_norm_q[MLA_NOPE_DIM:] * scale
    gk_n, gk_r = qk_norm_k[:MLA_NOPE_DIM], qk_norm_k[MLA_NOPE_DIM:]
    bq_n = jnp.broadcast_to(gq_n, (seq, MLA_NOPE_DIM))
    qc = _mla_pair_layout(jnp.concatenate([bq_n, cos * gq_r], axis=1), cos * gq_r, bq_n, seq)
    qs = _mla_pair_layout(jnp.concatenate([zeros(bq_n), sin * swap(gq_r)], axis=1),
                          sin * swap(gq_r), zeros(bq_n), seq)
    first = jnp.arange(MLA_ROPE_DIM) < half
    k_cos = cos * gk_r
    k_sa = jnp.where(first, 0.0, sin * swap(gk_r))
    k_sb = jnp.where(first, -sin * swap(gk_r), 0.0)
    twice = lambda a: jnp.concatenate([a, zeros(a), a, zeros(a)], axis=1)
    n_seq_tiles = seq // tm
    tab = lambda w: pl.BlockSpec((tm, w), lambda i: (i % n_seq_tiles, 0))
    hw = MLA_HEADS * LANES
    vw = MLA_HEADS * MLA_V_DIM
    row = lambda w: pl.BlockSpec((tm, w), lambda i: (i, 0))
    return pl.pallas_call(
        _mla_proj_kernel,
        grid=(t // tm,),
        in_specs=[
            row(d), _resident((1, d)),
            _resident((d, q_rank)), _resident((1, q_rank)), _resident((q_rank, hw)),
            _resident((q_rank, hw)),
            _resident((d, MLA_KV_RANK + LANES)), _resident((1, MLA_KV_RANK)),
            _resident((MLA_KV_RANK, n_pairs * LANES)), _resident((MLA_KV_RANK, vw)),
            _resident((2 * LANES, 2 * LANES)), _resident((LANES, 2 * LANES)), _resident((1, LANES)),
            tab(2 * LANES), tab(2 * LANES), tab(LANES), tab(LANES), tab(LANES),
        ],
        out_specs=[row(hw), row(hw), row(vw)],
        out_shape=[jax.ShapeDtypeStruct((t, hw), BF16), jax.ShapeDtypeStruct((t, hw), BF16),
                   jax.ShapeDtypeStruct((t, vw), BF16)],
        compiler_params=_params(1, 48),
        name="mla_proj",
    )(x, gain.reshape(1, d), w_dq.astype(BF16), q_lat_norm.reshape(1, q_rank), wuq, wuqs, wdkv,
      kv_lat_norm.reshape(1, MLA_KV_RANK), wuk, wuv, qsum, ksum,
      jnp.tile(gk_n, 2).reshape(1, LANES), qc, qs, twice(k_cos), twice(k_sa), twice(k_sb))


def _mla_attn_kernel(bound_ref, q_ref, k_ref, v_ref, o_ref,
                     a0_ref, a1_ref, l0_ref, l1_ref, m0_ref, m1_ref, *, tq):
    low = _lane_iota((tq, LANES)) < MLA_V_DIM
    lanes = (slice(0, LANES), slice(LANES, 2 * LANES))

    def load_q(rows):
        return tuple(q_ref[0, rows, ln] for ln in lanes)

    def store_out(rows, o0, o1):
        o_ref[0, rows, :] = jnp.where(low, o0, o1).astype(BF16)

    _attend_pair(bound_ref[0], load_q, ((k_ref, lanes[0]), (k_ref, lanes[1])), v_ref, store_out,
                 (a0_ref, a1_ref), (l0_ref, l1_ref), (m0_ref, m1_ref), q_ref.shape[1] // tq, tq)


def _mla_attn(q, k, v, bound, *, tq=512):
    b, s, _ = q.shape
    n_pairs = MLA_HEADS // 2
    kern = functools.partial(_mla_attn_kernel, tq=tq)
    return pl.pallas_call(
        kern,
        grid=(b, n_pairs),
        in_specs=[
            pl.BlockSpec(memory_space=pltpu.SMEM),
            pl.BlockSpec((1, s, 2 * LANES), lambda bi, p: (bi, 0, p)),
            pl.BlockSpec((1, s, 2 * LANES), lambda bi, p: (bi, 0, p)),
            pl.BlockSpec((1, s, LANES), lambda bi, p: (bi, 0, p)),
        ],
        out_specs=pl.BlockSpec((1, s, LANES), lambda bi, p: (bi, 0, p)),
        out_shape=jax.ShapeDtypeStruct((b, s, n_pairs * LANES), BF16),
        scratch_shapes=_pair_scratch(tq),
        compiler_params=_params(2, 48),
        name="mla_attn",
    )(bound, q, k, v)


def _mem_kv_kernel(m_ref, g_ref, w_ref, kn_ref, k_ref, v_ref):
    h = _rms(m_ref[0], g_ref[...]).astype(BF16)
    kv = _dot(h, w_ref[...])
    width = XM_HEADS * XM_HEAD_DIM
    for hd in range(XM_HEADS):
        lanes = slice(hd * LANES, (hd + 1) * LANES)
        k_ref[0, :, lanes] = _rms(kv[:, lanes], kn_ref[...]).astype(BF16)
    v_ref[0] = kv[:, width:].astype(BF16)


def _mem_kv(mem, gain, w_kv, k_norm):
    b, m, d = mem.shape
    width = XM_HEADS * XM_HEAD_DIM
    out = jax.ShapeDtypeStruct((b, m, width), BF16)
    blk = pl.BlockSpec((1, m, width), lambda i: (i, 0, 0))
    return pl.pallas_call(
        _mem_kv_kernel,
        grid=(b,),
        in_specs=[pl.BlockSpec((1, m, d), lambda i: (i, 0, 0)), _resident((1, d)),
                  _resident((d, 2 * width)), _resident((1, XM_HEAD_DIM))],
        out_specs=[blk, blk],
        out_shape=[out, out],
        compiler_params=_params(1, 32),
        name="mem_kv",
    )(mem, gain.reshape(1, d), w_kv.astype(BF16), k_norm.reshape(1, XM_HEAD_DIM))


def _post_kernel(*refs, n_mix):
    x_ref = refs[0]
    mix_refs = refs[1:1 + n_mix]
    wout_ref, g_ref, wq_ref, qn_ref, k_ref, v_ref, wo_ref, o_ref = refs[1 + n_mix:]
    x = x_ref[...]
    off = 0
    for a_ref in mix_refs:
        w = a_ref.shape[1]
        x = x + _dot(a_ref[...], wout_ref[off:off + w, :])
        off += w
    h = _rms(x, g_ref[...]).astype(BF16)
    q = _dot(h, wq_ref[...])
    heads = []
    for hd in range(XM_HEADS):
        lanes = slice(hd * LANES, (hd + 1) * LANES)
        qh = (_rms(q[:, lanes], qn_ref[...]) * (XM_HEAD_DIM ** -0.5)).astype(BF16)
        s = _dot_nt(qh, k_ref[0, :, lanes])
        p = jnp.exp(s - jnp.max(s, axis=-1, keepdims=True))
        l = jnp.sum(p, axis=-1, keepdims=True)
        heads.append((_dot(p.astype(BF16), v_ref[0, :, lanes]) / l).astype(BF16))
    o = jnp.concatenate(heads, axis=-1)
    o_ref[...] = x + _dot(o, wo_ref[...])


def _post(x, mixes, w_out, gain, w_q, q_norm, mem_k, mem_v, w_o, seq, *, tm=512):
    t, d = x.shape
    b, m, width = mem_k.shape
    n_seq_tiles = seq // tm
    row = lambda w: pl.BlockSpec((tm, w), lambda i: (i, 0))
    mem_spec = pl.BlockSpec((1, m, width), lambda i: (i // n_seq_tiles, 0, 0))
    kern = functools.partial(_post_kernel, n_mix=len(mixes))
    return pl.pallas_call(
        kern,
        grid=(t // tm,),
        in_specs=[row(d)] + [row(a.shape[1]) for a in mixes] + [
            _resident(w_out.shape), _resident((1, d)), _resident((d, width)),
            _resident((1, XM_HEAD_DIM)), mem_spec, mem_spec, _resident((width, d))],
        out_specs=row(d),
        out_shape=jax.ShapeDtypeStruct((t, d), F32),
        compiler_params=_params(1, 40),
        name="post",
    )(x, *mixes, w_out.astype(BF16), gain.reshape(1, d), w_q.astype(BF16),
      q_norm.reshape(1, XM_HEAD_DIM), mem_k, mem_v, w_o.astype(BF16))


def _rope_tables(seq, dim, start, group):
    half = dim // 2
    inv = 1.0 / (ROPE_THETA ** (jnp.arange(0, dim, 2, dtype=F32) / dim))
    ang = jnp.arange(seq, dtype=F32)[:, None] * inv[None, :]
    cos, sin = jnp.cos(ang), jnp.sin(ang)
    zeros = jnp.zeros_like(sin)
    pad = lambda a, fill: jnp.concatenate(
        [jnp.full((seq, start), fill, F32), a, jnp.full((seq, group - start - dim), fill, F32)],
        axis=1)
    cos_t = pad(jnp.concatenate([cos, cos], axis=1), 1.0)
    sin_a = pad(jnp.concatenate([zeros, sin], axis=1), 0.0)
    sin_b = pad(jnp.concatenate([-sin, zeros], axis=1), 0.0)
    reps = LANES // group
    return tuple(jnp.tile(a, (1, reps)) for a in (cos_t, sin_a, sin_b))


def kernel(x, mem, ffn_norm, ffn_w_gate, ffn_w_up, ffn_w_down, mix_norm, ab_w_in, ab_w_out, diff_q_norm, diff_k_norm, diff_lambda_q1, diff_lambda_k1, diff_lambda_q2, diff_lambda_k2, diff_subln, mla_w_dq, mla_q_norm, mla_w_uq, mla_w_dkv, mla_kv_norm, mla_w_ukv, mla_qk_norm_q, mla_qk_norm_k, mla_w_o, xm_norm, xm_mem_norm, xm_w_q, xm_w_kv, xm_q_norm, xm_k_norm, xm_w_o):
    b, s, d = x.shape
    depth = ffn_norm.shape[0]
    x = x.reshape(b * s, d)
    diff_tables = _rope_tables(s, DIFF_HEAD_DIM, 0, DIFF_HEAD_DIM)
    for layer in range(depth):
        i = layer // 2
        x = _ffn(x, ffn_norm[layer, 0], ffn_w_gate[layer, 0], ffn_w_up[layer, 0],
                 ffn_w_down[layer, 0])
        if layer % 2 == 0:
            lambda_init = 0.8 - 0.6 * math.exp(-0.3 * layer)
            proj = _ab_proj(x, mix_norm[layer], ab_w_in[i], diff_q_norm[i], diff_k_norm[i],
                            diff_tables, s).reshape(b, s, -1)
            lam_rows = jnp.stack([diff_lambda_q1[i], diff_lambda_k1[i],
                                  diff_lambda_q2[i], diff_lambda_k2[i]])
            bound = _score_bound(diff_q_norm[i], diff_k_norm[i], DIFF_HEAD_DIM)
            oa = _diff_attn(proj, bound, lam_rows, diff_subln[i], lambda_init)
            ob = _sb_attn(proj)
            mixes = [oa.reshape(b * s, -1), ob.reshape(b * s, -1)]
            w_out = ab_w_out[i]
        else:
            q, k, v = _mla_proj(x, mix_norm[layer], mla_w_dq[i], mla_q_norm[i], mla_w_uq[i],
                                mla_w_dkv[i], mla_kv_norm[i], mla_w_ukv[i], mla_qk_norm_q[i],
                                mla_qk_norm_k[i], s)
            bound = _score_bound(mla_qk_norm_q[i], mla_qk_norm_k[i], MLA_QK_DIM)
            o = _mla_attn(q.reshape(b, s, -1), k.reshape(b, s, -1), v.reshape(b, s, -1), bound)
            mixes = [o.reshape(b * s, -1)]
            w_out = mla_w_o[i]
        mem_k, mem_v = _mem_kv(mem, xm_mem_norm[layer], xm_w_kv[layer], xm_k_norm[layer])
        x = _post(x, mixes, w_out, xm_norm[layer], xm_w_q[layer], xm_q_norm[layer],
                  mem_k, mem_v, xm_w_o[layer], s)
        x = _ffn(x, ffn_norm[layer, 1], ffn_w_gate[layer, 1], ffn_w_up[layer, 1],
                 ffn_w_down[layer, 1])
    return x.reshape(b, s, d)
```

```python
import functools
import math

import jax
import jax.numpy as jnp
from jax import lax
from jax.experimental import pallas as pl
from jax.experimental.pallas import tpu as pltpu

F32 = jnp.float32
BF16 = jnp.bfloat16

LANES = 128
V7X_VMEM_LIMIT = 56 * 1024 * 1024

CHUNK = 64
ROPE_THETA = 10000.0
NORM_EPS = 1e-6
SB_LOG2_ZERO = -151.0
LOG2_E = math.log2(math.e)
MAX_SOFTMAX_SHIFT = 40.0

DIFF_HEADS = 4
DIFF_HEAD_DIM = 64
SB_HEADS = 8
SB_HEAD_DIM = 64
MLA_HEADS = 16
MLA_KV_RANK = 256
MLA_NOPE_DIM = 64
MLA_ROPE_DIM = 32
MLA_V_DIM = 64
MLA_QK_DIM = MLA_NOPE_DIM + MLA_ROPE_DIM
XM_HEADS = 4
XM_HEAD_DIM = 128


def _params(n_grid, vmem_mb):
    return pltpu.CompilerParams(
        dimension_semantics=("arbitrary",) * n_grid,
        vmem_limit_bytes=min(vmem_mb * 1024 * 1024, V7X_VMEM_LIMIT))


def _resident(shape):
    nd = len(shape)
    return pl.BlockSpec(shape, lambda *_: (0,) * nd, pipeline_mode=pl.Buffered(1))


def _rms(x, gain):
    return x * lax.rsqrt(jnp.mean(x * x, axis=-1, keepdims=True) + NORM_EPS) * gain


def _dot(a, b):
    return jnp.dot(a, b, preferred_element_type=F32)


def _dot_nt(a, b):
    return lax.dot_general(a, b, (((1,), (1,)), ((), ())), preferred_element_type=F32)


def _lane_iota(shape):
    return lax.broadcasted_iota(jnp.int32, shape, len(shape) - 1)


def _rope(x, cos, sin_a, sin_b, half):
    return x * cos + pltpu.roll(x, half, 1) * sin_a + pltpu.roll(x, LANES - half, 1) * sin_b


def _ffn_kernel(x_ref, g_ref, wg_ref, wu_ref, wd_ref, o_ref, h_ref, acc_ref, *, tf):
    x = x_ref[...]
    h_ref[...] = _rms(x, g_ref[...]).astype(BF16)
    for j in range(wg_ref.shape[1] // tf):
        cols = slice(j * tf, (j + 1) * tf)
        h = h_ref[...]
        g = _dot(h, wg_ref[:, cols])
        u = _dot(h, wu_ref[:, cols])
        a = (g * jax.nn.sigmoid(g) * u).astype(BF16)
        down = _dot(a, wd_ref[cols, :])
        if j == 0:
            acc_ref[...] = down
        else:
            acc_ref[...] += down
    o_ref[...] = x + 0.5 * acc_ref[...]


def _ffn(x, gain, w_gate, w_up, w_down, layer, half, *, tm=1024, tf=256):
    t, d = x.shape
    d_ff = w_gate.shape[-1]
    pick = lambda r, c: pl.BlockSpec((None, None, r, c), lambda i: (layer, half, 0, 0),
                                     pipeline_mode=pl.Buffered(1))
    return pl.pallas_call(
        functools.partial(_ffn_kernel, tf=tf),
        grid=(t // tm,),
        in_specs=[
            pl.BlockSpec((tm, d), lambda i: (i, 0)),
            _resident((1, d)),
            pick(d, d_ff), pick(d, d_ff), pick(d_ff, d),
        ],
        out_specs=pl.BlockSpec((tm, d), lambda i: (i, 0)),
        out_shape=jax.ShapeDtypeStruct((t, d), F32),
        scratch_shapes=[pltpu.VMEM((tm, d), BF16), pltpu.VMEM((tm, d), F32)],
        compiler_params=_params(1, 56),
        name="ffn",
    )(x, gain.reshape(1, d), w_gate, w_up, w_down)


def _ab_proj_kernel(x_ref, g_ref, w_ref, qn_ref, kn_ref, cos_ref, sa_ref, sb_ref, o_ref, *,
                    n_qk_groups, q_scales):
    h = _rms(x_ref[...], g_ref[...]).astype(BF16)
    cos, sa, sb = cos_ref[...], sa_ref[...], sb_ref[...]
    n_groups = w_ref.shape[1] // LANES
    lane = _lane_iota((x_ref.shape[0], LANES))
    low = lane < DIFF_HEAD_DIM
    for c in range(0, n_groups, 2):
        y2 = _dot(h, w_ref[:, c * LANES:(c + 2) * LANES])
        for k in range(2):
            g = c + k
            y = y2[:, k * LANES:(k + 1) * LANES]
            if g < 2 * n_qk_groups:
                gain = qn_ref[...] if g < n_qk_groups else kn_ref[...]
                sq = y * y
                ss_lo = jnp.sum(jnp.where(low, sq, 0.0), axis=-1, keepdims=True)
                ss_hi = jnp.sum(jnp.where(low, 0.0, sq), axis=-1, keepdims=True)
                ms = jnp.where(low, ss_lo, ss_hi) * (1.0 / DIFF_HEAD_DIM)
                y = y * lax.rsqrt(ms + NORM_EPS) * gain
                y = _rope(y, cos, sa, sb, DIFF_HEAD_DIM // 2)
            if g in q_scales:
                y = y * q_scales[g]
            o_ref[:, g * LANES:(g + 1) * LANES] = y.astype(BF16)


def _ab_proj(x, gain, w_in, q_norm, k_norm, tables, seq, *, tm=512):
    t, d = x.shape
    n = w_in.shape[1]
    n_qk_groups = 2 * DIFF_HEADS * DIFF_HEAD_DIM // LANES
    sb_q0 = (4 * DIFF_HEADS * DIFF_HEAD_DIM + DIFF_HEADS * 2 * DIFF_HEAD_DIM) // LANES
    q_scales = {g: DIFF_HEAD_DIM ** -0.5 for g in range(n_qk_groups)}
    q_scales.update({g: SB_HEAD_DIM ** -0.5 * LOG2_E
                     for g in range(sb_q0, sb_q0 + SB_HEADS * SB_HEAD_DIM // LANES)})
    cos, sa, sb = tables
    n_seq_tiles = seq // tm
    tab_spec = pl.BlockSpec((tm, LANES), lambda i: (i % n_seq_tiles, 0))
    kern = functools.partial(_ab_proj_kernel, n_qk_groups=n_qk_groups,
                             q_scales=q_scales)
    return pl.pallas_call(
        kern,
        grid=(t // tm,),
        in_specs=[
            pl.BlockSpec((tm, d), lambda i: (i, 0)),
            _resident((1, d)),
            _resident((d, n)),
            _resident((1, LANES)),
            _resident((1, LANES)),
            tab_spec, tab_spec, tab_spec,
        ],
        out_specs=pl.BlockSpec((tm, n), lambda i: (i, 0)),
        out_shape=jax.ShapeDtypeStruct((t, n), BF16),
        compiler_params=_params(1, 40),
        name="ab_proj",
    )(x, gain.reshape(1, d), w_in.astype(BF16),
      jnp.tile(q_norm, 2).reshape(1, LANES), jnp.tile(k_norm, 2).reshape(1, LANES),
      cos, sa, sb)


def _chunk_causal_mask(tq, tk):
    qc = lax.broadcasted_iota(jnp.int32, (tq, tk), 0) // CHUNK
    kc = lax.broadcasted_iota(jnp.int32, (tq, tk), 1) // CHUNK
    return kc <= qc


def _fold_lanes(p):
    out = p[:, :LANES]
    for c in range(1, p.shape[1] // LANES):
        out = out + p[:, c * LANES:(c + 1) * LANES]
    return out


def _attend_pair(bound, load_q, k_srcs, v_ref, store_out, acc_refs, l_refs, m_refs, n_q, tq):
    def key_block(h, rows):
        ref, lanes = k_srcs[h]
        return ref[0, rows, lanes]

    @pl.when(bound <= MAX_SOFTMAX_SHIFT)
    def _():
        hq = tq // 2
        lower_rows = lambda a, b: jnp.concatenate([a[:hq], a[hq:] + b], axis=0)
        for qi in range(n_q):
            q_rows = slice(qi * tq, (qi + 1) * tq)
            qs = load_q(q_rows)
            acc, lsum = [None, None], [None, None]
            for kb in range(qi):
                rows = slice(kb * tq, (kb + 1) * tq)
                v = v_ref[0, rows, :]
                for h in range(2):
                    p = jnp.exp(_dot_nt(qs[h], key_block(h, rows)) - bound)
                    part, pv = _fold_lanes(p), _dot(p.astype(BF16), v)
                    lsum[h] = part if kb == 0 else lsum[h] + part
                    acc[h] = pv if kb == 0 else acc[h] + pv
            rows_a = slice(qi * tq, qi * tq + hq)
            rows_b = slice(qi * tq + hq, (qi + 1) * tq)
            v_a, v_b = v_ref[0, rows_a, :], v_ref[0, rows_b, :]
            for h in range(2):
                p_a = jnp.exp(_dot_nt(qs[h], key_block(h, rows_a)) - bound)
                p_a = jnp.where(_chunk_causal_mask(tq, hq), p_a, 0.0)
                p_b = jnp.exp(_dot_nt(qs[h][hq:], key_block(h, rows_b)) - bound)
                p_b = jnp.where(_chunk_causal_mask(hq, hq), p_b, 0.0)
                part = lower_rows(_fold_lanes(p_a), _fold_lanes(p_b))
                pv = lower_rows(_dot(p_a.astype(BF16), v_a), _dot(p_b.astype(BF16), v_b))
                lsum[h] = part if qi == 0 else lsum[h] + part
                acc[h] = pv if qi == 0 else acc[h] + pv
            store_out(q_rows, *(acc[h] / jnp.sum(lsum[h], axis=-1, keepdims=True)
                                for h in range(2)))

    @pl.when(bound > MAX_SOFTMAX_SHIFT)
    def _():
        def q_tile(qi, carry):
            q_rows = pl.ds(pl.multiple_of(qi * tq, tq), tq)
            qs = load_q(q_rows)
            for h in range(2):
                acc_refs[h][...] = jnp.zeros_like(acc_refs[h])
                l_refs[h][...] = jnp.zeros_like(l_refs[h])
                m_refs[h][...] = jnp.full_like(m_refs[h], -jnp.inf)

            def block(kb, masked):
                rows = pl.ds(pl.multiple_of(kb * tq, tq), tq)
                v = v_ref[0, rows, :]
                for h in range(2):
                    s = _dot_nt(qs[h], key_block(h, rows))
                    if masked:
                        s = jnp.where(_chunk_causal_mask(tq, tq), s, -jnp.inf)
                    m_old = m_refs[h][...]
                    m_new = jnp.maximum(m_old, jnp.max(s, axis=-1, keepdims=True))
                    alpha = jnp.exp(m_old - m_new)
                    p = jnp.exp(s - m_new)
                    l_refs[h][...] = alpha * l_refs[h][...] + jnp.sum(p, axis=-1, keepdims=True)
                    acc_refs[h][...] = alpha * acc_refs[h][...] + _dot(p.astype(BF16), v)
                    m_refs[h][...] = m_new

            def body(kb, c):
                block(kb, False)
                return c

            lax.fori_loop(0, qi, body, 0)
            block(qi, True)
            store_out(q_rows, *(acc_refs[h][...] / l_refs[h][...] for h in range(2)))
            return carry

        lax.fori_loop(0, n_q, q_tile, 0)


def _pair_scratch(tq):
    acc = pltpu.VMEM((tq, LANES), F32)
    stat = pltpu.VMEM((tq, 1), F32)
    return [acc, acc, stat, stat, stat, stat]


def _score_bound(q_gain, k_gain, dim):
    return (jnp.max(jnp.abs(q_gain)) * jnp.max(jnp.abs(k_gain)) * math.sqrt(dim)).reshape(1)


def _diff_attn_kernel(bound_ref, q1_ref, q2_ref, k1_ref, k2_ref, v_ref, lam_ref, sub_ref, o_ref,
                      a1_ref, a2_ref, l1_ref, l2_ref, m1_ref, m2_ref, *, tq, lambda_init):
    j = pl.program_id(1)
    mine = (_lane_iota((tq, LANES)) >= DIFF_HEAD_DIM).astype(jnp.int32) == j % 2
    lam_v = lam_ref[...]
    lam = (jnp.exp(jnp.sum(lam_v[0:1] * lam_v[1:2], axis=-1, keepdims=True))
           - jnp.exp(jnp.sum(lam_v[2:3] * lam_v[3:4], axis=-1, keepdims=True)) + lambda_init)

    def load_q(rows):
        return tuple(jnp.where(mine, q_ref[0, rows, :].astype(F32), 0.0).astype(BF16)
                     for q_ref in (q1_ref, q2_ref))

    def store_out(rows, o1, o2):
        o = _rms(o1 - lam * o2, sub_ref[...]) * (1.0 - lambda_init)
        o_ref[0, rows, :] = o.astype(BF16)

    whole = slice(None)
    _attend_pair(bound_ref[0], load_q, ((k1_ref, whole), (k2_ref, whole)), v_ref, store_out,
                 (a1_ref, a2_ref), (l1_ref, l2_ref), (m1_ref, m2_ref), q1_ref.shape[1] // tq, tq)


def _diff_attn(proj, bound, lam_rows, subln, lambda_init, *, tq=512):
    b, s, _ = proj.shape
    k_off = 2 * DIFF_HEADS * DIFF_HEAD_DIM // LANES
    v_off = 2 * k_off
    half = DIFF_HEADS // 2
    kern = functools.partial(_diff_attn_kernel, tq=tq, lambda_init=lambda_init)
    cols = lambda f: pl.BlockSpec((1, s, LANES), lambda bi, j: (bi, 0, f(j)))
    return pl.pallas_call(
        kern,
        grid=(b, DIFF_HEADS),
        in_specs=[
            pl.BlockSpec(memory_space=pltpu.SMEM),
            cols(lambda j: j // 2),
            cols(lambda j: half + j // 2),
            cols(lambda j: k_off + j // 2),
            cols(lambda j: k_off + half + j // 2),
            cols(lambda j: v_off + j),
            _resident((4, DIFF_HEAD_DIM)),
            _resident((1, LANES)),
        ],
        out_specs=cols(lambda j: j),
        out_shape=jax.ShapeDtypeStruct((b, s, DIFF_HEADS * LANES), BF16),
        scratch_shapes=_pair_scratch(tq),
        compiler_params=_params(2, 48),
        name="diff_attn",
    )(bound, proj, proj, proj, proj, proj, lam_rows, subln.reshape(1, LANES))


def _split2(x):
    hi = x.astype(BF16)
    return hi, (x - hi.astype(F32)).astype(BF16)


def _sb_block(q, k, v, later_than, c_in, strict):
    z = _dot_nt(q, k)
    sign_bit = jnp.uint32(0x80000000)
    neg_abs = lax.bitcast_convert_type(lax.bitcast_convert_type(z, jnp.uint32) | sign_bit, F32)
    log_beta = jnp.minimum(z, 0.0) - jnp.log2(1.0 + jnp.exp2(neg_abs))
    log_keep = log_beta - z
    if strict is not None:
        log_keep = jnp.where(strict, log_keep, 0.0)
    hi, mid = _split2(log_keep)
    later = _dot(hi, later_than) + _dot(mid, later_than)
    expo = log_beta + later
    if c_in is not None:
        expo = expo + c_in
    w = jnp.exp2(expo)
    if strict is not None:
        w = jnp.where(strict, w, 0.0)
    return _dot(w.astype(BF16), v), jnp.sum(log_keep, axis=-1, keepdims=True)


def _sb_attn_kernel(q_ref, k_ref, v_ref, o_ref, c_ref, acc_ref, *, tq, n_groups):
    qi = pl.program_id(2)
    low = _lane_iota((tq, LANES)) < SB_HEAD_DIM
    q_heads = []
    for g in range(n_groups):
        q = q_ref[0, :, g * LANES:(g + 1) * LANES].astype(F32)
        q_heads += [jnp.where(low, q, 0.0).astype(BF16), jnp.where(low, 0.0, q).astype(BF16)]
    row = lax.broadcasted_iota(jnp.int32, (tq, tq), 0)
    col = lax.broadcasted_iota(jnp.int32, (tq, tq), 1)
    strict = col < row
    later_than = (row > col).astype(BF16)

    def key_rows(kb):
        return pl.ds(pl.multiple_of(kb * tq, tq), tq)

    def group_block(rows, g):
        lanes = slice(g * LANES, (g + 1) * LANES)
        return k_ref[0, rows, lanes], v_ref[0, rows, lanes]

    has_prev = qi > 0
    rows_a, rows_b = key_rows(qi), key_rows(jnp.maximum(qi - 1, 0))
    for g in range(n_groups):
        (k_a, v_a), (k_b, v_b) = group_block(rows_a, g), group_block(rows_b, g)
        outs = []
        for h in (2 * g, 2 * g + 1):
            pv_a, sum_a = _sb_block(q_heads[h], k_a, v_a, later_than, None, strict)
            pv_b, sum_b = _sb_block(q_heads[h], k_b, v_b, later_than, sum_a, None)
            outs.append(pv_a + jnp.where(has_prev, pv_b, 0.0))
            c_ref[:, h:h + 1] = sum_a + jnp.where(has_prev, sum_b, 0.0)
        acc_ref[:, g * LANES:(g + 1) * LANES] = jnp.where(low, outs[0], outs[1])

    def live():
        return jnp.max(c_ref[...]) > SB_LOG2_ZERO

    def cond(state):
        kb, go = state
        return jnp.logical_and(kb >= 0, go)

    def body(state):
        kb, _ = state
        rows = key_rows(kb)
        for g in range(n_groups):
            k, v = group_block(rows, g)
            outs = []
            for h in (2 * g, 2 * g + 1):
                c_old = c_ref[:, h:h + 1]
                pv, row_sum = _sb_block(q_heads[h], k, v, later_than, c_old, None)
                outs.append(pv)
                c_ref[:, h:h + 1] = c_old + row_sum
            acc_ref[:, g * LANES:(g + 1) * LANES] += jnp.where(low, outs[0], outs[1])
        return kb - 1, live()

    lax.while_loop(cond, body, (qi - 2, live()))
    o_ref[0] = acc_ref[...].astype(BF16)


def _sb_attn(proj, *, tq=256, n_groups=2):
    b, s, _ = proj.shape
    width = n_groups * LANES
    q_off = (4 * DIFF_HEADS * DIFF_HEAD_DIM + DIFF_HEADS * 2 * DIFF_HEAD_DIM) // width
    n_steps = SB_HEADS * SB_HEAD_DIM // width
    kern = functools.partial(_sb_attn_kernel, tq=tq, n_groups=n_groups)
    return pl.pallas_call(
        kern,
        grid=(b, n_steps, s // tq),
        in_specs=[
            pl.BlockSpec((1, tq, width), lambda bi, p, qi: (bi, qi, q_off + p)),
            pl.BlockSpec((1, s, width), lambda bi, p, qi: (bi, 0, q_off + n_steps + p)),
            pl.BlockSpec((1, s, width), lambda bi, p, qi: (bi, 0, q_off + 2 * n_steps + p)),
        ],
        out_specs=pl.BlockSpec((1, tq, width), lambda bi, p, qi: (bi, qi, p)),
        out_shape=jax.ShapeDtypeStruct((b, s, n_steps * width), BF16),
        scratch_shapes=[pltpu.VMEM((tq, 2 * n_groups), F32), pltpu.VMEM((tq, width), F32)],
        compiler_params=_params(3, 32),
        name="sb_attn",
    )(proj, proj, proj)


def _mla_proj_kernel(x_ref, g_ref, wdq_ref, qln_ref, wuq_ref, wuqs_ref, wdkv_ref, kvln_ref,
                     wuk_ref, wuv_ref, qsum_ref, ksum_ref, gkn_ref, qc_ref, qs_ref,
                     kc_ref, ksa_ref, ksb_ref, q_ref, k_ref, v_ref):
    tm = x_ref.shape[0]
    h = _rms(x_ref[...], g_ref[...]).astype(BF16)
    inv_dim = 1.0 / MLA_QK_DIM
    c_q = _rms(_dot(h, wdq_ref[...]), qln_ref[...]).astype(BF16)
    dkv = _dot(h, wdkv_ref[...])
    c_kv = _rms(dkv[:, :MLA_KV_RANK], kvln_ref[...]).astype(BF16)
    kr = dkv[:, MLA_KV_RANK:]
    kr_sq = 0.5 * jnp.sum(kr * kr, axis=-1, keepdims=True)
    half = MLA_ROPE_DIM // 2
    kr_rot = (kr * kc_ref[...] + pltpu.roll(kr, half, 1) * ksa_ref[...]
              + pltpu.roll(kr, LANES - half, 1) * ksb_ref[...])
    v_ref[...] = _dot(c_kv, wuv_ref[...]).astype(BF16)
    low = _lane_iota((tm, LANES)) < MLA_NOPE_DIM
    qc, qs, gkn = qc_ref[...], qs_ref[...], gkn_ref[...]
    for p2 in range(0, MLA_HEADS // 2, 2):
        kn4 = _dot(c_kv, wuk_ref[:, p2 * LANES:(p2 + 2) * LANES])
        for i in range(2):
            p = p2 + i
            cols = slice(p * 2 * LANES, (p + 1) * 2 * LANES)
            q2 = _dot(c_q, wuq_ref[:, cols])
            q2s = _dot(c_q, wuqs_ref[:, cols])
            q_ms = _dot((q2 * q2).astype(BF16), qsum_ref[...]) * inv_dim
            q_ref[:, cols] = ((q2 * qc + q2s * qs) * lax.rsqrt(q_ms + NORM_EPS)).astype(BF16)
            kn2 = kn4[:, i * LANES:(i + 1) * LANES]
            k_ms = (_dot((kn2 * kn2).astype(BF16), ksum_ref[...]) + kr_sq) * inv_dim
            rk = lax.rsqrt(k_ms + NORM_EPS)
            kg = kn2 * gkn
            k_ref[:, p * 2 * LANES:p * 2 * LANES + LANES] = (
                jnp.where(low, kg, kr_rot) * rk[:, :LANES]).astype(BF16)
            k_ref[:, p * 2 * LANES + LANES:(p + 1) * 2 * LANES] = (
                jnp.where(low, kr_rot, kg) * rk[:, LANES:]).astype(BF16)


def _mla_pair_layout(even, odd_first, odd_last, n_rows):
    z = lambda w: jnp.zeros((n_rows, w), F32)
    return jnp.concatenate([even, z(LANES - MLA_QK_DIM), odd_first, z(MLA_ROPE_DIM), odd_last], axis=1)


def _mla_proj(x, gain, w_dq, q_lat_norm, w_uq, w_dkv, kv_lat_norm, w_ukv, qk_norm_q, qk_norm_k,
              seq, *, tm=512):
    t, d = x.shape
    q_rank = w_dq.shape[1]
    half = MLA_ROPE_DIM // 2
    n_pairs = MLA_HEADS // 2
    swap = lambda a: jnp.concatenate([a[..., half:], a[..., :half]], axis=-1)

    uq = w_uq.reshape(q_rank, MLA_HEADS, MLA_QK_DIM)
    nope, rope = uq[:, :, :MLA_NOPE_DIM], uq[:, :, MLA_NOPE_DIM:]
    rope_sw = jnp.concatenate([-rope[..., half:], rope[..., :half]], axis=-1)
    zeros = jnp.zeros_like
    pack = lambda ev_n, ev_r, od_r, od_n: jnp.concatenate(
        [ev_n, ev_r, zeros(ev_r), od_r, zeros(od_r), od_n], axis=-1).reshape(q_rank, -1)
    wuq = pack(nope[:, 0::2], rope[:, 0::2], rope[:, 1::2], nope[:, 1::2]).astype(BF16)
    wuqs = pack(zeros(nope[:, 0::2]), rope_sw[:, 0::2], rope_sw[:, 1::2],
                zeros(nope[:, 1::2])).astype(BF16)

    ukv = w_ukv.reshape(MLA_KV_RANK, MLA_HEADS, MLA_NOPE_DIM + MLA_V_DIM)
    wuk = ukv[:, :, :MLA_NOPE_DIM].reshape(MLA_KV_RANK, -1).astype(BF16)
    wuv = ukv[:, :, MLA_NOPE_DIM:].reshape(MLA_KV_RANK, -1).astype(BF16)
    kr_w = w_dkv[:, MLA_KV_RANK:]
    kr_pad = jnp.zeros((d, MLA_ROPE_DIM), F32)
    wdkv = jnp.concatenate([w_dkv[:, :MLA_KV_RANK], kr_w, kr_pad, kr_w, kr_pad], axis=1).astype(BF16)

    r256 = jnp.arange(2 * LANES)
    qsum = (r256[:, None] // LANES == r256[None, :] // LANES).astype(BF16)
    r128 = jnp.arange(LANES)
    ksum = (r128[:, None] // MLA_NOPE_DIM == r256[None, :] // LANES).astype(BF16)

    inv = 1.0 / (ROPE_THETA ** (jnp.arange(0, MLA_ROPE_DIM, 2, dtype=F32) / MLA_ROPE_DIM))
    ang = jnp.arange(seq, dtype=F32)[:, None] * inv[None, :]
    cos, sin = jnp.tile(jnp.cos(ang), (1, 2)), jnp.tile(jnp.sin(ang), (1, 2))
    scale = MLA_QK_DIM ** -0.5
    gq_n, gq_r = qk_norm_q[:MLA_NOPE_DIM] * scale, qk_norm_q[MLA_NOPE_DIM:] * scale
    gk_n, gk_r = qk_norm_k[:MLA_NOPE_DIM], qk_norm_k[MLA_NOPE_DIM:]
    bq_n = jnp.broadcast_to(gq_n, (seq, MLA_NOPE_DIM))
    qc = _mla_pair_layout(jnp.concatenate([bq_n, cos * gq_r], axis=1), cos * gq_r, bq_n, seq)
    qs = _mla_pair_layout(jnp.concatenate([zeros(bq_n), sin * swap(gq_r)], axis=1),
                          sin * swap(gq_r), zeros(bq_n), seq)
    first = jnp.arange(MLA_ROPE_DIM) < half
    k_cos = cos * gk_r
    k_sa = jnp.where(first, 0.0, sin * swap(gk_r))
    k_sb = jnp.where(first, -sin * swap(gk_r), 0.0)
    twice = lambda a: jnp.concatenate([a, zeros(a), a, zeros(a)], axis=1)
    n_seq_tiles = seq // tm
    tab = lambda w: pl.BlockSpec((tm, w), lambda i: (i % n_seq_tiles, 0))
    hw = MLA_HEADS * LANES
    vw = MLA_HEADS * MLA_V_DIM
    row = lambda w: pl.BlockSpec((tm, w), lambda i: (i, 0))
    return pl.pallas_call(
        _mla_proj_kernel,
        grid=(t // tm,),
        in_specs=[
            row(d), _resident((1, d)),
            _resident((d, q_rank)), _resident((1, q_rank)), _resident((q_rank, hw)),
            _resident((q_rank, hw)),
            _resident((d, MLA_KV_RANK + LANES)), _resident((1, MLA_KV_RANK)),
            _resident((MLA_KV_RANK, n_pairs * LANES)), _resident((MLA_KV_RANK, vw)),
            _resident((2 * LANES, 2 * LANES)), _resident((LANES, 2 * LANES)), _resident((1, LANES)),
            tab(2 * LANES), tab(2 * LANES), tab(LANES), tab(LANES), tab(LANES),
        ],
        out_specs=[row(hw), row(hw), row(vw)],
        out_shape=[jax.ShapeDtypeStruct((t, hw), BF16), jax.ShapeDtypeStruct((t, hw), BF16),
                   jax.ShapeDtypeStruct((t, vw), BF16)],
        compiler_params=_params(1, 48),
        name="mla_proj",
    )(x, gain.reshape(1, d), w_dq.astype(BF16), q_lat_norm.reshape(1, q_rank), wuq, wuqs, wdkv,
      kv_lat_norm.reshape(1, MLA_KV_RANK), wuk, wuv, qsum, ksum,
      jnp.tile(gk_n, 2).reshape(1, LANES), qc, qs, twice(k_cos), twice(k_sa), twice(k_sb))


def _mla_attn_kernel(bound_ref, q_ref, k_ref, v_ref, o_ref,
                     a0_ref, a1_ref, l0_ref, l1_ref, m0_ref, m1_ref, *, tq):
    low = _lane_iota((tq, LANES)) < MLA_V_DIM
    lanes = (slice(0, LANES), slice(LANES, 2 * LANES))

    def load_q(rows):
        return tuple(q_ref[0, rows, ln] for ln in lanes)

    def store_out(rows, o0, o1):
        o_ref[0, rows, :] = jnp.where(low, o0, o1).astype(BF16)

    _attend_pair(bound_ref[0], load_q, ((k_ref, lanes[0]), (k_ref, lanes[1])), v_ref, store_out,
                 (a0_ref, a1_ref), (l0_ref, l1_ref), (m0_ref, m1_ref), q_ref.shape[1] // tq, tq)


def _mla_attn(q, k, v, bound, *, tq=512):
    b, s, _ = q.shape
    n_pairs = MLA_HEADS // 2
    kern = functools.partial(_mla_attn_kernel, tq=tq)
    return pl.pallas_call(
        kern,
        grid=(b, n_pairs),
        in_specs=[
            pl.BlockSpec(memory_space=pltpu.SMEM),
            pl.BlockSpec((1, s, 2 * LANES), lambda bi, p: (bi, 0, p)),
            pl.BlockSpec((1, s, 2 * LANES), lambda bi, p: (bi, 0, p)),
            pl.BlockSpec((1, s, LANES), lambda bi, p: (bi, 0, p)),
        ],
        out_specs=pl.BlockSpec((1, s, LANES), lambda bi, p: (bi, 0, p)),
        out_shape=jax.ShapeDtypeStruct((b, s, n_pairs * LANES), BF16),
        scratch_shapes=_pair_scratch(tq),
        compiler_params=_params(2, 48),
        name="mla_attn",
    )(bound, q, k, v)


def _mem_kv_kernel(m_ref, g_ref, w_ref, kn_ref, k_ref, v_ref):
    h = _rms(m_ref[0], g_ref[...]).astype(BF16)
    kv = _dot(h, w_ref[...])
    width = XM_HEADS * XM_HEAD_DIM
    for hd in range(XM_HEADS):
        lanes = slice(hd * LANES, (hd + 1) * LANES)
        k_ref[0, :, lanes] = _rms(kv[:, lanes], kn_ref[...]).astype(BF16)
    v_ref[0] = kv[:, width:].astype(BF16)


def _mem_kv(mem, gain, w_kv, k_norm):
    b, m, d = mem.shape
    width = XM_HEADS * XM_HEAD_DIM
    out = jax.ShapeDtypeStruct((b, m, width), BF16)
    blk = pl.BlockSpec((1, m, width), lambda i: (i, 0, 0))
    return pl.pallas_call(
        _mem_kv_kernel,
        grid=(b,),
        in_specs=[pl.BlockSpec((1, m, d), lambda i: (i, 0, 0)), _resident((1, d)),
                  _resident((d, 2 * width)), _resident((1, XM_HEAD_DIM))],
        out_specs=[blk, blk],
        out_shape=[out, out],
        compiler_params=_params(1, 32),
        name="mem_kv",
    )(mem, gain.reshape(1, d), w_kv.astype(BF16), k_norm.reshape(1, XM_HEAD_DIM))


def _post_kernel(*refs, n_mix):
    x_ref = refs[0]
    mix_refs = refs[1:1 + n_mix]
    wout_ref, g_ref, wq_ref, qn_ref, k_ref, v_ref, wo_ref, o_ref = refs[1 + n_mix:]
    x = x_ref[...]
    off = 0
    for a_ref in mix_refs:
        w = a_ref.shape[1]
        x = x + _dot(a_ref[...], wout_ref[off:off + w, :])
        off += w
    h = _rms(x, g_ref[...]).astype(BF16)
    q = _dot(h, wq_ref[...])
    heads = []
    for hd in range(XM_HEADS):
        lanes = slice(hd * LANES, (hd + 1) * LANES)
        qh = (_rms(q[:, lanes], qn_ref[...]) * (XM_HEAD_DIM ** -0.5)).astype(BF16)
        s = _dot_nt(qh, k_ref[0, :, lanes])
        p = jnp.exp(s - jnp.max(s, axis=-1, keepdims=True))
        l = jnp.sum(p, axis=-1, keepdims=True)
        heads.append((_dot(p.astype(BF16), v_ref[0, :, lanes]) / l).astype(BF16))
    o = jnp.concatenate(heads, axis=-1)
    o_ref[...] = x + _dot(o, wo_ref[...])


def _post(x, mixes, w_out, gain, w_q, q_norm, mem_k, mem_v, w_o, seq, *, tm=512):
    t, d = x.shape
    b, m, width = mem_k.shape
    n_seq_tiles = seq // tm
    row = lambda w: pl.BlockSpec((tm, w), lambda i: (i, 0))
    mem_spec = pl.BlockSpec((1, m, width), lambda i: (i // n_seq_tiles, 0, 0))
    kern = functools.partial(_post_kernel, n_mix=len(mixes))
    return pl.pallas_call(
        kern,
        grid=(t // tm,),
        in_specs=[row(d)] + [row(a.shape[1]) for a in mixes] + [
            _resident(w_out.shape), _resident((1, d)), _resident((d, width)),
            _resident((1, XM_HEAD_DIM)), mem_spec, mem_spec, _resident((width, d))],
        out_specs=row(d),
        out_shape=jax.ShapeDtypeStruct((t, d), F32),
        compiler_params=_params(1, 40),
        name="post",
    )(x, *mixes, w_out.astype(BF16), gain.reshape(1, d), w_q.astype(BF16),
      q_norm.reshape(1, XM_HEAD_DIM), mem_k, mem_v, w_o.astype(BF16))


def _rope_tables(seq, dim, start, group):
    half = dim // 2
    inv = 1.0 / (ROPE_THETA ** (jnp.arange(0, dim, 2, dtype=F32) / dim))
    ang = jnp.arange(seq, dtype=F32)[:, None] * inv[None, :]
    cos, sin = jnp.cos(ang), jnp.sin(ang)
    zeros = jnp.zeros_like(sin)
    pad = lambda a, fill: jnp.concatenate(
        [jnp.full((seq, start), fill, F32), a, jnp.full((seq, group - start - dim), fill, F32)],
        axis=1)
    cos_t = pad(jnp.concatenate([cos, cos], axis=1), 1.0)
    sin_a = pad(jnp.concatenate([zeros, sin], axis=1), 0.0)
    sin_b = pad(jnp.concatenate([-sin, zeros], axis=1), 0.0)
    reps = LANES // group
    return tuple(jnp.tile(a, (1, reps)) for a in (cos_t, sin_a, sin_b))


def kernel(x, mem, ffn_norm, ffn_w_gate, ffn_w_up, ffn_w_down, mix_norm, ab_w_in, ab_w_out, diff_q_norm, diff_k_norm, diff_lambda_q1, diff_lambda_k1, diff_lambda_q2, diff_lambda_k2, diff_subln, mla_w_dq, mla_q_norm, mla_w_uq, mla_w_dkv, mla_kv_norm, mla_w_ukv, mla_qk_norm_q, mla_qk_norm_k, mla_w_o, xm_norm, xm_mem_norm, xm_w_q, xm_w_kv, xm_q_norm, xm_k_norm, xm_w_o):
    b, s, d = x.shape
    depth = ffn_norm.shape[0]
    x = x.reshape(b * s, d)
    diff_tables = _rope_tables(s, DIFF_HEAD_DIM, 0, DIFF_HEAD_DIM)
    ffn_w = tuple(w.astype(BF16) for w in (ffn_w_gate, ffn_w_up, ffn_w_down))
    for layer in range(depth):
        i = layer // 2
        x = _ffn(x, ffn_norm[layer, 0], *ffn_w, layer, 0)
        if layer % 2 == 0:
            lambda_init = 0.8 - 0.6 * math.exp(-0.3 * layer)
            proj = _ab_proj(x, mix_norm[layer], ab_w_in[i], diff_q_norm[i], diff_k_norm[i],
                            diff_tables, s).reshape(b, s, -1)
            lam_rows = jnp.stack([diff_lambda_q1[i], diff_lambda_k1[i],
                                  diff_lambda_q2[i], diff_lambda_k2[i]])
            bound = _score_bound(diff_q_norm[i], diff_k_norm[i], DIFF_HEAD_DIM)
            oa = _diff_attn(proj, bound, lam_rows, diff_subln[i], lambda_init)
            ob = _sb_attn(proj)
            mixes = [oa.reshape(b * s, -1), ob.reshape(b * s, -1)]
            w_out = ab_w_out[i]
        else:
            q, k, v = _mla_proj(x, mix_norm[layer], mla_w_dq[i], mla_q_norm[i], mla_w_uq[i],
                                mla_w_dkv[i], mla_kv_norm[i], mla_w_ukv[i], mla_qk_norm_q[i],
                                mla_qk_norm_k[i], s)
            bound = _score_bound(mla_qk_norm_q[i], mla_qk_norm_k[i], MLA_QK_DIM)
            o = _mla_attn(q.reshape(b, s, -1), k.reshape(b, s, -1), v.reshape(b, s, -1), bound)
            mixes = [o.reshape(b * s, -1)]
            w_out = mla_w_o[i]
        mem_k, mem_v = _mem_kv(mem, xm_mem_norm[layer], xm_w_kv[layer], xm_k_norm[layer])
        x = _post(x, mixes, w_out, xm_norm[layer], xm_w_q[layer], xm_q_norm[layer],
                  mem_k, mem_v, xm_w_o[layer], s)
        x = _ffn(x, ffn_norm[layer, 1], *ffn_w, layer, 1)
    return x.reshape(b, s, d)
```

```python
import functools
import math

import jax
import jax.numpy as jnp
from jax import lax
from jax.experimental import pallas as pl
from jax.experimental.pallas import tpu as pltpu

F32 = jnp.float32
BF16 = jnp.bfloat16

LANES = 128
V7X_MXU_DIM = 256
V7X_VMEM_LIMIT = 56 * 1024 * 1024

FFN_ROWS = 1024
PROJ_ROWS = 1024
ATTN_TILE = 512
SB_TILE = 256
VMEM_MB = dict(ffn=56, ab_proj=56, diff_attn=48, sb_attn=32, mla_proj=56, mla_attn=48,
               mem_kv=32, post=56)

CHUNK = 64
ROPE_THETA = 10000.0
NORM_EPS = 1e-6
SB_LOG2_ZERO = -151.0
LOG2_E = math.log2(math.e)
MAX_SOFTMAX_SHIFT = 40.0

DIFF_HEADS = 4
DIFF_HEAD_DIM = 64
SB_HEADS = 8
SB_HEAD_DIM = 64
MLA_HEADS = 16
MLA_KV_RANK = 256
MLA_NOPE_DIM = 64
MLA_ROPE_DIM = 32
MLA_V_DIM = 64
MLA_QK_DIM = MLA_NOPE_DIM + MLA_ROPE_DIM
XM_HEADS = 4
XM_HEAD_DIM = 128


def _params(n_grid, vmem_mb):
    return pltpu.CompilerParams(
        dimension_semantics=("arbitrary",) * n_grid,
        vmem_limit_bytes=min(vmem_mb * 1024 * 1024, V7X_VMEM_LIMIT))


def _resident(shape):
    nd = len(shape)
    return pl.BlockSpec(shape, lambda *_: (0,) * nd, pipeline_mode=pl.Buffered(1))


def _rms(x, gain):
    return x * lax.rsqrt(jnp.mean(x * x, axis=-1, keepdims=True) + NORM_EPS) * gain


def _dot(a, b):
    return jnp.dot(a, b, preferred_element_type=F32)


def _dot_nt(a, b):
    return lax.dot_general(a, b, (((1,), (1,)), ((), ())), preferred_element_type=F32)


def _lane_iota(shape):
    return lax.broadcasted_iota(jnp.int32, shape, len(shape) - 1)


def _rope(x, cos, sin_a, sin_b, half):
    return x * cos + pltpu.roll(x, half, 1) * sin_a + pltpu.roll(x, LANES - half, 1) * sin_b


def _ffn_kernel(x_ref, g_ref, wg_ref, wu_ref, wd_ref, o_ref, h_ref, acc_ref, *, tf):
    x = x_ref[...]
    h_ref[...] = _rms(x, g_ref[...]).astype(BF16)
    for j in range(wg_ref.shape[1] // tf):
        cols = slice(j * tf, (j + 1) * tf)
        h = h_ref[...]
        g = _dot(h, wg_ref[:, cols])
        u = _dot(h, wu_ref[:, cols])
        a = (g * jax.nn.sigmoid(g) * u).astype(BF16)
        down = _dot(a, wd_ref[cols, :])
        if j == 0:
            acc_ref[...] = down
        else:
            acc_ref[...] += down
    o_ref[...] = x + 0.5 * acc_ref[...]


def _ffn(x, gain, w_gate, w_up, w_down, layer, half, *, tm=FFN_ROWS, tf=V7X_MXU_DIM):
    t, d = x.shape
    d_ff = w_gate.shape[-1]
    pick = lambda r, c: pl.BlockSpec((None, None, r, c), lambda i: (layer, half, 0, 0),
                                     pipeline_mode=pl.Buffered(1))
    return pl.pallas_call(
        functools.partial(_ffn_kernel, tf=tf),
        grid=(t // tm,),
        in_specs=[
            pl.BlockSpec((tm, d), lambda i: (i, 0)),
            _resident((1, d)),
            pick(d, d_ff), pick(d, d_ff), pick(d_ff, d),
        ],
        out_specs=pl.BlockSpec((tm, d), lambda i: (i, 0)),
        out_shape=jax.ShapeDtypeStruct((t, d), F32),
        scratch_shapes=[pltpu.VMEM((tm, d), BF16), pltpu.VMEM((tm, d), F32)],
        compiler_params=_params(1, VMEM_MB["ffn"]),
        name="ffn",
    )(x, gain.reshape(1, d), w_gate, w_up, w_down)


def _ab_proj_kernel(x_ref, g_ref, w_ref, qn_ref, kn_ref, cos_ref, sa_ref, sb_ref, o_ref, *,
                    n_qk_groups, q_scales):
    h = _rms(x_ref[...], g_ref[...]).astype(BF16)
    cos, sa, sb = cos_ref[...], sa_ref[...], sb_ref[...]
    n_groups = w_ref.shape[1] // LANES
    lane = _lane_iota((x_ref.shape[0], LANES))
    low = lane < DIFF_HEAD_DIM
    for c in range(0, n_groups, 2):
        y2 = _dot(h, w_ref[:, c * LANES:(c + 2) * LANES])
        for k in range(2):
            g = c + k
            y = y2[:, k * LANES:(k + 1) * LANES]
            if g < 2 * n_qk_groups:
                gain = qn_ref[...] if g < n_qk_groups else kn_ref[...]
                sq = y * y
                ss_lo = jnp.sum(jnp.where(low, sq, 0.0), axis=-1, keepdims=True)
                ss_hi = jnp.sum(jnp.where(low, 0.0, sq), axis=-1, keepdims=True)
                ms = jnp.where(low, ss_lo, ss_hi) * (1.0 / DIFF_HEAD_DIM)
                y = y * lax.rsqrt(ms + NORM_EPS) * gain
                y = _rope(y, cos, sa, sb, DIFF_HEAD_DIM // 2)
            if g in q_scales:
                y = y * q_scales[g]
            o_ref[:, g * LANES:(g + 1) * LANES] = y.astype(BF16)


def _ab_proj(x, gain, w_in, q_norm, k_norm, tables, seq, *, tm=PROJ_ROWS):
    t, d = x.shape
    n = w_in.shape[1]
    n_qk_groups = 2 * DIFF_HEADS * DIFF_HEAD_DIM // LANES
    sb_q0 = (4 * DIFF_HEADS * DIFF_HEAD_DIM + DIFF_HEADS * 2 * DIFF_HEAD_DIM) // LANES
    q_scales = {g: DIFF_HEAD_DIM ** -0.5 for g in range(n_qk_groups)}
    q_scales.update({g: SB_HEAD_DIM ** -0.5 * LOG2_E
                     for g in range(sb_q0, sb_q0 + SB_HEADS * SB_HEAD_DIM // LANES)})
    cos, sa, sb = tables
    n_seq_tiles = seq // tm
    tab_spec = pl.BlockSpec((tm, LANES), lambda i: (i % n_seq_tiles, 0))
    kern = functools.partial(_ab_proj_kernel, n_qk_groups=n_qk_groups,
                             q_scales=q_scales)
    return pl.pallas_call(
        kern,
        grid=(t // tm,),
        in_specs=[
            pl.BlockSpec((tm, d), lambda i: (i, 0)),
            _resident((1, d)),
            _resident((d, n)),
            _resident((1, LANES)),
            _resident((1, LANES)),
            tab_spec, tab_spec, tab_spec,
        ],
        out_specs=pl.BlockSpec((tm, n), lambda i: (i, 0)),
        out_shape=jax.ShapeDtypeStruct((t, n), BF16),
        compiler_params=_params(1, VMEM_MB["ab_proj"]),
        name="ab_proj",
    )(x, gain.reshape(1, d), w_in.astype(BF16),
      jnp.tile(q_norm, 2).reshape(1, LANES), jnp.tile(k_norm, 2).reshape(1, LANES),
      cos, sa, sb)


def _chunk_causal_mask(tq, tk):
    qc = lax.broadcasted_iota(jnp.int32, (tq, tk), 0) // CHUNK
    kc = lax.broadcasted_iota(jnp.int32, (tq, tk), 1) // CHUNK
    return kc <= qc


def _fold_lanes(p):
    out = p[:, :LANES]
    for c in range(1, p.shape[1] // LANES):
        out = out + p[:, c * LANES:(c + 1) * LANES]
    return out


def _attend_pair(bound, load_q, k_srcs, v_ref, store_out, acc_refs, l_refs, m_refs, n_q, tq):
    def key_block(h, rows):
        ref, lanes = k_srcs[h]
        return ref[0, rows, lanes]

    @pl.when(bound <= MAX_SOFTMAX_SHIFT)
    def _():
        hq = tq // 2
        lower_rows = lambda a, b: jnp.concatenate([a[:hq], a[hq:] + b], axis=0)
        for qi in range(n_q):
            q_rows = slice(qi * tq, (qi + 1) * tq)
            qs = load_q(q_rows)
            acc, lsum = [None, None], [None, None]
            for kb in range(qi):
                rows = slice(kb * tq, (kb + 1) * tq)
                v = v_ref[0, rows, :]
                for h in range(2):
                    p = jnp.exp(_dot_nt(qs[h], key_block(h, rows)) - bound)
                    part, pv = _fold_lanes(p), _dot(p.astype(BF16), v)
                    lsum[h] = part if kb == 0 else lsum[h] + part
                    acc[h] = pv if kb == 0 else acc[h] + pv
            rows_a = slice(qi * tq, qi * tq + hq)
            rows_b = slice(qi * tq + hq, (qi + 1) * tq)
            v_a, v_b = v_ref[0, rows_a, :], v_ref[0, rows_b, :]
            for h in range(2):
                p_a = jnp.exp(_dot_nt(qs[h], key_block(h, rows_a)) - bound)
                p_a = jnp.where(_chunk_causal_mask(tq, hq), p_a, 0.0)
                p_b = jnp.exp(_dot_nt(qs[h][hq:], key_block(h, rows_b)) - bound)
                p_b = jnp.where(_chunk_causal_mask(hq, hq), p_b, 0.0)
                part = lower_rows(_fold_lanes(p_a), _fold_lanes(p_b))
                pv = lower_rows(_dot(p_a.astype(BF16), v_a), _dot(p_b.astype(BF16), v_b))
                lsum[h] = part if qi == 0 else lsum[h] + part
                acc[h] = pv if qi == 0 else acc[h] + pv
            store_out(q_rows, *(acc[h] / jnp.sum(lsum[h], axis=-1, keepdims=True)
                                for h in range(2)))

    @pl.when(bound > MAX_SOFTMAX_SHIFT)
    def _():
        def q_tile(qi, carry):
            q_rows = pl.ds(pl.multiple_of(qi * tq, tq), tq)
            qs = load_q(q_rows)
            for h in range(2):
                acc_refs[h][...] = jnp.zeros_like(acc_refs[h])
                l_refs[h][...] = jnp.zeros_like(l_refs[h])
                m_refs[h][...] = jnp.full_like(m_refs[h], -jnp.inf)

            def block(kb, masked):
                rows = pl.ds(pl.multiple_of(kb * tq, tq), tq)
                v = v_ref[0, rows, :]
                for h in range(2):
                    s = _dot_nt(qs[h], key_block(h, rows))
                    if masked:
                        s = jnp.where(_chunk_causal_mask(tq, tq), s, -jnp.inf)
                    m_old = m_refs[h][...]
                    m_new = jnp.maximum(m_old, jnp.max(s, axis=-1, keepdims=True))
                    alpha = jnp.exp(m_old - m_new)
                    p = jnp.exp(s - m_new)
                    l_refs[h][...] = alpha * l_refs[h][...] + jnp.sum(p, axis=-1, keepdims=True)
                    acc_refs[h][...] = alpha * acc_refs[h][...] + _dot(p.astype(BF16), v)
                    m_refs[h][...] = m_new

            def body(kb, c):
                block(kb, False)
                return c

            lax.fori_loop(0, qi, body, 0)
            block(qi, True)
            store_out(q_rows, *(acc_refs[h][...] / l_refs[h][...] for h in range(2)))
            return carry

        lax.fori_loop(0, n_q, q_tile, 0)


def _pair_scratch(tq):
    acc = pltpu.VMEM((tq, LANES), F32)
    stat = pltpu.VMEM((tq, 1), F32)
    return [acc, acc, stat, stat, stat, stat]


def _score_bound(q_gain, k_gain, dim):
    return (jnp.max(jnp.abs(q_gain)) * jnp.max(jnp.abs(k_gain)) * math.sqrt(dim)).reshape(1)


def _diff_attn_kernel(bound_ref, q1_ref, q2_ref, k1_ref, k2_ref, v_ref, lam_ref, sub_ref, o_ref,
                      a1_ref, a2_ref, l1_ref, l2_ref, m1_ref, m2_ref, *, tq, lambda_init):
    j = pl.program_id(1)
    mine = (_lane_iota((tq, LANES)) >= DIFF_HEAD_DIM).astype(jnp.int32) == j % 2
    lam_v = lam_ref[...]
    lam = (jnp.exp(jnp.sum(lam_v[0:1] * lam_v[1:2], axis=-1, keepdims=True))
           - jnp.exp(jnp.sum(lam_v[2:3] * lam_v[3:4], axis=-1, keepdims=True)) + lambda_init)

    def load_q(rows):
        return tuple(jnp.where(mine, q_ref[0, rows, :].astype(F32), 0.0).astype(BF16)
                     for q_ref in (q1_ref, q2_ref))

    def store_out(rows, o1, o2):
        o = _rms(o1 - lam * o2, sub_ref[...]) * (1.0 - lambda_init)
        o_ref[0, rows, :] = o.astype(BF16)

    whole = slice(None)
    _attend_pair(bound_ref[0], load_q, ((k1_ref, whole), (k2_ref, whole)), v_ref, store_out,
                 (a1_ref, a2_ref), (l1_ref, l2_ref), (m1_ref, m2_ref), q1_ref.shape[1] // tq, tq)


def _diff_attn(proj, bound, lam_rows, subln, lambda_init, *, tq=ATTN_TILE):
    b, s, _ = proj.shape
    k_off = 2 * DIFF_HEADS * DIFF_HEAD_DIM // LANES
    v_off = 2 * k_off
    half = DIFF_HEADS // 2
    kern = functools.partial(_diff_attn_kernel, tq=tq, lambda_init=lambda_init)
    cols = lambda f: pl.BlockSpec((1, s, LANES), lambda bi, j: (bi, 0, f(j)))
    return pl.pallas_call(
        kern,
        grid=(b, DIFF_HEADS),
        in_specs=[
            pl.BlockSpec(memory_space=pltpu.SMEM),
            cols(lambda j: j // 2),
            cols(lambda j: half + j // 2),
            cols(lambda j: k_off + j // 2),
            cols(lambda j: k_off + half + j // 2),
            cols(lambda j: v_off + j),
            _resident((4, DIFF_HEAD_DIM)),
            _resident((1, LANES)),
        ],
        out_specs=cols(lambda j: j),
        out_shape=jax.ShapeDtypeStruct((b, s, DIFF_HEADS * LANES), BF16),
        scratch_shapes=_pair_scratch(tq),
        compiler_params=_params(2, VMEM_MB["diff_attn"]),
        name="diff_attn",
    )(bound, proj, proj, proj, proj, proj, lam_rows, subln.reshape(1, LANES))


def _split2(x):
    hi = x.astype(BF16)
    return hi, (x - hi.astype(F32)).astype(BF16)


def _sb_block(q, k, v, later_than, c_in, strict):
    z = _dot_nt(q, k)
    sign_bit = jnp.uint32(0x80000000)
    neg_abs = lax.bitcast_convert_type(lax.bitcast_convert_type(z, jnp.uint32) | sign_bit, F32)
    log_beta = jnp.minimum(z, 0.0) - jnp.log2(1.0 + jnp.exp2(neg_abs))
    log_keep = log_beta - z
    if strict is not None:
        log_keep = jnp.where(strict, log_keep, 0.0)
    hi, mid = _split2(log_keep)
    later = _dot(hi, later_than) + _dot(mid, later_than)
    expo = log_beta + later
    if c_in is not None:
        expo = expo + c_in
    w = jnp.exp2(expo)
    if strict is not None:
        w = jnp.where(strict, w, 0.0)
    return _dot(w.astype(BF16), v), jnp.sum(log_keep, axis=-1, keepdims=True)


def _sb_attn_kernel(q_ref, k_ref, v_ref, o_ref, c_ref, acc_ref, *, tq, n_groups):
    qi = pl.program_id(2)
    low = _lane_iota((tq, LANES)) < SB_HEAD_DIM
    q_heads = []
    for g in range(n_groups):
        q = q_ref[0, :, g * LANES:(g + 1) * LANES].astype(F32)
        q_heads += [jnp.where(low, q, 0.0).astype(BF16), jnp.where(low, 0.0, q).astype(BF16)]
    row = lax.broadcasted_iota(jnp.int32, (tq, tq), 0)
    col = lax.broadcasted_iota(jnp.int32, (tq, tq), 1)
    strict = col < row
    later_than = (row > col).astype(BF16)

    def key_rows(kb):
        return pl.ds(pl.multiple_of(kb * tq, tq), tq)

    def group_block(rows, g):
        lanes = slice(g * LANES, (g + 1) * LANES)
        return k_ref[0, rows, lanes], v_ref[0, rows, lanes]

    has_prev = qi > 0
    rows_a, rows_b = key_rows(qi), key_rows(jnp.maximum(qi - 1, 0))
    for g in range(n_groups):
        (k_a, v_a), (k_b, v_b) = group_block(rows_a, g), group_block(rows_b, g)
        outs = []
        for h in (2 * g, 2 * g + 1):
            pv_a, sum_a = _sb_block(q_heads[h], k_a, v_a, later_than, None, strict)
            pv_b, sum_b = _sb_block(q_heads[h], k_b, v_b, later_than, sum_a, None)
            outs.append(pv_a + jnp.where(has_prev, pv_b, 0.0))
            c_ref[:, h:h + 1] = sum_a + jnp.where(has_prev, sum_b, 0.0)
        acc_ref[:, g * LANES:(g + 1) * LANES] = jnp.where(low, outs[0], outs[1])

    def live():
        return jnp.max(c_ref[...]) > SB_LOG2_ZERO

    def cond(state):
        kb, go = state
        return jnp.logical_and(kb >= 0, go)

    def body(state):
        kb, _ = state
        rows = key_rows(kb)
        for g in range(n_groups):
            k, v = group_block(rows, g)
            outs = []
            for h in (2 * g, 2 * g + 1):
                c_old = c_ref[:, h:h + 1]
                pv, row_sum = _sb_block(q_heads[h], k, v, later_than, c_old, None)
                outs.append(pv)
                c_ref[:, h:h + 1] = c_old + row_sum
            acc_ref[:, g * LANES:(g + 1) * LANES] += jnp.where(low, outs[0], outs[1])
        return kb - 1, live()

    lax.while_loop(cond, body, (qi - 2, live()))
    o_ref[0] = acc_ref[...].astype(BF16)


def _sb_attn(proj, *, tq=SB_TILE, n_groups=2):
    b, s, _ = proj.shape
    width = n_groups * LANES
    q_off = (4 * DIFF_HEADS * DIFF_HEAD_DIM + DIFF_HEADS * 2 * DIFF_HEAD_DIM) // width
    n_steps = SB_HEADS * SB_HEAD_DIM // width
    kern = functools.partial(_sb_attn_kernel, tq=tq, n_groups=n_groups)
    return pl.pallas_call(
        kern,
        grid=(b, n_steps, s // tq),
        in_specs=[
            pl.BlockSpec((1, tq, width), lambda bi, p, qi: (bi, qi, q_off + p)),
            pl.BlockSpec((1, s, width), lambda bi, p, qi: (bi, 0, q_off + n_steps + p)),
            pl.BlockSpec((1, s, width), lambda bi, p, qi: (bi, 0, q_off + 2 * n_steps + p)),
        ],
        out_specs=pl.BlockSpec((1, tq, width), lambda bi, p, qi: (bi, qi, p)),
        out_shape=jax.ShapeDtypeStruct((b, s, n_steps * width), BF16),
        scratch_shapes=[pltpu.VMEM((tq, 2 * n_groups), F32), pltpu.VMEM((tq, width), F32)],
        compiler_params=_params(3, VMEM_MB["sb_attn"]),
        name="sb_attn",
    )(proj, proj, proj)


def _mla_proj_kernel(x_ref, g_ref, wdq_ref, qln_ref, wuq_ref, wdkv_ref, kvln_ref,
                     wuk_ref, wuv_ref, qsum_ref, ksum_ref, gkn_ref, qc_ref, qs_ref,
                     kc_ref, ksa_ref, ksb_ref, q_ref, k_ref, v_ref):
    tm = x_ref.shape[0]
    h = _rms(x_ref[...], g_ref[...]).astype(BF16)
    inv_dim = 1.0 / MLA_QK_DIM
    c_q = _rms(_dot(h, wdq_ref[...]), qln_ref[...]).astype(BF16)
    dkv = _dot(h, wdkv_ref[...])
    c_kv = _rms(dkv[:, :MLA_KV_RANK], kvln_ref[...]).astype(BF16)
    kr = dkv[:, MLA_KV_RANK:]
    kr_sq = 0.5 * jnp.sum(kr * kr, axis=-1, keepdims=True)
    half = MLA_ROPE_DIM // 2
    kr_rot = (kr * kc_ref[...] + pltpu.roll(kr, half, 1) * ksa_ref[...]
              + pltpu.roll(kr, LANES - half, 1) * ksb_ref[...])
    v_ref[...] = _dot(c_kv, wuv_ref[...]).astype(BF16)
    low = _lane_iota((tm, LANES)) < MLA_NOPE_DIM
    qc, qs, gkn = qc_ref[...], qs_ref[...], gkn_ref[...]
    for p2 in range(0, MLA_HEADS // 2, 2):
        kn4 = _dot(c_kv, wuk_ref[:, p2 * LANES:(p2 + 2) * LANES])
        for i in range(2):
            p = p2 + i
            cols = slice(p * 2 * LANES, (p + 1) * 2 * LANES)
            q4 = _dot(c_q, wuq_ref[:, 2 * p * 2 * LANES:2 * (p + 1) * 2 * LANES])
            q2, q2s = q4[:, :2 * LANES], q4[:, 2 * LANES:]
            q_ms = _dot((q2 * q2).astype(BF16), qsum_ref[...]) * inv_dim
            q_ref[:, cols] = ((q2 * qc + q2s * qs) * lax.rsqrt(q_ms + NORM_EPS)).astype(BF16)
            kn2 = kn4[:, i * LANES:(i + 1) * LANES]
            k_ms = (_dot((kn2 * kn2).astype(BF16), ksum_ref[...]) + kr_sq) * inv_dim
            rk = lax.rsqrt(k_ms + NORM_EPS)
            kg = kn2 * gkn
            k_ref[:, p * 2 * LANES:p * 2 * LANES + LANES] = (
                jnp.where(low, kg, kr_rot) * rk[:, :LANES]).astype(BF16)
            k_ref[:, p * 2 * LANES + LANES:(p + 1) * 2 * LANES] = (
                jnp.where(low, kr_rot, kg) * rk[:, LANES:]).astype(BF16)


def _mla_pair_layout(even, odd_first, odd_last, n_rows):
    z = lambda w: jnp.zeros((n_rows, w), F32)
    return jnp.concatenate([even, z(LANES - MLA_QK_DIM), odd_first, z(MLA_ROPE_DIM), odd_last], axis=1)


def _mla_proj(x, gain, w_dq, q_lat_norm, w_uq, w_dkv, kv_lat_norm, w_ukv, qk_norm_q, qk_norm_k,
              seq, *, tm=PROJ_ROWS):
    t, d = x.shape
    q_rank = w_dq.shape[1]
    half = MLA_ROPE_DIM // 2
    n_pairs = MLA_HEADS // 2
    swap = lambda a: jnp.concatenate([a[..., half:], a[..., :half]], axis=-1)

    uq = w_uq.reshape(q_rank, MLA_HEADS, MLA_QK_DIM)
    nope, rope = uq[:, :, :MLA_NOPE_DIM], uq[:, :, MLA_NOPE_DIM:]
    rope_sw = jnp.concatenate([-rope[..., half:], rope[..., :half]], axis=-1)
    zeros = jnp.zeros_like
    ev_n, ev_r, od_r, od_n = nope[:, 0::2], rope[:, 0::2], rope[:, 1::2], nope[:, 1::2]
    wuq = jnp.concatenate(
        [ev_n, ev_r, zeros(ev_r), od_r, zeros(od_r), od_n,
         zeros(ev_n), rope_sw[:, 0::2], zeros(ev_r), rope_sw[:, 1::2], zeros(od_r), zeros(od_n)],
        axis=-1).reshape(q_rank, -1).astype(BF16)

    ukv = w_ukv.reshape(MLA_KV_RANK, MLA_HEADS, MLA_NOPE_DIM + MLA_V_DIM)
    wuk = ukv[:, :, :MLA_NOPE_DIM].reshape(MLA_KV_RANK, -1).astype(BF16)
    wuv = ukv[:, :, MLA_NOPE_DIM:].reshape(MLA_KV_RANK, -1).astype(BF16)
    kr_w = w_dkv[:, MLA_KV_RANK:]
    kr_pad = jnp.zeros((d, MLA_ROPE_DIM), F32)
    wdkv = jnp.concatenate([w_dkv[:, :MLA_KV_RANK], kr_w, kr_pad, kr_w, kr_pad], axis=1).astype(BF16)

    r256 = jnp.arange(2 * LANES)
    qsum = (r256[:, None] // LANES == r256[None, :] // LANES).astype(BF16)
    r128 = jnp.arange(LANES)
    ksum = (r128[:, None] // MLA_NOPE_DIM == r256[None, :] // LANES).astype(BF16)

    inv = 1.0 / (ROPE_THETA ** (jnp.arange(0, MLA_ROPE_DIM, 2, dtype=F32) / MLA_ROPE_DIM))
    ang = jnp.arange(seq, dtype=F32)[:, None] * inv[None, :]
    cos, sin = jnp.tile(jnp.cos(ang), (1, 2)), jnp.tile(jnp.sin(ang), (1, 2))
    scale = MLA_QK_DIM ** -0.5
    gq_n, gq_r = qk_norm_q[:MLA_NOPE_DIM] * scale, qk_norm_q[MLA_NOPE_DIM:] * scale
    gk_n, gk_r = qk_norm_k[:MLA_NOPE_DIM], qk_norm_k[MLA_NOPE_DIM:]
    bq_n = jnp.broadcast_to(gq_n, (seq, MLA_NOPE_DIM))
    qc = _mla_pair_layout(jnp.concatenate([bq_n, cos * gq_r], axis=1), cos * gq_r, bq_n, seq)
    qs = _mla_pair_layout(jnp.concatenate([zeros(bq_n), sin * swap(gq_r)], axis=1),
                          sin * swap(gq_r), zeros(bq_n), seq)
    first = jnp.arange(MLA_ROPE_DIM) < half
    k_cos = cos * gk_r
    k_sa = jnp.where(first, 0.0, sin * swap(gk_r))
    k_sb = jnp.where(first, -sin * swap(gk_r), 0.0)
    twice = lambda a: jnp.concatenate([a, zeros(a), a, zeros(a)], axis=1)
    n_seq_tiles = seq // tm
    tab = lambda w: pl.BlockSpec((tm, w), lambda i: (i % n_seq_tiles, 0))
    hw = MLA_HEADS * LANES
    vw = MLA_HEADS * MLA_V_DIM
    row = lambda w: pl.BlockSpec((tm, w), lambda i: (i, 0))
    return pl.pallas_call(
        _mla_proj_kernel,
        grid=(t // tm,),
        in_specs=[
            row(d), _resident((1, d)),
            _resident((d, q_rank)), _resident((1, q_rank)), _resident((q_rank, 2 * hw)),
            _resident((d, MLA_KV_RANK + LANES)), _resident((1, MLA_KV_RANK)),
            _resident((MLA_KV_RANK, n_pairs * LANES)), _resident((MLA_KV_RANK, vw)),
            _resident((2 * LANES, 2 * LANES)), _resident((LANES, 2 * LANES)), _resident((1, LANES)),
            tab(2 * LANES), tab(2 * LANES), tab(LANES), tab(LANES), tab(LANES),
        ],
        out_specs=[row(hw), row(hw), row(vw)],
        out_shape=[jax.ShapeDtypeStruct((t, hw), BF16), jax.ShapeDtypeStruct((t, hw), BF16),
                   jax.ShapeDtypeStruct((t, vw), BF16)],
        compiler_params=_params(1, VMEM_MB["mla_proj"]),
        name="mla_proj",
    )(x, gain.reshape(1, d), w_dq.astype(BF16), q_lat_norm.reshape(1, q_rank), wuq, wdkv,
      kv_lat_norm.reshape(1, MLA_KV_RANK), wuk, wuv, qsum, ksum,
      jnp.tile(gk_n, 2).reshape(1, LANES), qc, qs, twice(k_cos), twice(k_sa), twice(k_sb))


def _mla_attn_kernel(bound_ref, q_ref, k_ref, v_ref, o_ref,
                     a0_ref, a1_ref, l0_ref, l1_ref, m0_ref, m1_ref, *, tq):
    low = _lane_iota((tq, LANES)) < MLA_V_DIM
    lanes = (slice(0, LANES), slice(LANES, 2 * LANES))

    def load_q(rows):
        return tuple(q_ref[0, rows, ln] for ln in lanes)

    def store_out(rows, o0, o1):
        o_ref[0, rows, :] = jnp.where(low, o0, o1).astype(BF16)

    _attend_pair(bound_ref[0], load_q, ((k_ref, lanes[0]), (k_ref, lanes[1])), v_ref, store_out,
                 (a0_ref, a1_ref), (l0_ref, l1_ref), (m0_ref, m1_ref), q_ref.shape[1] // tq, tq)


def _mla_attn(q, k, v, bound, *, tq=ATTN_TILE):
    b, s, _ = q.shape
    n_pairs = MLA_HEADS // 2
    kern = functools.partial(_mla_attn_kernel, tq=tq)
    return pl.pallas_call(
        kern,
        grid=(b, n_pairs),
        in_specs=[
            pl.BlockSpec(memory_space=pltpu.SMEM),
            pl.BlockSpec((1, s, 2 * LANES), lambda bi, p: (bi, 0, p)),
            pl.BlockSpec((1, s, 2 * LANES), lambda bi, p: (bi, 0, p)),
            pl.BlockSpec((1, s, LANES), lambda bi, p: (bi, 0, p)),
        ],
        out_specs=pl.BlockSpec((1, s, LANES), lambda bi, p: (bi, 0, p)),
        out_shape=jax.ShapeDtypeStruct((b, s, n_pairs * LANES), BF16),
        scratch_shapes=_pair_scratch(tq),
        compiler_params=_params(2, VMEM_MB["mla_attn"]),
        name="mla_attn",
    )(bound, q, k, v)


def _mem_kv_kernel(m_ref, g_ref, w_ref, kn_ref, k_ref, v_ref):
    h = _rms(m_ref[0], g_ref[...]).astype(BF16)
    kv = _dot(h, w_ref[...])
    width = XM_HEADS * XM_HEAD_DIM
    for hd in range(XM_HEADS):
        lanes = slice(hd * LANES, (hd + 1) * LANES)
        k_ref[0, :, lanes] = _rms(kv[:, lanes], kn_ref[...]).astype(BF16)
    v_ref[0] = kv[:, width:].astype(BF16)


def _mem_kv(mem, gain, w_kv, k_norm):
    b, m, d = mem.shape
    width = XM_HEADS * XM_HEAD_DIM
    out = jax.ShapeDtypeStruct((b, m, width), BF16)
    blk = pl.BlockSpec((1, m, width), lambda i: (i, 0, 0))
    return pl.pallas_call(
        _mem_kv_kernel,
        grid=(b,),
        in_specs=[pl.BlockSpec((1, m, d), lambda i: (i, 0, 0)), _resident((1, d)),
                  _resident((d, 2 * width)), _resident((1, XM_HEAD_DIM))],
        out_specs=[blk, blk],
        out_shape=[out, out],
        compiler_params=_params(1, VMEM_MB["mem_kv"]),
        name="mem_kv",
    )(mem, gain.reshape(1, d), w_kv.astype(BF16), k_norm.reshape(1, XM_HEAD_DIM))


def _post_kernel(*refs, n_mix):
    x_ref = refs[0]
    mix_refs = refs[1:1 + n_mix]
    wout_ref, g_ref, wq_ref, qn_ref, k_ref, v_ref, wo_ref, o_ref = refs[1 + n_mix:]
    x = x_ref[...]
    off = 0
    for a_ref in mix_refs:
        w = a_ref.shape[1]
        x = x + _dot(a_ref[...], wout_ref[off:off + w, :])
        off += w
    h = _rms(x, g_ref[...]).astype(BF16)
    q = _dot(h, wq_ref[...])
    heads = []
    for hd in range(XM_HEADS):
        lanes = slice(hd * LANES, (hd + 1) * LANES)
        qh = (_rms(q[:, lanes], qn_ref[...]) * (XM_HEAD_DIM ** -0.5)).astype(BF16)
        s = _dot_nt(qh, k_ref[0, :, lanes])
        p = jnp.exp(s - jnp.max(s, axis=-1, keepdims=True))
        l = jnp.sum(p, axis=-1, keepdims=True)
        heads.append((_dot(p.astype(BF16), v_ref[0, :, lanes]) / l).astype(BF16))
    o = jnp.concatenate(heads, axis=-1)
    o_ref[...] = x + _dot(o, wo_ref[...])


def _post(x, mixes, w_out, gain, w_q, q_norm, mem_k, mem_v, w_o, seq, *, tm=PROJ_ROWS):
    t, d = x.shape
    b, m, width = mem_k.shape
    n_seq_tiles = seq // tm
    row = lambda w: pl.BlockSpec((tm, w), lambda i: (i, 0))
    mem_spec = pl.BlockSpec((1, m, width), lambda i: (i // n_seq_tiles, 0, 0))
    kern = functools.partial(_post_kernel, n_mix=len(mixes))
    return pl.pallas_call(
        kern,
        grid=(t // tm,),
        in_specs=[row(d)] + [row(a.shape[1]) for a in mixes] + [
            _resident(w_out.shape), _resident((1, d)), _resident((d, width)),
            _resident((1, XM_HEAD_DIM)), mem_spec, mem_spec, _resident((width, d))],
        out_specs=row(d),
        out_shape=jax.ShapeDtypeStruct((t, d), F32),
        compiler_params=_params(1, VMEM_MB["post"]),
        name="post",
    )(x, *mixes, w_out.astype(BF16), gain.reshape(1, d), w_q.astype(BF16),
      q_norm.reshape(1, XM_HEAD_DIM), mem_k, mem_v, w_o.astype(BF16))


def _rope_tables(seq, dim, start, group):
    half = dim // 2
    inv = 1.0 / (ROPE_THETA ** (jnp.arange(0, dim, 2, dtype=F32) / dim))
    ang = jnp.arange(seq, dtype=F32)[:, None] * inv[None, :]
    cos, sin = jnp.cos(ang), jnp.sin(ang)
    zeros = jnp.zeros_like(sin)
    pad = lambda a, fill: jnp.concatenate(
        [jnp.full((seq, start), fill, F32), a, jnp.full((seq, group - start - dim), fill, F32)],
        axis=1)
    cos_t = pad(jnp.concatenate([cos, cos], axis=1), 1.0)
    sin_a = pad(jnp.concatenate([zeros, sin], axis=1), 0.0)
    sin_b = pad(jnp.concatenate([-sin, zeros], axis=1), 0.0)
    reps = LANES // group
    return tuple(jnp.tile(a, (1, reps)) for a in (cos_t, sin_a, sin_b))


def kernel(x, mem, ffn_norm, ffn_w_gate, ffn_w_up, ffn_w_down, mix_norm, ab_w_in, ab_w_out, diff_q_norm, diff_k_norm, diff_lambda_q1, diff_lambda_k1, diff_lambda_q2, diff_lambda_k2, diff_subln, mla_w_dq, mla_q_norm, mla_w_uq, mla_w_dkv, mla_kv_norm, mla_w_ukv, mla_qk_norm_q, mla_qk_norm_k, mla_w_o, xm_norm, xm_mem_norm, xm_w_q, xm_w_kv, xm_q_norm, xm_k_norm, xm_w_o):
    b, s, d = x.shape
    depth = ffn_norm.shape[0]
    x = x.reshape(b * s, d)
    diff_tables = _rope_tables(s, DIFF_HEAD_DIM, 0, DIFF_HEAD_DIM)
    ffn_w = tuple(w.astype(BF16) for w in (ffn_w_gate, ffn_w_up, ffn_w_down))
    for layer in range(depth):
        i = layer // 2
        x = _ffn(x, ffn_norm[layer, 0], *ffn_w, layer, 0)
        if layer % 2 == 0:
            lambda_init = 0.8 - 0.6 * math.exp(-0.3 * layer)
            proj = _ab_proj(x, mix_norm[layer], ab_w_in[i], diff_q_norm[i], diff_k_norm[i],
                            diff_tables, s).reshape(b, s, -1)
            lam_rows = jnp.stack([diff_lambda_q1[i], diff_lambda_k1[i],
                                  diff_lambda_q2[i], diff_lambda_k2[i]])
            bound = _score_bound(diff_q_norm[i], diff_k_norm[i], DIFF_HEAD_DIM)
            oa = _diff_attn(proj, bound, lam_rows, diff_subln[i], lambda_init)
            ob = _sb_attn(proj)
            mixes = [oa.reshape(b * s, -1), ob.reshape(b * s, -1)]
            w_out = ab_w_out[i]
        else:
            q, k, v = _mla_proj(x, mix_norm[layer], mla_w_dq[i], mla_q_norm[i], mla_w_uq[i],
                                mla_w_dkv[i], mla_kv_norm[i], mla_w_ukv[i], mla_qk_norm_q[i],
                                mla_qk_norm_k[i], s)
            bound = _score_bound(mla_qk_norm_q[i], mla_qk_norm_k[i], MLA_QK_DIM)
            o = _mla_attn(q.reshape(b, s, -1), k.reshape(b, s, -1), v.reshape(b, s, -1), bound)
            mixes = [o.reshape(b * s, -1)]
            w_out = mla_w_o[i]
        mem_k, mem_v = _mem_kv(mem, xm_mem_norm[layer], xm_w_kv[layer], xm_k_norm[layer])
        x = _post(x, mixes, w_out, xm_norm[layer], xm_w_q[layer], xm_q_norm[layer],
                  mem_k, mem_v, xm_w_o[layer], s)
        x = _ffn(x, ffn_norm[layer, 1], *ffn_w, layer, 1)
    return x.reshape(b, s, d)
```

```python
import functools
import math

import jax
import jax.numpy as jnp
from jax import lax
from jax.experimental import pallas as pl
from jax.experimental.pallas import tpu as pltpu

F32 = jnp.float32
BF16 = jnp.bfloat16

LANES = 128
V7X_MXU_DIM = 256
V7X_VMEM_LIMIT = 56 * 1024 * 1024

FFN_ROWS = 1024
PROJ_ROWS = 1024
ATTN_TILE = 512
SB_TILE = 256
VMEM_MB = dict(ffn=56, ab_proj=56, diff_attn=48, sb_attn=32, mla_proj=56, mla_attn=48,
               mem_kv=32, post=56)

CHUNK = 64
ROPE_THETA = 10000.0
NORM_EPS = 1e-6
SB_LOG2_ZERO = -151.0
LOG2_E = math.log2(math.e)
MAX_SOFTMAX_SHIFT = 40.0

DIFF_HEADS = 4
DIFF_HEAD_DIM = 64
SB_HEADS = 8
SB_HEAD_DIM = 64
MLA_HEADS = 16
MLA_KV_RANK = 256
MLA_NOPE_DIM = 64
MLA_ROPE_DIM = 32
MLA_V_DIM = 64
MLA_QK_DIM = MLA_NOPE_DIM + MLA_ROPE_DIM
XM_HEADS = 4
XM_HEAD_DIM = 128


def _params(n_grid, vmem_mb):
    return pltpu.CompilerParams(
        dimension_semantics=("arbitrary",) * n_grid,
        vmem_limit_bytes=min(vmem_mb * 1024 * 1024, V7X_VMEM_LIMIT))


def _resident(shape):
    nd = len(shape)
    return pl.BlockSpec(shape, lambda *_: (0,) * nd, pipeline_mode=pl.Buffered(1))


def _rms(x, gain):
    return x * lax.rsqrt(jnp.mean(x * x, axis=-1, keepdims=True) + NORM_EPS) * gain


def _dot(a, b):
    return jnp.dot(a, b, preferred_element_type=F32)


def _dot_nt(a, b):
    return lax.dot_general(a, b, (((1,), (1,)), ((), ())), preferred_element_type=F32)


def _lane_iota(shape):
    return lax.broadcasted_iota(jnp.int32, shape, len(shape) - 1)


def _rope(x, cos, sin_a, sin_b, half):
    return x * cos + pltpu.roll(x, half, 1) * sin_a + pltpu.roll(x, LANES - half, 1) * sin_b


def _ffn_kernel(x_ref, g_ref, wg_ref, wu_ref, wd_ref, o_ref, h_ref, acc_ref, *, tf):
    x = x_ref[...]
    h_ref[...] = _rms(x, g_ref[...]).astype(BF16)
    for j in range(wg_ref.shape[1] // tf):
        cols = slice(j * tf, (j + 1) * tf)
        h = h_ref[...]
        g = _dot(h, wg_ref[:, cols])
        u = _dot(h, wu_ref[:, cols])
        a = (g * jax.nn.sigmoid(g) * u).astype(BF16)
        down = _dot(a, wd_ref[cols, :])
        if j == 0:
            acc_ref[...] = down
        else:
            acc_ref[...] += down
    o_ref[...] = x + 0.5 * acc_ref[...]


def _ffn(x, gain, w_gate, w_up, w_down, layer, half, *, tm=FFN_ROWS, tf=V7X_MXU_DIM):
    t, d = x.shape
    d_ff = w_gate.shape[-1]
    pick = lambda r, c: pl.BlockSpec((None, None, r, c), lambda i: (layer, half, 0, 0),
                                     pipeline_mode=pl.Buffered(1))
    return pl.pallas_call(
        functools.partial(_ffn_kernel, tf=tf),
        grid=(t // tm,),
        in_specs=[
            pl.BlockSpec((tm, d), lambda i: (i, 0)),
            _resident((1, d)),
            pick(d, d_ff), pick(d, d_ff), pick(d_ff, d),
        ],
        out_specs=pl.BlockSpec((tm, d), lambda i: (i, 0)),
        out_shape=jax.ShapeDtypeStruct((t, d), F32),
        scratch_shapes=[pltpu.VMEM((tm, d), BF16), pltpu.VMEM((tm, d), F32)],
        compiler_params=_params(1, VMEM_MB["ffn"]),
        name="ffn",
    )(x, gain.reshape(1, d), w_gate, w_up, w_down)


def _ab_proj_kernel(x_ref, g_ref, w_ref, qn_ref, kn_ref, cos_ref, sa_ref, sb_ref, o_ref, *,
                    n_qk_groups, q_scales):
    h = _rms(x_ref[...], g_ref[...]).astype(BF16)
    cos, sa, sb = cos_ref[...], sa_ref[...], sb_ref[...]
    n_groups = w_ref.shape[1] // LANES
    lane = _lane_iota((x_ref.shape[0], LANES))
    low = lane < DIFF_HEAD_DIM
    for c in range(0, n_groups, 2):
        y2 = _dot(h, w_ref[:, c * LANES:(c + 2) * LANES])
        for k in range(2):
            g = c + k
            y = y2[:, k * LANES:(k + 1) * LANES]
            if g < 2 * n_qk_groups:
                gain = qn_ref[...] if g < n_qk_groups else kn_ref[...]
                sq = y * y
                ss_lo = jnp.sum(jnp.where(low, sq, 0.0), axis=-1, keepdims=True)
                ss_hi = jnp.sum(jnp.where(low, 0.0, sq), axis=-1, keepdims=True)
                ms = jnp.where(low, ss_lo, ss_hi) * (1.0 / DIFF_HEAD_DIM)
                y = y * lax.rsqrt(ms + NORM_EPS) * gain
                y = _rope(y, cos, sa, sb, DIFF_HEAD_DIM // 2)
            if g in q_scales:
                y = y * q_scales[g]
            o_ref[:, g * LANES:(g + 1) * LANES] = y.astype(BF16)


def _ab_proj(x, gain, w_in, q_norm, k_norm, tables, seq, *, tm=PROJ_ROWS):
    t, d = x.shape
    n = w_in.shape[1]
    n_qk_groups = 2 * DIFF_HEADS * DIFF_HEAD_DIM // LANES
    sb_q0 = (4 * DIFF_HEADS * DIFF_HEAD_DIM + DIFF_HEADS * 2 * DIFF_HEAD_DIM) // LANES
    q_scales = {g: DIFF_HEAD_DIM ** -0.5 for g in range(n_qk_groups)}
    q_scales.update({g: SB_HEAD_DIM ** -0.5 * LOG2_E
                     for g in range(sb_q0, sb_q0 + SB_HEADS * SB_HEAD_DIM // LANES)})
    cos, sa, sb = tables
    n_seq_tiles = seq // tm
    tab_spec = pl.BlockSpec((tm, LANES), lambda i: (i % n_seq_tiles, 0))
    kern = functools.partial(_ab_proj_kernel, n_qk_groups=n_qk_groups,
                             q_scales=q_scales)
    return pl.pallas_call(
        kern,
        grid=(t // tm,),
        in_specs=[
            pl.BlockSpec((tm, d), lambda i: (i, 0)),
            _resident((1, d)),
            _resident((d, n)),
            _resident((1, LANES)),
            _resident((1, LANES)),
            tab_spec, tab_spec, tab_spec,
        ],
        out_specs=pl.BlockSpec((tm, n), lambda i: (i, 0)),
        out_shape=jax.ShapeDtypeStruct((t, n), BF16),
        compiler_params=_params(1, VMEM_MB["ab_proj"]),
        name="ab_proj",
    )(x, gain.reshape(1, d), w_in.astype(BF16),
      jnp.tile(q_norm, 2).reshape(1, LANES), jnp.tile(k_norm, 2).reshape(1, LANES),
      cos, sa, sb)


def _chunk_causal_mask(tq, tk):
    qc = lax.broadcasted_iota(jnp.int32, (tq, tk), 0) // CHUNK
    kc = lax.broadcasted_iota(jnp.int32, (tq, tk), 1) // CHUNK
    return kc <= qc


def _fold_lanes(p):
    out = p[:, :LANES]
    for c in range(1, p.shape[1] // LANES):
        out = out + p[:, c * LANES:(c + 1) * LANES]
    return out


def _attend_pair(bound, load_q, k_srcs, v_ref, store_out, acc_refs, l_refs, m_refs, n_q, tq):
    def key_block(h, rows):
        ref, lanes = k_srcs[h]
        return ref[0, rows, lanes]

    @pl.when(bound <= MAX_SOFTMAX_SHIFT)
    def _():
        hq = tq // 2
        lower_rows = lambda a, b: jnp.concatenate([a[:hq], a[hq:] + b], axis=0)
        for qi in range(n_q):
            q_rows = slice(qi * tq, (qi + 1) * tq)
            qs = load_q(q_rows)
            acc, lsum = [None, None], [None, None]
            for kb in range(qi):
                rows = slice(kb * tq, (kb + 1) * tq)
                v = v_ref[0, rows, :]
                for h in range(2):
                    p = jnp.exp(_dot_nt(qs[h], key_block(h, rows)) - bound)
                    part, pv = _fold_lanes(p), _dot(p.astype(BF16), v)
                    lsum[h] = part if kb == 0 else lsum[h] + part
                    acc[h] = pv if kb == 0 else acc[h] + pv
            rows_a = slice(qi * tq, qi * tq + hq)
            rows_b = slice(qi * tq + hq, (qi + 1) * tq)
            v_a, v_b = v_ref[0, rows_a, :], v_ref[0, rows_b, :]
            for h in range(2):
                p_a = jnp.exp(_dot_nt(qs[h], key_block(h, rows_a)) - bound)
                p_a = jnp.where(_chunk_causal_mask(tq, hq), p_a, 0.0)
                p_b = jnp.exp(_dot_nt(qs[h][hq:], key_block(h, rows_b)) - bound)
                p_b = jnp.where(_chunk_causal_mask(hq, hq), p_b, 0.0)
                part = lower_rows(_fold_lanes(p_a), _fold_lanes(p_b))
                pv = lower_rows(_dot(p_a.astype(BF16), v_a), _dot(p_b.astype(BF16), v_b))
                lsum[h] = part if qi == 0 else lsum[h] + part
                acc[h] = pv if qi == 0 else acc[h] + pv
            store_out(q_rows, *(acc[h] / jnp.sum(lsum[h], axis=-1, keepdims=True)
                                for h in range(2)))

    @pl.when(bound > MAX_SOFTMAX_SHIFT)
    def _():
        def q_tile(qi, carry):
            q_rows = pl.ds(pl.multiple_of(qi * tq, tq), tq)
            qs = load_q(q_rows)
            for h in range(2):
                acc_refs[h][...] = jnp.zeros_like(acc_refs[h])
                l_refs[h][...] = jnp.zeros_like(l_refs[h])
                m_refs[h][...] = jnp.full_like(m_refs[h], -jnp.inf)

            def block(kb, masked):
                rows = pl.ds(pl.multiple_of(kb * tq, tq), tq)
                v = v_ref[0, rows, :]
                for h in range(2):
                    s = _dot_nt(qs[h], key_block(h, rows))
                    if masked:
                        s = jnp.where(_chunk_causal_mask(tq, tq), s, -jnp.inf)
                    m_old = m_refs[h][...]
                    m_new = jnp.maximum(m_old, jnp.max(s, axis=-1, keepdims=True))
                    alpha = jnp.exp(m_old - m_new)
                    p = jnp.exp(s - m_new)
                    l_refs[h][...] = alpha * l_refs[h][...] + jnp.sum(p, axis=-1, keepdims=True)
                    acc_refs[h][...] = alpha * acc_refs[h][...] + _dot(p.astype(BF16), v)
                    m_refs[h][...] = m_new

            def body(kb, c):
                block(kb, False)
                return c

            lax.fori_loop(0, qi, body, 0)
            block(qi, True)
            store_out(q_rows, *(acc_refs[h][...] / l_refs[h][...] for h in range(2)))
            return carry

        lax.fori_loop(0, n_q, q_tile, 0)


def _pair_scratch(tq):
    acc = pltpu.VMEM((tq, LANES), F32)
    stat = pltpu.VMEM((tq, 1), F32)
    return [acc, acc, stat, stat, stat, stat]


def _score_bound(q_gain, k_gain, dim):
    return (jnp.max(jnp.abs(q_gain)) * jnp.max(jnp.abs(k_gain)) * math.sqrt(dim)).reshape(1)


def _diff_attn_kernel(bound_ref, q1_ref, q2_ref, k1_ref, k2_ref, v_ref, lam_ref, sub_ref, o_ref,
                      a1_ref, a2_ref, l1_ref, l2_ref, m1_ref, m2_ref, *, tq, lambda_init):
    j = pl.program_id(1)
    mine = (_lane_iota((tq, LANES)) >= DIFF_HEAD_DIM).astype(jnp.int32) == j % 2
    lam_v = lam_ref[...]
    lam = (jnp.exp(jnp.sum(lam_v[0:1] * lam_v[1:2], axis=-1, keepdims=True))
           - jnp.exp(jnp.sum(lam_v[2:3] * lam_v[3:4], axis=-1, keepdims=True)) + lambda_init)

    def load_q(rows):
        return tuple(jnp.where(mine, q_ref[0, rows, :].astype(F32), 0.0).astype(BF16)
                     for q_ref in (q1_ref, q2_ref))

    def store_out(rows, o1, o2):
        o = _rms(o1 - lam * o2, sub_ref[...]) * (1.0 - lambda_init)
        o_ref[0, rows, :] = o.astype(BF16)

    whole = slice(None)
    _attend_pair(bound_ref[0], load_q, ((k1_ref, whole), (k2_ref, whole)), v_ref, store_out,
                 (a1_ref, a2_ref), (l1_ref, l2_ref), (m1_ref, m2_ref), q1_ref.shape[1] // tq, tq)


def _diff_attn(proj, bound, lam_rows, subln, lambda_init, *, tq=ATTN_TILE):
    b, s, _ = proj.shape
    k_off = 2 * DIFF_HEADS * DIFF_HEAD_DIM // LANES
    v_off = 2 * k_off
    half = DIFF_HEADS // 2
    kern = functools.partial(_diff_attn_kernel, tq=tq, lambda_init=lambda_init)
    cols = lambda f: pl.BlockSpec((1, s, LANES), lambda bi, j: (bi, 0, f(j)))
    return pl.pallas_call(
        kern,
        grid=(b, DIFF_HEADS),
        in_specs=[
            pl.BlockSpec(memory_space=pltpu.SMEM),
            cols(lambda j: j // 2),
            cols(lambda j: half + j // 2),
            cols(lambda j: k_off + j // 2),
            cols(lambda j: k_off + half + j // 2),
            cols(lambda j: v_off + j),
            _resident((4, DIFF_HEAD_DIM)),
            _resident((1, LANES)),
        ],
        out_specs=cols(lambda j: j),
        out_shape=jax.ShapeDtypeStruct((b, s, DIFF_HEADS * LANES), BF16),
        scratch_shapes=_pair_scratch(tq),
        compiler_params=_params(2, VMEM_MB["diff_attn"]),
        name="diff_attn",
    )(bound, proj, proj, proj, proj, proj, lam_rows, subln.reshape(1, LANES))


def _split2(x):
    hi = x.astype(BF16)
    return hi, (x - hi.astype(F32)).astype(BF16)


def _sb_log_terms(z):
    sign_bit = jnp.uint32(0x80000000)
    neg_abs = lax.bitcast_convert_type(lax.bitcast_convert_type(z, jnp.uint32) | sign_bit, F32)
    log_beta = jnp.minimum(z, 0.0) - jnp.log2(1.0 + jnp.exp2(neg_abs))
    return log_beta, log_beta - z


def _sb_block(q, k, v, later_than, c_in):
    log_beta, log_keep = _sb_log_terms(_dot_nt(q, k))
    hi, mid = _split2(log_keep)
    later = _dot(hi, later_than) + _dot(mid, later_than)
    w = jnp.exp2(log_beta + later + c_in)
    return _dot(w.astype(BF16), v), jnp.sum(log_keep, axis=-1, keepdims=True)


def _sb_attn_kernel(q_ref, k_ref, v_ref, o_ref, c_ref, acc_ref, *, tq, n_groups):
    qi = pl.program_id(2)
    low = _lane_iota((tq, LANES)) < SB_HEAD_DIM
    q_heads = []
    for g in range(n_groups):
        q = q_ref[0, :, g * LANES:(g + 1) * LANES].astype(F32)
        q_heads += [jnp.where(low, q, 0.0).astype(BF16), jnp.where(low, 0.0, q).astype(BF16)]
    row = lax.broadcasted_iota(jnp.int32, (tq, tq), 0)
    col = lax.broadcasted_iota(jnp.int32, (tq, tq), 1)
    strict = col < row
    later_than = (row > col).astype(BF16)

    def key_rows(kb):
        return pl.ds(pl.multiple_of(kb * tq, tq), tq)

    def group_block(rows, g):
        lanes = slice(g * LANES, (g + 1) * LANES)
        return k_ref[0, rows, lanes], v_ref[0, rows, lanes]

    has_prev = qi > 0
    rows_a, rows_b = key_rows(qi), key_rows(jnp.maximum(qi - 1, 0))
    for g in range(n_groups):
        (k_a, v_a), (k_b, v_b) = group_block(rows_a, g), group_block(rows_b, g)
        k_win = jnp.concatenate([k_b, k_a], axis=0)
        v_win = jnp.concatenate([jnp.where(has_prev, v_b, jnp.zeros_like(v_b)), v_a], axis=0)
        heads = (2 * g, 2 * g + 1)
        log_beta, pieces, sums = [], [], []
        for h in heads:
            lb, lk = _sb_log_terms(_dot_nt(q_heads[h], k_win))
            lk = jnp.concatenate([lk[:, :tq], jnp.where(strict, lk[:, tq:], 0.0)], axis=1)
            hi, mid = _split2(lk)
            pieces += [hi[:, :tq], hi[:, tq:], mid[:, :tq], mid[:, tq:]]
            sum_a = jnp.sum(lk[:, tq:], axis=-1, keepdims=True)
            sum_b = jnp.where(has_prev, jnp.sum(lk[:, :tq], axis=-1, keepdims=True), 0.0)
            log_beta.append(lb)
            sums.append((sum_a, sum_b))
        later = _dot(jnp.concatenate(pieces, axis=0), later_than)
        outs = []
        for i, h in enumerate(heads):
            blk = lambda j: later[(4 * i + j) * tq:(4 * i + j + 1) * tq]
            sum_a, sum_b = sums[i]
            w_b = jnp.exp2(log_beta[i][:, :tq] + (blk(0) + blk(2)) + sum_a)
            w_a = jnp.where(strict, jnp.exp2(log_beta[i][:, tq:] + (blk(1) + blk(3))), 0.0)
            outs.append(_dot(jnp.concatenate([w_b, w_a], axis=1).astype(BF16), v_win))
            c_ref[:, h:h + 1] = sum_a + sum_b
        acc_ref[:, g * LANES:(g + 1) * LANES] = jnp.where(low, outs[0], outs[1])

    def live():
        return jnp.max(c_ref[...]) > SB_LOG2_ZERO

    def cond(state):
        kb, go = state
        return jnp.logical_and(kb >= 0, go)

    def body(state):
        kb, _ = state
        rows = key_rows(kb)
        for g in range(n_groups):
            k, v = group_block(rows, g)
            outs = []
            for h in (2 * g, 2 * g + 1):
                c_old = c_ref[:, h:h + 1]
                pv, row_sum = _sb_block(q_heads[h], k, v, later_than, c_old)
                outs.append(pv)
                c_ref[:, h:h + 1] = c_old + row_sum
            acc_ref[:, g * LANES:(g + 1) * LANES] += jnp.where(low, outs[0], outs[1])
        return kb - 1, live()

    lax.while_loop(cond, body, (qi - 2, live()))
    o_ref[0] = acc_ref[...].astype(BF16)


def _sb_attn(proj, *, tq=SB_TILE, n_groups=2):
    b, s, _ = proj.shape
    width = n_groups * LANES
    q_off = (4 * DIFF_HEADS * DIFF_HEAD_DIM + DIFF_HEADS * 2 * DIFF_HEAD_DIM) // width
    n_steps = SB_HEADS * SB_HEAD_DIM // width
    kern = functools.partial(_sb_attn_kernel, tq=tq, n_groups=n_groups)
    return pl.pallas_call(
        kern,
        grid=(b, n_steps, s // tq),
        in_specs=[
            pl.BlockSpec((1, tq, width), lambda bi, p, qi: (bi, qi, q_off + p)),
            pl.BlockSpec((1, s, width), lambda bi, p, qi: (bi, 0, q_off + n_steps + p)),
            pl.BlockSpec((1, s, width), lambda bi, p, qi: (bi, 0, q_off + 2 * n_steps + p)),
        ],
        out_specs=pl.BlockSpec((1, tq, width), lambda bi, p, qi: (bi, qi, p)),
        out_shape=jax.ShapeDtypeStruct((b, s, n_steps * width), BF16),
        scratch_shapes=[pltpu.VMEM((tq, 2 * n_groups), F32), pltpu.VMEM((tq, width), F32)],
        compiler_params=_params(3, VMEM_MB["sb_attn"]),
        name="sb_attn",
    )(proj, proj, proj)


def _mla_proj_kernel(x_ref, g_ref, wdq_ref, qln_ref, wuq_ref, wdkv_ref, kvln_ref,
                     wuk_ref, wuv_ref, qsum_ref, ksum_ref, gkn_ref, qc_ref, qs_ref,
                     kc_ref, ksa_ref, ksb_ref, q_ref, k_ref, v_ref):
    tm = x_ref.shape[0]
    h = _rms(x_ref[...], g_ref[...]).astype(BF16)
    inv_dim = 1.0 / MLA_QK_DIM
    c_q = _rms(_dot(h, wdq_ref[...]), qln_ref[...]).astype(BF16)
    dkv = _dot(h, wdkv_ref[...])
    c_kv = _rms(dkv[:, :MLA_KV_RANK], kvln_ref[...]).astype(BF16)
    kr = dkv[:, MLA_KV_RANK:]
    kr_sq = 0.5 * jnp.sum(kr * kr, axis=-1, keepdims=True)
    half = MLA_ROPE_DIM // 2
    kr_rot = (kr * kc_ref[...] + pltpu.roll(kr, half, 1) * ksa_ref[...]
              + pltpu.roll(kr, LANES - half, 1) * ksb_ref[...])
    v_ref[...] = _dot(c_kv, wuv_ref[...]).astype(BF16)
    low = _lane_iota((tm, LANES)) < MLA_NOPE_DIM
    qc, qs, gkn = qc_ref[...], qs_ref[...], gkn_ref[...]
    for p2 in range(0, MLA_HEADS // 2, 2):
        kn4 = _dot(c_kv, wuk_ref[:, p2 * LANES:(p2 + 2) * LANES])
        for i in range(2):
            p = p2 + i
            cols = slice(p * 2 * LANES, (p + 1) * 2 * LANES)
            q4 = _dot(c_q, wuq_ref[:, 2 * p * 2 * LANES:2 * (p + 1) * 2 * LANES])
            q2, q2s = q4[:, :2 * LANES], q4[:, 2 * LANES:]
            q_ms = _dot((q2 * q2).astype(BF16), qsum_ref[...]) * inv_dim
            q_ref[:, cols] = ((q2 * qc + q2s * qs) * lax.rsqrt(q_ms + NORM_EPS)).astype(BF16)
            kn2 = kn4[:, i * LANES:(i + 1) * LANES]
            k_ms = (_dot((kn2 * kn2).astype(BF16), ksum_ref[...]) + kr_sq) * inv_dim
            rk = lax.rsqrt(k_ms + NORM_EPS)
            kg = kn2 * gkn
            k_ref[:, p * 2 * LANES:p * 2 * LANES + LANES] = (
                jnp.where(low, kg, kr_rot) * rk[:, :LANES]).astype(BF16)
            k_ref[:, p * 2 * LANES + LANES:(p + 1) * 2 * LANES] = (
                jnp.where(low, kr_rot, kg) * rk[:, LANES:]).astype(BF16)


def _mla_pair_layout(even, odd_first, odd_last, n_rows):
    z = lambda w: jnp.zeros((n_rows, w), F32)
    return jnp.concatenate([even, z(LANES - MLA_QK_DIM), odd_first, z(MLA_ROPE_DIM), odd_last], axis=1)


def _mla_proj(x, gain, w_dq, q_lat_norm, w_uq, w_dkv, kv_lat_norm, w_ukv, qk_norm_q, qk_norm_k,
              seq, *, tm=PROJ_ROWS):
    t, d = x.shape
    q_rank = w_dq.shape[1]
    half = MLA_ROPE_DIM // 2
    n_pairs = MLA_HEADS // 2
    swap = lambda a: jnp.concatenate([a[..., half:], a[..., :half]], axis=-1)

    uq = w_uq.reshape(q_rank, MLA_HEADS, MLA_QK_DIM)
    nope, rope = uq[:, :, :MLA_NOPE_DIM], uq[:, :, MLA_NOPE_DIM:]
    rope_sw = jnp.concatenate([-rope[..., half:], rope[..., :half]], axis=-1)
    zeros = jnp.zeros_like
    ev_n, ev_r, od_r, od_n = nope[:, 0::2], rope[:, 0::2], rope[:, 1::2], nope[:, 1::2]
    wuq = jnp.concatenate(
        [ev_n, ev_r, zeros(ev_r), od_r, zeros(od_r), od_n,
         zeros(ev_n), rope_sw[:, 0::2], zeros(ev_r), rope_sw[:, 1::2], zeros(od_r), zeros(od_n)],
        axis=-1).reshape(q_rank, -1).astype(BF16)

    ukv = w_ukv.reshape(MLA_KV_RANK, MLA_HEADS, MLA_NOPE_DIM + MLA_V_DIM)
    wuk = ukv[:, :, :MLA_NOPE_DIM].reshape(MLA_KV_RANK, -1).astype(BF16)
    wuv = ukv[:, :, MLA_NOPE_DIM:].reshape(MLA_KV_RANK, -1).astype(BF16)
    kr_w = w_dkv[:, MLA_KV_RANK:]
    kr_pad = jnp.zeros((d, MLA_ROPE_DIM), F32)
    wdkv = jnp.concatenate([w_dkv[:, :MLA_KV_RANK], kr_w, kr_pad, kr_w, kr_pad], axis=1).astype(BF16)

    r256 = jnp.arange(2 * LANES)
    qsum = (r256[:, None] // LANES == r256[None, :] // LANES).astype(BF16)
    r128 = jnp.arange(LANES)
    ksum = (r128[:, None] // MLA_NOPE_DIM == r256[None, :] // LANES).astype(BF16)

    inv = 1.0 / (ROPE_THETA ** (jnp.arange(0, MLA_ROPE_DIM, 2, dtype=F32) / MLA_ROPE_DIM))
    ang = jnp.arange(seq, dtype=F32)[:, None] * inv[None, :]
    cos, sin = jnp.tile(jnp.cos(ang), (1, 2)), jnp.tile(jnp.sin(ang), (1, 2))
    scale = MLA_QK_DIM ** -0.5
    gq_n, gq_r = qk_norm_q[:MLA_NOPE_DIM] * scale, qk_norm_q[MLA_NOPE_DIM:] * scale
    gk_n, gk_r = qk_norm_k[:MLA_NOPE_DIM], qk_norm_k[MLA_NOPE_DIM:]
    bq_n = jnp.broadcast_to(gq_n, (seq, MLA_NOPE_DIM))
    qc = _mla_pair_layout(jnp.concatenate([bq_n, cos * gq_r], axis=1), cos * gq_r, bq_n, seq)
    qs = _mla_pair_layout(jnp.concatenate([zeros(bq_n), sin * swap(gq_r)], axis=1),
                          sin * swap(gq_r), zeros(bq_n), seq)
    first = jnp.arange(MLA_ROPE_DIM) < half
    k_cos = cos * gk_r
    k_sa = jnp.where(first, 0.0, sin * swap(gk_r))
    k_sb = jnp.where(first, -sin * swap(gk_r), 0.0)
    twice = lambda a: jnp.concatenate([a, zeros(a), a, zeros(a)], axis=1)
    n_seq_tiles = seq // tm
    tab = lambda w: pl.BlockSpec((tm, w), lambda i: (i % n_seq_tiles, 0))
    hw = MLA_HEADS * LANES
    vw = MLA_HEADS * MLA_V_DIM
    row = lambda w: pl.BlockSpec((tm, w), lambda i: (i, 0))
    return pl.pallas_call(
        _mla_proj_kernel,
        grid=(t // tm,),
        in_specs=[
            row(d), _resident((1, d)),
            _resident((d, q_rank)), _resident((1, q_rank)), _resident((q_rank, 2 * hw)),
            _resident((d, MLA_KV_RANK + LANES)), _resident((1, MLA_KV_RANK)),
            _resident((MLA_KV_RANK, n_pairs * LANES)), _resident((MLA_KV_RANK, vw)),
            _resident((2 * LANES, 2 * LANES)), _resident((LANES, 2 * LANES)), _resident((1, LANES)),
            tab(2 * LANES), tab(2 * LANES), tab(LANES), tab(LANES), tab(LANES),
        ],
        out_specs=[row(hw), row(hw), row(vw)],
        out_shape=[jax.ShapeDtypeStruct((t, hw), BF16), jax.ShapeDtypeStruct((t, hw), BF16),
                   jax.ShapeDtypeStruct((t, vw), BF16)],
        compiler_params=_params(1, VMEM_MB["mla_proj"]),
        name="mla_proj",
    )(x, gain.reshape(1, d), w_dq.astype(BF16), q_lat_norm.reshape(1, q_rank), wuq, wdkv,
      kv_lat_norm.reshape(1, MLA_KV_RANK), wuk, wuv, qsum, ksum,
      jnp.tile(gk_n, 2).reshape(1, LANES), qc, qs, twice(k_cos), twice(k_sa), twice(k_sb))


def _mla_attn_kernel(bound_ref, q_ref, k_ref, v_ref, o_ref,
                     a0_ref, a1_ref, l0_ref, l1_ref, m0_ref, m1_ref, *, tq):
    low = _lane_iota((tq, LANES)) < MLA_V_DIM
    lanes = (slice(0, LANES), slice(LANES, 2 * LANES))

    def load_q(rows):
        return tuple(q_ref[0, rows, ln] for ln in lanes)

    def store_out(rows, o0, o1):
        o_ref[0, rows, :] = jnp.where(low, o0, o1).astype(BF16)

    _attend_pair(bound_ref[0], load_q, ((k_ref, lanes[0]), (k_ref, lanes[1])), v_ref, store_out,
                 (a0_ref, a1_ref), (l0_ref, l1_ref), (m0_ref, m1_ref), q_ref.shape[1] // tq, tq)


def _mla_attn(q, k, v, bound, *, tq=ATTN_TILE):
    b, s, _ = q.shape
    n_pairs = MLA_HEADS // 2
    kern = functools.partial(_mla_attn_kernel, tq=tq)
    return pl.pallas_call(
        kern,
        grid=(b, n_pairs),
        in_specs=[
            pl.BlockSpec(memory_space=pltpu.SMEM),
            pl.BlockSpec((1, s, 2 * LANES), lambda bi, p: (bi, 0, p)),
            pl.BlockSpec((1, s, 2 * LANES), lambda bi, p: (bi, 0, p)),
            pl.BlockSpec((1, s, LANES), lambda bi, p: (bi, 0, p)),
        ],
        out_specs=pl.BlockSpec((1, s, LANES), lambda bi, p: (bi, 0, p)),
        out_shape=jax.ShapeDtypeStruct((b, s, n_pairs * LANES), BF16),
        scratch_shapes=_pair_scratch(tq),
        compiler_params=_params(2, VMEM_MB["mla_attn"]),
        name="mla_attn",
    )(bound, q, k, v)


def _mem_kv_kernel(m_ref, g_ref, w_ref, kn_ref, k_ref, v_ref):
    h = _rms(m_ref[0], g_ref[...]).astype(BF16)
    kv = _dot(h, w_ref[...])
    width = XM_HEADS * XM_HEAD_DIM
    for hd in range(XM_HEADS):
        lanes = slice(hd * LANES, (hd + 1) * LANES)
        k_ref[0, :, lanes] = _rms(kv[:, lanes], kn_ref[...]).astype(BF16)
    v_ref[0] = kv[:, width:].astype(BF16)


def _mem_kv(mem, gain, w_kv, k_norm):
    b, m, d = mem.shape
    width = XM_HEADS * XM_HEAD_DIM
    out = jax.ShapeDtypeStruct((b, m, width), BF16)
    blk = pl.BlockSpec((1, m, width), lambda i: (i, 0, 0))
    return pl.pallas_call(
        _mem_kv_kernel,
        grid=(b,),
        in_specs=[pl.BlockSpec((1, m, d), lambda i: (i, 0, 0)), _resident((1, d)),
                  _resident((d, 2 * width)), _resident((1, XM_HEAD_DIM))],
        out_specs=[blk, blk],
        out_shape=[out, out],
        compiler_params=_params(1, VMEM_MB["mem_kv"]),
        name="mem_kv",
    )(mem, gain.reshape(1, d), w_kv.astype(BF16), k_norm.reshape(1, XM_HEAD_DIM))


def _post_kernel(*refs, n_mix):
    x_ref = refs[0]
    mix_refs = refs[1:1 + n_mix]
    wout_ref, g_ref, wq_ref, qn_ref, k_ref, v_ref, wo_ref, o_ref = refs[1 + n_mix:]
    x = x_ref[...]
    off = 0
    for a_ref in mix_refs:
        w = a_ref.shape[1]
        x = x + _dot(a_ref[...], wout_ref[off:off + w, :])
        off += w
    h = _rms(x, g_ref[...]).astype(BF16)
    q = _dot(h, wq_ref[...])
    heads = []
    for hd in range(XM_HEADS):
        lanes = slice(hd * LANES, (hd + 1) * LANES)
        qh = (_rms(q[:, lanes], qn_ref[...]) * (XM_HEAD_DIM ** -0.5)).astype(BF16)
        s = _dot_nt(qh, k_ref[0, :, lanes])
        p = jnp.exp(s - jnp.max(s, axis=-1, keepdims=True))
        l = jnp.sum(p, axis=-1, keepdims=True)
        heads.append((_dot(p.astype(BF16), v_ref[0, :, lanes]) / l).astype(BF16))
    o = jnp.concatenate(heads, axis=-1)
    o_ref[...] = x + _dot(o, wo_ref[...])


def _post(x, mixes, w_out, gain, w_q, q_norm, mem_k, mem_v, w_o, seq, *, tm=PROJ_ROWS):
    t, d = x.shape
    b, m, width = mem_k.shape
    n_seq_tiles = seq // tm
    row = lambda w: pl.BlockSpec((tm, w), lambda i: (i, 0))
    mem_spec = pl.BlockSpec((1, m, width), lambda i: (i // n_seq_tiles, 0, 0))
    kern = functools.partial(_post_kernel, n_mix=len(mixes))
    return pl.pallas_call(
        kern,
        grid=(t // tm,),
        in_specs=[row(d)] + [row(a.shape[1]) for a in mixes] + [
            _resident(w_out.shape), _resident((1, d)), _resident((d, width)),
            _resident((1, XM_HEAD_DIM)), mem_spec, mem_spec, _resident((width, d))],
        out_specs=row(d),
        out_shape=jax.ShapeDtypeStruct((t, d), F32),
        compiler_params=_params(1, VMEM_MB["post"]),
        name="post",
    )(x, *mixes, w_out.astype(BF16), gain.reshape(1, d), w_q.astype(BF16),
      q_norm.reshape(1, XM_HEAD_DIM), mem_k, mem_v, w_o.astype(BF16))


def _rope_tables(seq, dim, start, group):
    half = dim // 2
    inv = 1.0 / (ROPE_THETA ** (jnp.arange(0, dim, 2, dtype=F32) / dim))
    ang = jnp.arange(seq, dtype=F32)[:, None] * inv[None, :]
    cos, sin = jnp.cos(ang), jnp.sin(ang)
    zeros = jnp.zeros_like(sin)
    pad = lambda a, fill: jnp.concatenate(
        [jnp.full((seq, start), fill, F32), a, jnp.full((seq, group - start - dim), fill, F32)],
        axis=1)
    cos_t = pad(jnp.concatenate([cos, cos], axis=1), 1.0)
    sin_a = pad(jnp.concatenate([zeros, sin], axis=1), 0.0)
    sin_b = pad(jnp.concatenate([-sin, zeros], axis=1), 0.0)
    reps = LANES // group
    return tuple(jnp.tile(a, (1, reps)) for a in (cos_t, sin_a, sin_b))


def kernel(x, mem, ffn_norm, ffn_w_gate, ffn_w_up, ffn_w_down, mix_norm, ab_w_in, ab_w_out, diff_q_norm, diff_k_norm, diff_lambda_q1, diff_lambda_k1, diff_lambda_q2, diff_lambda_k2, diff_subln, mla_w_dq, mla_q_norm, mla_w_uq, mla_w_dkv, mla_kv_norm, mla_w_ukv, mla_qk_norm_q, mla_qk_norm_k, mla_w_o, xm_norm, xm_mem_norm, xm_w_q, xm_w_kv, xm_q_norm, xm_k_norm, xm_w_o):
    b, s, d = x.shape
    depth = ffn_norm.shape[0]
    x = x.reshape(b * s, d)
    diff_tables = _rope_tables(s, DIFF_HEAD_DIM, 0, DIFF_HEAD_DIM)
    ffn_w = tuple(w.astype(BF16) for w in (ffn_w_gate, ffn_w_up, ffn_w_down))
    for layer in range(depth):
        i = layer // 2
        x = _ffn(x, ffn_norm[layer, 0], *ffn_w, layer, 0)
        if layer % 2 == 0:
            lambda_init = 0.8 - 0.6 * math.exp(-0.3 * layer)
            proj = _ab_proj(x, mix_norm[layer], ab_w_in[i], diff_q_norm[i], diff_k_norm[i],
                            diff_tables, s).reshape(b, s, -1)
            lam_rows = jnp.stack([diff_lambda_q1[i], diff_lambda_k1[i],
                                  diff_lambda_q2[i], diff_lambda_k2[i]])
            bound = _score_bound(diff_q_norm[i], diff_k_norm[i], DIFF_HEAD_DIM)
            oa = _diff_attn(proj, bound, lam_rows, diff_subln[i], lambda_init)
            ob = _sb_attn(proj)
            mixes = [oa.reshape(b * s, -1), ob.reshape(b * s, -1)]
            w_out = ab_w_out[i]
        else:
            q, k, v = _mla_proj(x, mix_norm[layer], mla_w_dq[i], mla_q_norm[i], mla_w_uq[i],
                                mla_w_dkv[i], mla_kv_norm[i], mla_w_ukv[i], mla_qk_norm_q[i],
                                mla_qk_norm_k[i], s)
            bound = _score_bound(mla_qk_norm_q[i], mla_qk_norm_k[i], MLA_QK_DIM)
            o = _mla_attn(q.reshape(b, s, -1), k.reshape(b, s, -1), v.reshape(b, s, -1), bound)
            mixes = [o.reshape(b * s, -1)]
            w_out = mla_w_o[i]
        mem_k, mem_v = _mem_kv(mem, xm_mem_norm[layer], xm_w_kv[layer], xm_k_norm[layer])
        x = _post(x, mixes, w_out, xm_norm[layer], xm_w_q[layer], xm_q_norm[layer],
                  mem_k, mem_v, xm_w_o[layer], s)
        x = _ffn(x, ffn_norm[layer, 1], *ffn_w, layer, 1)
    return x.reshape(b, s, d)
```

```python
import functools
import math

import jax
import jax.numpy as jnp
from jax import lax
from jax.experimental import pallas as pl
from jax.experimental.pallas import tpu as pltpu

F32 = jnp.float32
BF16 = jnp.bfloat16

LANES = 128
V7X_MXU_DIM = 256
V7X_VMEM_LIMIT = 56 * 1024 * 1024

FFN_ROWS = 1024
PROJ_ROWS = 1024
ATTN_TILE = 512
SB_TILE = 256
VMEM_MB = dict(ffn=56, ab_proj=56, diff_attn=48, sb_attn=32, mla_proj=56, mla_attn=48,
               mem_kv=32, post=56)

CHUNK = 64
ROPE_THETA = 10000.0
NORM_EPS = 1e-6
SB_LOG2_ZERO = -151.0
LOG2_E = math.log2(math.e)
MAX_SOFTMAX_SHIFT = 40.0

DIFF_HEADS = 4
DIFF_HEAD_DIM = 64
SB_HEADS = 8
SB_HEAD_DIM = 64
MLA_HEADS = 16
MLA_KV_RANK = 256
MLA_NOPE_DIM = 64
MLA_ROPE_DIM = 32
MLA_V_DIM = 64
MLA_QK_DIM = MLA_NOPE_DIM + MLA_ROPE_DIM
XM_HEADS = 4
XM_HEAD_DIM = 128


def _params(n_grid, vmem_mb):
    return pltpu.CompilerParams(
        dimension_semantics=("arbitrary",) * n_grid,
        vmem_limit_bytes=min(vmem_mb * 1024 * 1024, V7X_VMEM_LIMIT))


def _resident(shape):
    nd = len(shape)
    return pl.BlockSpec(shape, lambda *_: (0,) * nd, pipeline_mode=pl.Buffered(1))


def _rms(x, gain):
    return x * lax.rsqrt(jnp.mean(x * x, axis=-1, keepdims=True) + NORM_EPS) * gain


def _dot(a, b):
    return jnp.dot(a, b, preferred_element_type=F32)


def _dot_nt(a, b):
    return lax.dot_general(a, b, (((1,), (1,)), ((), ())), preferred_element_type=F32)


def _lane_iota(shape):
    return lax.broadcasted_iota(jnp.int32, shape, len(shape) - 1)


def _rope(x, cos, sin_a, sin_b, half):
    return x * cos + pltpu.roll(x, half, 1) * sin_a + pltpu.roll(x, LANES - half, 1) * sin_b


def _ffn_kernel(x_ref, g_ref, wg_ref, wu_ref, wd_ref, o_ref, h_ref, acc_ref, *, tf):
    x = x_ref[...]
    h_ref[...] = _rms(x, g_ref[...]).astype(BF16)
    for j in range(wg_ref.shape[1] // tf):
        cols = slice(j * tf, (j + 1) * tf)
        h = h_ref[...]
        g = _dot(h, wg_ref[:, cols])
        u = _dot(h, wu_ref[:, cols])
        a = (g * jax.nn.sigmoid(g) * u).astype(BF16)
        down = _dot(a, wd_ref[cols, :])
        if j == 0:
            acc_ref[...] = down
        else:
            acc_ref[...] += down
    o_ref[...] = x + 0.5 * acc_ref[...]


def _ffn(x, gain, w_gate, w_up, w_down, layer, half, *, tm=FFN_ROWS, tf=V7X_MXU_DIM):
    t, d = x.shape
    d_ff = w_gate.shape[-1]
    pick = lambda r, c: pl.BlockSpec((None, None, r, c), lambda i: (layer, half, 0, 0),
                                     pipeline_mode=pl.Buffered(1))
    return pl.pallas_call(
        functools.partial(_ffn_kernel, tf=tf),
        grid=(t // tm,),
        in_specs=[
            pl.BlockSpec((tm, d), lambda i: (i, 0)),
            _resident((1, d)),
            pick(d, d_ff), pick(d, d_ff), pick(d_ff, d),
        ],
        out_specs=pl.BlockSpec((tm, d), lambda i: (i, 0)),
        out_shape=jax.ShapeDtypeStruct((t, d), F32),
        scratch_shapes=[pltpu.VMEM((tm, d), BF16), pltpu.VMEM((tm, d), F32)],
        compiler_params=_params(1, VMEM_MB["ffn"]),
        name="ffn",
    )(x, gain.reshape(1, d), w_gate, w_up, w_down)


def _ab_proj_kernel(x_ref, g_ref, w_ref, qn_ref, kn_ref, cos_ref, sa_ref, sb_ref, o_ref, *,
                    n_qk_groups, q_scales):
    h = _rms(x_ref[...], g_ref[...]).astype(BF16)
    cos, sa, sb = cos_ref[...], sa_ref[...], sb_ref[...]
    n_groups = w_ref.shape[1] // LANES
    lane = _lane_iota((x_ref.shape[0], LANES))
    low = lane < DIFF_HEAD_DIM
    for c in range(0, n_groups, 2):
        y2 = _dot(h, w_ref[:, c * LANES:(c + 2) * LANES])
        for k in range(2):
            g = c + k
            y = y2[:, k * LANES:(k + 1) * LANES]
            if g < 2 * n_qk_groups:
                gain = qn_ref[...] if g < n_qk_groups else kn_ref[...]
                sq = y * y
                ss_lo = jnp.sum(jnp.where(low, sq, 0.0), axis=-1, keepdims=True)
                ss_hi = jnp.sum(jnp.where(low, 0.0, sq), axis=-1, keepdims=True)
                ms = jnp.where(low, ss_lo, ss_hi) * (1.0 / DIFF_HEAD_DIM)
                y = y * lax.rsqrt(ms + NORM_EPS) * gain
                y = _rope(y, cos, sa, sb, DIFF_HEAD_DIM // 2)
            if g in q_scales:
                y = y * q_scales[g]
            o_ref[:, g * LANES:(g + 1) * LANES] = y.astype(BF16)


def _ab_proj(x, gain, w_in, q_norm, k_norm, tables, seq, *, tm=PROJ_ROWS):
    t, d = x.shape
    n = w_in.shape[1]
    n_qk_groups = 2 * DIFF_HEADS * DIFF_HEAD_DIM // LANES
    sb_q0 = (4 * DIFF_HEADS * DIFF_HEAD_DIM + DIFF_HEADS * 2 * DIFF_HEAD_DIM) // LANES
    q_scales = {g: DIFF_HEAD_DIM ** -0.5 for g in range(n_qk_groups)}
    q_scales.update({g: SB_HEAD_DIM ** -0.5 * LOG2_E
                     for g in range(sb_q0, sb_q0 + SB_HEADS * SB_HEAD_DIM // LANES)})
    cos, sa, sb = tables
    n_seq_tiles = seq // tm
    tab_spec = pl.BlockSpec((tm, LANES), lambda i: (i % n_seq_tiles, 0))
    kern = functools.partial(_ab_proj_kernel, n_qk_groups=n_qk_groups,
                             q_scales=q_scales)
    return pl.pallas_call(
        kern,
        grid=(t // tm,),
        in_specs=[
            pl.BlockSpec((tm, d), lambda i: (i, 0)),
            _resident((1, d)),
            _resident((d, n)),
            _resident((1, LANES)),
            _resident((1, LANES)),
            tab_spec, tab_spec, tab_spec,
        ],
        out_specs=pl.BlockSpec((tm, n), lambda i: (i, 0)),
        out_shape=jax.ShapeDtypeStruct((t, n), BF16),
        compiler_params=_params(1, VMEM_MB["ab_proj"]),
        name="ab_proj",
    )(x, gain.reshape(1, d), w_in.astype(BF16),
      jnp.tile(q_norm, 2).reshape(1, LANES), jnp.tile(k_norm, 2).reshape(1, LANES),
      cos, sa, sb)


def _chunk_causal_mask(tq, tk):
    qc = lax.broadcasted_iota(jnp.int32, (tq, tk), 0) // CHUNK
    kc = lax.broadcasted_iota(jnp.int32, (tq, tk), 1) // CHUNK
    return kc <= qc


def _fold_lanes(p):
    out = p[:, :LANES]
    for c in range(1, p.shape[1] // LANES):
        out = out + p[:, c * LANES:(c + 1) * LANES]
    return out


def _attend_pair(bound, load_q, k_srcs, v_ref, store_out, acc_refs, l_refs, m_refs, n_q, tq):
    def key_block(h, rows):
        ref, lanes = k_srcs[h]
        return ref[0, rows, lanes]

    @pl.when(bound <= MAX_SOFTMAX_SHIFT)
    def _():
        hq = tq // 2
        lower_rows = lambda a, b: jnp.concatenate([a[:hq], a[hq:] + b], axis=0)
        for qi in range(n_q):
            q_rows = slice(qi * tq, (qi + 1) * tq)
            qs = load_q(q_rows)
            acc, lsum = [None, None], [None, None]
            for kb in range(qi):
                rows = slice(kb * tq, (kb + 1) * tq)
                v = v_ref[0, rows, :]
                for h in range(2):
                    p = jnp.exp(_dot_nt(qs[h], key_block(h, rows)) - bound)
                    part, pv = _fold_lanes(p), _dot(p.astype(BF16), v)
                    lsum[h] = part if kb == 0 else lsum[h] + part
                    acc[h] = pv if kb == 0 else acc[h] + pv
            rows_a = slice(qi * tq, qi * tq + hq)
            rows_b = slice(qi * tq + hq, (qi + 1) * tq)
            v_a, v_b = v_ref[0, rows_a, :], v_ref[0, rows_b, :]
            for h in range(2):
                p_a = jnp.exp(_dot_nt(qs[h], key_block(h, rows_a)) - bound)
                p_a = jnp.where(_chunk_causal_mask(tq, hq), p_a, 0.0)
                p_b = jnp.exp(_dot_nt(qs[h][hq:], key_block(h, rows_b)) - bound)
                p_b = jnp.where(_chunk_causal_mask(hq, hq), p_b, 0.0)
                part = lower_rows(_fold_lanes(p_a), _fold_lanes(p_b))
                pv = lower_rows(_dot(p_a.astype(BF16), v_a), _dot(p_b.astype(BF16), v_b))
                lsum[h] = part if qi == 0 else lsum[h] + part
                acc[h] = pv if qi == 0 else acc[h] + pv
            store_out(q_rows, *(acc[h] / jnp.sum(lsum[h], axis=-1, keepdims=True)
                                for h in range(2)))

    @pl.when(bound > MAX_SOFTMAX_SHIFT)
    def _():
        def q_tile(qi, carry):
            q_rows = pl.ds(pl.multiple_of(qi * tq, tq), tq)
            qs = load_q(q_rows)
            for h in range(2):
                acc_refs[h][...] = jnp.zeros_like(acc_refs[h])
                l_refs[h][...] = jnp.zeros_like(l_refs[h])
                m_refs[h][...] = jnp.full_like(m_refs[h], -jnp.inf)

            def block(kb, masked):
                rows = pl.ds(pl.multiple_of(kb * tq, tq), tq)
                v = v_ref[0, rows, :]
                for h in range(2):
                    s = _dot_nt(qs[h], key_block(h, rows))
                    if masked:
                        s = jnp.where(_chunk_causal_mask(tq, tq), s, -jnp.inf)
                    m_old = m_refs[h][...]
                    m_new = jnp.maximum(m_old, jnp.max(s, axis=-1, keepdims=True))
                    alpha = jnp.exp(m_old - m_new)
                    p = jnp.exp(s - m_new)
                    l_refs[h][...] = alpha * l_refs[h][...] + jnp.sum(p, axis=-1, keepdims=True)
                    acc_refs[h][...] = alpha * acc_refs[h][...] + _dot(p.astype(BF16), v)
                    m_refs[h][...] = m_new

            def body(kb, c):
                block(kb, False)
                return c

            lax.fori_loop(0, qi, body, 0)
            block(qi, True)
            store_out(q_rows, *(acc_refs[h][...] / l_refs[h][...] for h in range(2)))
            return carry

        lax.fori_loop(0, n_q, q_tile, 0)


def _pair_scratch(tq):
    acc = pltpu.VMEM((tq, LANES), F32)
    stat = pltpu.VMEM((tq, 1), F32)
    return [acc, acc, stat, stat, stat, stat]


def _score_bound(q_gain, k_gain, dim):
    return (jnp.max(jnp.abs(q_gain)) * jnp.max(jnp.abs(k_gain)) * math.sqrt(dim)).reshape(1)


def _diff_attn_kernel(bound_ref, q1_ref, q2_ref, k1_ref, k2_ref, v_ref, lam_ref, sub_ref, o_ref,
                      a1_ref, a2_ref, l1_ref, l2_ref, m1_ref, m2_ref, *, tq, lambda_init):
    j = pl.program_id(1)
    mine = (_lane_iota((tq, LANES)) >= DIFF_HEAD_DIM).astype(jnp.int32) == j % 2
    lam_v = lam_ref[...]
    lam = (jnp.exp(jnp.sum(lam_v[0:1] * lam_v[1:2], axis=-1, keepdims=True))
           - jnp.exp(jnp.sum(lam_v[2:3] * lam_v[3:4], axis=-1, keepdims=True)) + lambda_init)

    def load_q(rows):
        return tuple(jnp.where(mine, q_ref[0, rows, :].astype(F32), 0.0).astype(BF16)
                     for q_ref in (q1_ref, q2_ref))

    def store_out(rows, o1, o2):
        o = _rms(o1 - lam * o2, sub_ref[...]) * (1.0 - lambda_init)
        o_ref[0, rows, :] = o.astype(BF16)

    whole = slice(None)
    _attend_pair(bound_ref[0], load_q, ((k1_ref, whole), (k2_ref, whole)), v_ref, store_out,
                 (a1_ref, a2_ref), (l1_ref, l2_ref), (m1_ref, m2_ref), q1_ref.shape[1] // tq, tq)


def _diff_attn(proj, bound, lam_rows, subln, lambda_init, *, tq=ATTN_TILE):
    b, s, _ = proj.shape
    k_off = 2 * DIFF_HEADS * DIFF_HEAD_DIM // LANES
    v_off = 2 * k_off
    half = DIFF_HEADS // 2
    kern = functools.partial(_diff_attn_kernel, tq=tq, lambda_init=lambda_init)
    cols = lambda f: pl.BlockSpec((1, s, LANES), lambda bi, j: (bi, 0, f(j)))
    return pl.pallas_call(
        kern,
        grid=(b, DIFF_HEADS),
        in_specs=[
            pl.BlockSpec(memory_space=pltpu.SMEM),
            cols(lambda j: j // 2),
            cols(lambda j: half + j // 2),
            cols(lambda j: k_off + j // 2),
            cols(lambda j: k_off + half + j // 2),
            cols(lambda j: v_off + j),
            _resident((4, DIFF_HEAD_DIM)),
            _resident((1, LANES)),
        ],
        out_specs=cols(lambda j: j),
        out_shape=jax.ShapeDtypeStruct((b, s, DIFF_HEADS * LANES), BF16),
        scratch_shapes=_pair_scratch(tq),
        compiler_params=_params(2, VMEM_MB["diff_attn"]),
        name="diff_attn",
    )(bound, proj, proj, proj, proj, proj, lam_rows, subln.reshape(1, LANES))


def _split2(x):
    hi = x.astype(BF16)
    return hi, (x - hi.astype(F32)).astype(BF16)


def _sb_log_terms(z):
    sign_bit = jnp.uint32(0x80000000)
    neg_abs = lax.bitcast_convert_type(lax.bitcast_convert_type(z, jnp.uint32) | sign_bit, F32)
    log_beta = jnp.minimum(z, 0.0) - jnp.log2(1.0 + jnp.exp2(neg_abs))
    return log_beta, log_beta - z


def _sb_block(q, k, v, later_than, c_in):
    log_beta, log_keep = _sb_log_terms(_dot_nt(q, k))
    hi, mid = _split2(log_keep)
    later = _dot(hi, later_than) + _dot(mid, later_than)
    w = jnp.exp2(log_beta + later + c_in)
    return _dot(w.astype(BF16), v), jnp.sum(log_keep, axis=-1, keepdims=True)


def _sb_attn_kernel(q_ref, k_ref, v_ref, o_ref, c_ref, acc_ref, *, tq, n_groups):
    qi = pl.program_id(2)
    low = _lane_iota((tq, LANES)) < SB_HEAD_DIM
    q_heads = []
    for g in range(n_groups):
        q = q_ref[0, :, g * LANES:(g + 1) * LANES].astype(F32)
        q_heads += [jnp.where(low, q, 0.0).astype(BF16), jnp.where(low, 0.0, q).astype(BF16)]
    row = lax.broadcasted_iota(jnp.int32, (tq, tq), 0)
    col = lax.broadcasted_iota(jnp.int32, (tq, tq), 1)
    strict = col < row
    later_than = (row > col).astype(BF16)

    def key_rows(kb):
        return pl.ds(pl.multiple_of(kb * tq, tq), tq)

    def group_block(rows, g):
        lanes = slice(g * LANES, (g + 1) * LANES)
        return k_ref[0, rows, lanes], v_ref[0, rows, lanes]

    has_prev = qi > 0
    rows_a, rows_b = key_rows(qi), key_rows(jnp.maximum(qi - 1, 0))
    for g in range(n_groups):
        (k_a, v_a), (k_b, v_b) = group_block(rows_a, g), group_block(rows_b, g)
        k_win = jnp.concatenate([k_b, k_a], axis=0)
        v_win = jnp.concatenate([jnp.where(has_prev, v_b, jnp.zeros_like(v_b)), v_a], axis=0)
        heads = (2 * g, 2 * g + 1)
        log_beta, pieces, sums = [], [], []
        for h in heads:
            lb, lk = _sb_log_terms(_dot_nt(q_heads[h], k_win))
            lk = jnp.concatenate([lk[:, :tq], jnp.where(strict, lk[:, tq:], 0.0)], axis=1)
            hi, mid = _split2(lk)
            pieces += [hi[:, :tq], hi[:, tq:], mid[:, :tq], mid[:, tq:]]
            sum_a = jnp.sum(lk[:, tq:], axis=-1, keepdims=True)
            sum_b = jnp.where(has_prev, jnp.sum(lk[:, :tq], axis=-1, keepdims=True), 0.0)
            log_beta.append(lb)
            sums.append((sum_a, sum_b))
        later = _dot(jnp.concatenate(pieces, axis=0), later_than)
        outs = []
        for i, h in enumerate(heads):
            blk = lambda j: later[(4 * i + j) * tq:(4 * i + j + 1) * tq]
            sum_a, sum_b = sums[i]
            w_b = jnp.exp2(log_beta[i][:, :tq] + (blk(0) + blk(2)) + sum_a)
            w_a = jnp.where(strict, jnp.exp2(log_beta[i][:, tq:] + (blk(1) + blk(3))), 0.0)
            outs.append(_dot(jnp.concatenate([w_b, w_a], axis=1).astype(BF16), v_win))
            c_ref[:, h:h + 1] = sum_a + sum_b
        acc_ref[:, g * LANES:(g + 1) * LANES] = jnp.where(low, outs[0], outs[1])

    def live():
        return jnp.max(c_ref[...]) > SB_LOG2_ZERO

    def cond(state):
        kb, go = state
        return jnp.logical_and(kb >= 0, go)

    def body(state):
        kb, _ = state
        rows = key_rows(kb)
        for g in range(n_groups):
            k, v = group_block(rows, g)
            outs = []
            for h in (2 * g, 2 * g + 1):
                c_old = c_ref[:, h:h + 1]
                pv, row_sum = _sb_block(q_heads[h], k, v, later_than, c_old)
                outs.append(pv)
                c_ref[:, h:h + 1] = c_old + row_sum
            acc_ref[:, g * LANES:(g + 1) * LANES] += jnp.where(low, outs[0], outs[1])
        return kb - 1, live()

    lax.while_loop(cond, body, (qi - 2, live()))
    o_ref[0] = acc_ref[...].astype(BF16)


def _sb_attn(proj, *, tq=SB_TILE, n_groups=2):
    b, s, _ = proj.shape
    width = n_groups * LANES
    q_off = (4 * DIFF_HEADS * DIFF_HEAD_DIM + DIFF_HEADS * 2 * DIFF_HEAD_DIM) // width
    n_steps = SB_HEADS * SB_HEAD_DIM // width
    kern = functools.partial(_sb_attn_kernel, tq=tq, n_groups=n_groups)
    return pl.pallas_call(
        kern,
        grid=(b, n_steps, s // tq),
        in_specs=[
            pl.BlockSpec((1, tq, width), lambda bi, p, qi: (bi, qi, q_off + p)),
            pl.BlockSpec((1, s, width), lambda bi, p, qi: (bi, 0, q_off + n_steps + p)),
            pl.BlockSpec((1, s, width), lambda bi, p, qi: (bi, 0, q_off + 2 * n_steps + p)),
        ],
        out_specs=pl.BlockSpec((1, tq, width), lambda bi, p, qi: (bi, qi, p)),
        out_shape=jax.ShapeDtypeStruct((b, s, n_steps * width), BF16),
        scratch_shapes=[pltpu.VMEM((tq, 2 * n_groups), F32), pltpu.VMEM((tq, width), F32)],
        compiler_params=_params(3, VMEM_MB["sb_attn"]),
        name="sb_attn",
    )(proj, proj, proj)


def _mla_proj_kernel(x_ref, g_ref, wdq_ref, qln_ref, wuq_ref, wdkv_ref, kvln_ref,
                     wuk_ref, wuv_ref, gkn_ref, qc_ref, qs_ref,
                     kc_ref, ksa_ref, ksb_ref, q_ref, k_ref, v_ref):
    tm = x_ref.shape[0]
    h = _rms(x_ref[...], g_ref[...]).astype(BF16)
    inv_dim = 1.0 / MLA_QK_DIM
    c_q = _rms(_dot(h, wdq_ref[...]), qln_ref[...]).astype(BF16)
    dkv = _dot(h, wdkv_ref[...])
    c_kv = _rms(dkv[:, :MLA_KV_RANK], kvln_ref[...]).astype(BF16)
    kr = dkv[:, MLA_KV_RANK:]
    kr_sq = 0.5 * jnp.sum(kr * kr, axis=-1, keepdims=True)
    half = MLA_ROPE_DIM // 2
    kr_rot = (kr * kc_ref[...] + pltpu.roll(kr, half, 1) * ksa_ref[...]
              + pltpu.roll(kr, LANES - half, 1) * ksb_ref[...])
    v_ref[...] = _dot(c_kv, wuv_ref[...]).astype(BF16)
    low = _lane_iota((tm, LANES)) < MLA_NOPE_DIM
    qc, qs, gkn = qc_ref[...], qs_ref[...], gkn_ref[...]

    def inv_rms(squares, extra=0.0):
        return lax.rsqrt((jnp.sum(squares, axis=-1, keepdims=True) + extra) * inv_dim + NORM_EPS)

    for p2 in range(0, MLA_HEADS // 2, 2):
        kn4 = _dot(c_kv, wuk_ref[:, p2 * LANES:(p2 + 2) * LANES])
        for i in range(2):
            p = p2 + i
            cols = slice(p * 2 * LANES, (p + 1) * 2 * LANES)
            q4 = _dot(c_q, wuq_ref[:, 2 * p * 2 * LANES:2 * (p + 1) * 2 * LANES])
            q2, q2s = q4[:, :2 * LANES], q4[:, 2 * LANES:]
            q_sq = q2 * q2
            rq = jnp.concatenate([jnp.broadcast_to(inv_rms(q_sq[:, :LANES]), (tm, LANES)),
                                  jnp.broadcast_to(inv_rms(q_sq[:, LANES:]), (tm, LANES))], axis=1)
            q_ref[:, cols] = ((q2 * qc + q2s * qs) * rq).astype(BF16)
            kn2 = kn4[:, i * LANES:(i + 1) * LANES]
            k_sq = kn2 * kn2
            kg = kn2 * gkn
            k_ref[:, p * 2 * LANES:p * 2 * LANES + LANES] = (
                jnp.where(low, kg, kr_rot) * inv_rms(jnp.where(low, k_sq, 0.0), kr_sq)).astype(BF16)
            k_ref[:, p * 2 * LANES + LANES:(p + 1) * 2 * LANES] = (
                jnp.where(low, kr_rot, kg) * inv_rms(jnp.where(low, 0.0, k_sq), kr_sq)).astype(BF16)


def _mla_pair_layout(even, odd_first, odd_last, n_rows):
    z = lambda w: jnp.zeros((n_rows, w), F32)
    return jnp.concatenate([even, z(LANES - MLA_QK_DIM), odd_first, z(MLA_ROPE_DIM), odd_last], axis=1)


def _mla_proj(x, gain, w_dq, q_lat_norm, w_uq, w_dkv, kv_lat_norm, w_ukv, qk_norm_q, qk_norm_k,
              seq, *, tm=PROJ_ROWS):
    t, d = x.shape
    q_rank = w_dq.shape[1]
    half = MLA_ROPE_DIM // 2
    n_pairs = MLA_HEADS // 2
    swap = lambda a: jnp.concatenate([a[..., half:], a[..., :half]], axis=-1)

    uq = w_uq.reshape(q_rank, MLA_HEADS, MLA_QK_DIM)
    nope, rope = uq[:, :, :MLA_NOPE_DIM], uq[:, :, MLA_NOPE_DIM:]
    rope_sw = jnp.concatenate([-rope[..., half:], rope[..., :half]], axis=-1)
    zeros = jnp.zeros_like
    ev_n, ev_r, od_r, od_n = nope[:, 0::2], rope[:, 0::2], rope[:, 1::2], nope[:, 1::2]
    wuq = jnp.concatenate(
        [ev_n, ev_r, zeros(ev_r), od_r, zeros(od_r), od_n,
         zeros(ev_n), rope_sw[:, 0::2], zeros(ev_r), rope_sw[:, 1::2], zeros(od_r), zeros(od_n)],
        axis=-1).reshape(q_rank, -1).astype(BF16)

    ukv = w_ukv.reshape(MLA_KV_RANK, MLA_HEADS, MLA_NOPE_DIM + MLA_V_DIM)
    wuk = ukv[:, :, :MLA_NOPE_DIM].reshape(MLA_KV_RANK, -1).astype(BF16)
    wuv = ukv[:, :, MLA_NOPE_DIM:].reshape(MLA_KV_RANK, -1).astype(BF16)
    kr_w = w_dkv[:, MLA_KV_RANK:]
    kr_pad = jnp.zeros((d, MLA_ROPE_DIM), F32)
    wdkv = jnp.concatenate([w_dkv[:, :MLA_KV_RANK], kr_w, kr_pad, kr_w, kr_pad], axis=1).astype(BF16)

    inv = 1.0 / (ROPE_THETA ** (jnp.arange(0, MLA_ROPE_DIM, 2, dtype=F32) / MLA_ROPE_DIM))
    ang = jnp.arange(seq, dtype=F32)[:, None] * inv[None, :]
    cos, sin = jnp.tile(jnp.cos(ang), (1, 2)), jnp.tile(jnp.sin(ang), (1, 2))
    scale = MLA_QK_DIM ** -0.5
    gq_n, gq_r = qk_norm_q[:MLA_NOPE_DIM] * scale, qk_norm_q[MLA_NOPE_DIM:] * scale
    gk_n, gk_r = qk_norm_k[:MLA_NOPE_DIM], qk_norm_k[MLA_NOPE_DIM:]
    bq_n = jnp.broadcast_to(gq_n, (seq, MLA_NOPE_DIM))
    qc = _mla_pair_layout(jnp.concatenate([bq_n, cos * gq_r], axis=1), cos * gq_r, bq_n, seq)
    qs = _mla_pair_layout(jnp.concatenate([zeros(bq_n), sin * swap(gq_r)], axis=1),
                          sin * swap(gq_r), zeros(bq_n), seq)
    first = jnp.arange(MLA_ROPE_DIM) < half
    k_cos = cos * gk_r
    k_sa = jnp.where(first, 0.0, sin * swap(gk_r))
    k_sb = jnp.where(first, -sin * swap(gk_r), 0.0)
    twice = lambda a: jnp.concatenate([a, zeros(a), a, zeros(a)], axis=1)
    n_seq_tiles = seq // tm
    tab = lambda w: pl.BlockSpec((tm, w), lambda i: (i % n_seq_tiles, 0))
    hw = MLA_HEADS * LANES
    vw = MLA_HEADS * MLA_V_DIM
    row = lambda w: pl.BlockSpec((tm, w), lambda i: (i, 0))
    return pl.pallas_call(
        _mla_proj_kernel,
        grid=(t // tm,),
        in_specs=[
            row(d), _resident((1, d)),
            _resident((d, q_rank)), _resident((1, q_rank)), _resident((q_rank, 2 * hw)),
            _resident((d, MLA_KV_RANK + LANES)), _resident((1, MLA_KV_RANK)),
            _resident((MLA_KV_RANK, n_pairs * LANES)), _resident((MLA_KV_RANK, vw)),
            _resident((1, LANES)),
            tab(2 * LANES), tab(2 * LANES), tab(LANES), tab(LANES), tab(LANES),
        ],
        out_specs=[row(hw), row(hw), row(vw)],
        out_shape=[jax.ShapeDtypeStruct((t, hw), BF16), jax.ShapeDtypeStruct((t, hw), BF16),
                   jax.ShapeDtypeStruct((t, vw), BF16)],
        compiler_params=_params(1, VMEM_MB["mla_proj"]),
        name="mla_proj",
    )(x, gain.reshape(1, d), w_dq.astype(BF16), q_lat_norm.reshape(1, q_rank), wuq, wdkv,
      kv_lat_norm.reshape(1, MLA_KV_RANK), wuk, wuv,
      jnp.tile(gk_n, 2).reshape(1, LANES), qc, qs, twice(k_cos), twice(k_sa), twice(k_sb))


def _mla_attn_kernel(bound_ref, q_ref, k_ref, v_ref, o_ref,
                     a0_ref, a1_ref, l0_ref, l1_ref, m0_ref, m1_ref, *, tq):
    low = _lane_iota((tq, LANES)) < MLA_V_DIM
    lanes = (slice(0, LANES), slice(LANES, 2 * LANES))

    def load_q(rows):
        return tuple(q_ref[0, rows, ln] for ln in lanes)

    def store_out(rows, o0, o1):
        o_ref[0, rows, :] = jnp.where(low, o0, o1).astype(BF16)

    _attend_pair(bound_ref[0], load_q, ((k_ref, lanes[0]), (k_ref, lanes[1])), v_ref, store_out,
                 (a0_ref, a1_ref), (l0_ref, l1_ref), (m0_ref, m1_ref), q_ref.shape[1] // tq, tq)


def _mla_attn(q, k, v, bound, *, tq=ATTN_TILE):
    b, s, _ = q.shape
    n_pairs = MLA_HEADS // 2
    kern = functools.partial(_mla_attn_kernel, tq=tq)
    return pl.pallas_call(
        kern,
        grid=(b, n_pairs),
        in_specs=[
            pl.BlockSpec(memory_space=pltpu.SMEM),
            pl.BlockSpec((1, s, 2 * LANES), lambda bi, p: (bi, 0, p)),
            pl.BlockSpec((1, s, 2 * LANES), lambda bi, p: (bi, 0, p)),
            pl.BlockSpec((1, s, LANES), lambda bi, p: (bi, 0, p)),
        ],
        out_specs=pl.BlockSpec((1, s, LANES), lambda bi, p: (bi, 0, p)),
        out_shape=jax.ShapeDtypeStruct((b, s, n_pairs * LANES), BF16),
        scratch_shapes=_pair_scratch(tq),
        compiler_params=_params(2, VMEM_MB["mla_attn"]),
        name="mla_attn",
    )(bound, q, k, v)


def _mem_kv_kernel(m_ref, g_ref, w_ref, kn_ref, k_ref, v_ref):
    h = _rms(m_ref[0], g_ref[...]).astype(BF16)
    kv = _dot(h, w_ref[...])
    width = XM_HEADS * XM_HEAD_DIM
    for hd in range(XM_HEADS):
        lanes = slice(hd * LANES, (hd + 1) * LANES)
        k_ref[0, :, lanes] = _rms(kv[:, lanes], kn_ref[...]).astype(BF16)
    v_ref[0] = kv[:, width:].astype(BF16)


def _mem_kv(mem, gain, w_kv, k_norm):
    b, m, d = mem.shape
    width = XM_HEADS * XM_HEAD_DIM
    out = jax.ShapeDtypeStruct((b, m, width), BF16)
    blk = pl.BlockSpec((1, m, width), lambda i: (i, 0, 0))
    return pl.pallas_call(
        _mem_kv_kernel,
        grid=(b,),
        in_specs=[pl.BlockSpec((1, m, d), lambda i: (i, 0, 0)), _resident((1, d)),
                  _resident((d, 2 * width)), _resident((1, XM_HEAD_DIM))],
        out_specs=[blk, blk],
        out_shape=[out, out],
        compiler_params=_params(1, VMEM_MB["mem_kv"]),
        name="mem_kv",
    )(mem, gain.reshape(1, d), w_kv.astype(BF16), k_norm.reshape(1, XM_HEAD_DIM))


def _post_kernel(*refs, n_mix):
    x_ref = refs[0]
    mix_refs = refs[1:1 + n_mix]
    wout_ref, g_ref, wq_ref, qn_ref, k_ref, v_ref, wo_ref, o_ref = refs[1 + n_mix:]
    x = x_ref[...]
    off = 0
    for a_ref in mix_refs:
        w = a_ref.shape[1]
        x = x + _dot(a_ref[...], wout_ref[off:off + w, :])
        off += w
    h = _rms(x, g_ref[...]).astype(BF16)
    q = _dot(h, wq_ref[...])
    heads = []
    for hd in range(XM_HEADS):
        lanes = slice(hd * LANES, (hd + 1) * LANES)
        qh = (_rms(q[:, lanes], qn_ref[...]) * (XM_HEAD_DIM ** -0.5)).astype(BF16)
        s = _dot_nt(qh, k_ref[0, :, lanes])
        p = jnp.exp(s - jnp.max(s, axis=-1, keepdims=True))
        l = jnp.sum(p, axis=-1, keepdims=True)
        heads.append((_dot(p.astype(BF16), v_ref[0, :, lanes]) / l).astype(BF16))
    o = jnp.concatenate(heads, axis=-1)
    o_ref[...] = x + _dot(o, wo_ref[...])


def _post(x, mixes, w_out, gain, w_q, q_norm, mem_k, mem_v, w_o, seq, *, tm=PROJ_ROWS):
    t, d = x.shape
    b, m, width = mem_k.shape
    n_seq_tiles = seq // tm
    row = lambda w: pl.BlockSpec((tm, w), lambda i: (i, 0))
    mem_spec = pl.BlockSpec((1, m, width), lambda i: (i // n_seq_tiles, 0, 0))
    kern = functools.partial(_post_kernel, n_mix=len(mixes))
    return pl.pallas_call(
        kern,
        grid=(t // tm,),
        in_specs=[row(d)] + [row(a.shape[1]) for a in mixes] + [
            _resident(w_out.shape), _resident((1, d)), _resident((d, width)),
            _resident((1, XM_HEAD_DIM)), mem_spec, mem_spec, _resident((width, d))],
        out_specs=row(d),
        out_shape=jax.ShapeDtypeStruct((t, d), F32),
        compiler_params=_params(1, VMEM_MB["post"]),
        name="post",
    )(x, *mixes, w_out.astype(BF16), gain.reshape(1, d), w_q.astype(BF16),
      q_norm.reshape(1, XM_HEAD_DIM), mem_k, mem_v, w_o.astype(BF16))


def _rope_tables(seq, dim, start, group):
    half = dim // 2
    inv = 1.0 / (ROPE_THETA ** (jnp.arange(0, dim, 2, dtype=F32) / dim))
    ang = jnp.arange(seq, dtype=F32)[:, None] * inv[None, :]
    cos, sin = jnp.cos(ang), jnp.sin(ang)
    zeros = jnp.zeros_like(sin)
    pad = lambda a, fill: jnp.concatenate(
        [jnp.full((seq, start), fill, F32), a, jnp.full((seq, group - start - dim), fill, F32)],
        axis=1)
    cos_t = pad(jnp.concatenate([cos, cos], axis=1), 1.0)
    sin_a = pad(jnp.concatenate([zeros, sin], axis=1), 0.0)
    sin_b = pad(jnp.concatenate([-sin, zeros], axis=1), 0.0)
    reps = LANES // group
    return tuple(jnp.tile(a, (1, reps)) for a in (cos_t, sin_a, sin_b))


def kernel(x, mem, ffn_norm, ffn_w_gate, ffn_w_up, ffn_w_down, mix_norm, ab_w_in, ab_w_out, diff_q_norm, diff_k_norm, diff_lambda_q1, diff_lambda_k1, diff_lambda_q2, diff_lambda_k2, diff_subln, mla_w_dq, mla_q_norm, mla_w_uq, mla_w_dkv, mla_kv_norm, mla_w_ukv, mla_qk_norm_q, mla_qk_norm_k, mla_w_o, xm_norm, xm_mem_norm, xm_w_q, xm_w_kv, xm_q_norm, xm_k_norm, xm_w_o):
    b, s, d = x.shape
    depth = ffn_norm.shape[0]
    x = x.reshape(b * s, d)
    diff_tables = _rope_tables(s, DIFF_HEAD_DIM, 0, DIFF_HEAD_DIM)
    ffn_w = tuple(w.astype(BF16) for w in (ffn_w_gate, ffn_w_up, ffn_w_down))
    for layer in range(depth):
        i = layer // 2
        x = _ffn(x, ffn_norm[layer, 0], *ffn_w, layer, 0)
        if layer % 2 == 0:
            lambda_init = 0.8 - 0.6 * math.exp(-0.3 * layer)
            proj = _ab_proj(x, mix_norm[layer], ab_w_in[i], diff_q_norm[i], diff_k_norm[i],
                            diff_tables, s).reshape(b, s, -1)
            lam_rows = jnp.stack([diff_lambda_q1[i], diff_lambda_k1[i],
                                  diff_lambda_q2[i], diff_lambda_k2[i]])
            bound = _score_bound(diff_q_norm[i], diff_k_norm[i], DIFF_HEAD_DIM)
            oa = _diff_attn(proj, bound, lam_rows, diff_subln[i], lambda_init)
            ob = _sb_attn(proj)
            mixes = [oa.reshape(b * s, -1), ob.reshape(b * s, -1)]
            w_out = ab_w_out[i]
        else:
            q, k, v = _mla_proj(x, mix_norm[layer], mla_w_dq[i], mla_q_norm[i], mla_w_uq[i],
                                mla_w_dkv[i], mla_kv_norm[i], mla_w_ukv[i], mla_qk_norm_q[i],
                                mla_qk_norm_k[i], s)
            bound = _score_bound(mla_qk_norm_q[i], mla_qk_norm_k[i], MLA_QK_DIM)
            o = _mla_attn(q.reshape(b, s, -1), k.reshape(b, s, -1), v.reshape(b, s, -1), bound)
            mixes = [o.reshape(b * s, -1)]
            w_out = mla_w_o[i]
        mem_k, mem_v = _mem_kv(mem, xm_mem_norm[layer], xm_w_kv[layer], xm_k_norm[layer])
        x = _post(x, mixes, w_out, xm_norm[layer], xm_w_q[layer], xm_q_norm[layer],
                  mem_k, mem_v, xm_w_o[layer], s)
        x = _ffn(x, ffn_norm[layer, 1], *ffn_w, layer, 1)
    return x.reshape(b, s, d)
```

```python
import functools
import math

import jax
import jax.numpy as jnp
from jax import lax
from jax.experimental import pallas as pl
from jax.experimental.pallas import tpu as pltpu

F32 = jnp.float32
BF16 = jnp.bfloat16

LANES = 128
V7X_MXU_DIM = 256
V7X_VMEM_LIMIT = 56 * 1024 * 1024

FFN_ROWS = 1024
PROJ_ROWS = 1024
ATTN_TILE = 512
SB_TILE = 256
VMEM_MB = dict(ffn=56, ab_proj=56, diff_attn=48, sb_attn=48, mla_proj=56, mla_attn=48,
               mem_kv=32, post=56)

CHUNK = 64
ROPE_THETA = 10000.0
NORM_EPS = 1e-6
SB_LOG2_ZERO = -151.0
LOG2_E = math.log2(math.e)
MAX_SOFTMAX_SHIFT = 40.0

DIFF_HEADS = 4
DIFF_HEAD_DIM = 64
SB_HEADS = 8
SB_HEAD_DIM = 64
MLA_HEADS = 16
MLA_KV_RANK = 256
MLA_NOPE_DIM = 64
MLA_ROPE_DIM = 32
MLA_V_DIM = 64
MLA_QK_DIM = MLA_NOPE_DIM + MLA_ROPE_DIM
XM_HEADS = 4
XM_HEAD_DIM = 128


def _params(n_grid, vmem_mb):
    return pltpu.CompilerParams(
        dimension_semantics=("arbitrary",) * n_grid,
        vmem_limit_bytes=min(vmem_mb * 1024 * 1024, V7X_VMEM_LIMIT))


def _resident(shape):
    nd = len(shape)
    return pl.BlockSpec(shape, lambda *_: (0,) * nd, pipeline_mode=pl.Buffered(1))


def _rms(x, gain):
    return x * lax.rsqrt(jnp.mean(x * x, axis=-1, keepdims=True) + NORM_EPS) * gain


def _dot(a, b):
    return jnp.dot(a, b, preferred_element_type=F32)


def _dot_nt(a, b):
    return lax.dot_general(a, b, (((1,), (1,)), ((), ())), preferred_element_type=F32)


def _lane_iota(shape):
    return lax.broadcasted_iota(jnp.int32, shape, len(shape) - 1)


def _rope(x, cos, sin_a, sin_b, half):
    return x * cos + pltpu.roll(x, half, 1) * sin_a + pltpu.roll(x, LANES - half, 1) * sin_b


def _ffn_kernel(x_ref, g_ref, wg_ref, wu_ref, wd_ref, o_ref, h_ref, acc_ref, *, tf):
    x = x_ref[...]
    h_ref[...] = _rms(x, g_ref[...]).astype(BF16)
    for j in range(wg_ref.shape[1] // tf):
        cols = slice(j * tf, (j + 1) * tf)
        h = h_ref[...]
        g = _dot(h, wg_ref[:, cols])
        u = _dot(h, wu_ref[:, cols])
        a = (g * jax.nn.sigmoid(g) * u).astype(BF16)
        down = _dot(a, wd_ref[cols, :])
        if j == 0:
            acc_ref[...] = down
        else:
            acc_ref[...] += down
    o_ref[...] = x + 0.5 * acc_ref[...]


def _ffn(x, gain, w_gate, w_up, w_down, layer, half, *, tm=FFN_ROWS, tf=V7X_MXU_DIM):
    t, d = x.shape
    d_ff = w_gate.shape[-1]
    pick = lambda r, c: pl.BlockSpec((None, None, r, c), lambda i: (layer, half, 0, 0),
                                     pipeline_mode=pl.Buffered(1))
    return pl.pallas_call(
        functools.partial(_ffn_kernel, tf=tf),
        grid=(t // tm,),
        in_specs=[
            pl.BlockSpec((tm, d), lambda i: (i, 0)),
            _resident((1, d)),
            pick(d, d_ff), pick(d, d_ff), pick(d_ff, d),
        ],
        out_specs=pl.BlockSpec((tm, d), lambda i: (i, 0)),
        out_shape=jax.ShapeDtypeStruct((t, d), F32),
        scratch_shapes=[pltpu.VMEM((tm, d), BF16), pltpu.VMEM((tm, d), F32)],
        compiler_params=_params(1, VMEM_MB["ffn"]),
        name="ffn",
    )(x, gain.reshape(1, d), w_gate, w_up, w_down)


def _ab_proj_kernel(x_ref, g_ref, w_ref, qn_ref, kn_ref, cos_ref, sa_ref, sb_ref, o_ref, *,
                    n_qk_groups, q_scales):
    h = _rms(x_ref[...], g_ref[...]).astype(BF16)
    cos, sa, sb = cos_ref[...], sa_ref[...], sb_ref[...]
    n_groups = w_ref.shape[1] // LANES
    lane = _lane_iota((x_ref.shape[0], LANES))
    low = lane < DIFF_HEAD_DIM
    for c in range(0, n_groups, 2):
        y2 = _dot(h, w_ref[:, c * LANES:(c + 2) * LANES])
        for k in range(2):
            g = c + k
            y = y2[:, k * LANES:(k + 1) * LANES]
            if g < 2 * n_qk_groups:
                gain = qn_ref[...] if g < n_qk_groups else kn_ref[...]
                sq = y * y
                ss_lo = jnp.sum(jnp.where(low, sq, 0.0), axis=-1, keepdims=True)
                ss_hi = jnp.sum(jnp.where(low, 0.0, sq), axis=-1, keepdims=True)
                ms = jnp.where(low, ss_lo, ss_hi) * (1.0 / DIFF_HEAD_DIM)
                y = y * lax.rsqrt(ms + NORM_EPS) * gain
                y = _rope(y, cos, sa, sb, DIFF_HEAD_DIM // 2)
            if g in q_scales:
                y = y * q_scales[g]
            o_ref[:, g * LANES:(g + 1) * LANES] = y.astype(BF16)


def _ab_proj(x, gain, w_in, q_norm, k_norm, tables, seq, *, tm=PROJ_ROWS):
    t, d = x.shape
    n = w_in.shape[1]
    n_qk_groups = 2 * DIFF_HEADS * DIFF_HEAD_DIM // LANES
    sb_q0 = (4 * DIFF_HEADS * DIFF_HEAD_DIM + DIFF_HEADS * 2 * DIFF_HEAD_DIM) // LANES
    q_scales = {g: DIFF_HEAD_DIM ** -0.5 for g in range(n_qk_groups)}
    q_scales.update({g: SB_HEAD_DIM ** -0.5 * LOG2_E
                     for g in range(sb_q0, sb_q0 + SB_HEADS * SB_HEAD_DIM // LANES)})
    cos, sa, sb = tables
    n_seq_tiles = seq // tm
    tab_spec = pl.BlockSpec((tm, LANES), lambda i: (i % n_seq_tiles, 0))
    kern = functools.partial(_ab_proj_kernel, n_qk_groups=n_qk_groups,
                             q_scales=q_scales)
    return pl.pallas_call(
        kern,
        grid=(t // tm,),
        in_specs=[
            pl.BlockSpec((tm, d), lambda i: (i, 0)),
            _resident((1, d)),
            _resident((d, n)),
            _resident((1, LANES)),
            _resident((1, LANES)),
            tab_spec, tab_spec, tab_spec,
        ],
        out_specs=pl.BlockSpec((tm, n), lambda i: (i, 0)),
        out_shape=jax.ShapeDtypeStruct((t, n), BF16),
        compiler_params=_params(1, VMEM_MB["ab_proj"]),
        name="ab_proj",
    )(x, gain.reshape(1, d), w_in.astype(BF16),
      jnp.tile(q_norm, 2).reshape(1, LANES), jnp.tile(k_norm, 2).reshape(1, LANES),
      cos, sa, sb)


def _chunk_causal_mask(tq, tk):
    qc = lax.broadcasted_iota(jnp.int32, (tq, tk), 0) // CHUNK
    kc = lax.broadcasted_iota(jnp.int32, (tq, tk), 1) // CHUNK
    return kc <= qc


def _fold_lanes(p):
    out = p[:, :LANES]
    for c in range(1, p.shape[1] // LANES):
        out = out + p[:, c * LANES:(c + 1) * LANES]
    return out


def _attend_pair(bound, load_q, k_srcs, v_ref, store_out, acc_refs, l_refs, m_refs, n_q, tq):
    def key_block(h, rows):
        ref, lanes = k_srcs[h]
        return ref[0, rows, lanes]

    @pl.when(bound <= MAX_SOFTMAX_SHIFT)
    def _():
        hq = tq // 2
        lower_rows = lambda a, b: jnp.concatenate([a[:hq], a[hq:] + b], axis=0)
        for qi in range(n_q):
            q_rows = slice(qi * tq, (qi + 1) * tq)
            qs = load_q(q_rows)
            acc, lsum = [None, None], [None, None]
            for kb in range(qi):
                rows = slice(kb * tq, (kb + 1) * tq)
                v = v_ref[0, rows, :]
                for h in range(2):
                    p = jnp.exp(_dot_nt(qs[h], key_block(h, rows)) - bound)
                    part, pv = _fold_lanes(p), _dot(p.astype(BF16), v)
                    lsum[h] = part if kb == 0 else lsum[h] + part
                    acc[h] = pv if kb == 0 else acc[h] + pv
            rows_a = slice(qi * tq, qi * tq + hq)
            rows_b = slice(qi * tq + hq, (qi + 1) * tq)
            v_a, v_b = v_ref[0, rows_a, :], v_ref[0, rows_b, :]
            for h in range(2):
                p_a = jnp.exp(_dot_nt(qs[h], key_block(h, rows_a)) - bound)
                p_a = jnp.where(_chunk_causal_mask(tq, hq), p_a, 0.0)
                p_b = jnp.exp(_dot_nt(qs[h][hq:], key_block(h, rows_b)) - bound)
                p_b = jnp.where(_chunk_causal_mask(hq, hq), p_b, 0.0)
                part = lower_rows(_fold_lanes(p_a), _fold_lanes(p_b))
                pv = lower_rows(_dot(p_a.astype(BF16), v_a), _dot(p_b.astype(BF16), v_b))
                lsum[h] = part if qi == 0 else lsum[h] + part
                acc[h] = pv if qi == 0 else acc[h] + pv
            store_out(q_rows, *(acc[h] / jnp.sum(lsum[h], axis=-1, keepdims=True)
                                for h in range(2)))

    @pl.when(bound > MAX_SOFTMAX_SHIFT)
    def _():
        def q_tile(qi, carry):
            q_rows = pl.ds(pl.multiple_of(qi * tq, tq), tq)
            qs = load_q(q_rows)
            for h in range(2):
                acc_refs[h][...] = jnp.zeros_like(acc_refs[h])
                l_refs[h][...] = jnp.zeros_like(l_refs[h])
                m_refs[h][...] = jnp.full_like(m_refs[h], -jnp.inf)

            def block(kb, masked):
                rows = pl.ds(pl.multiple_of(kb * tq, tq), tq)
                v = v_ref[0, rows, :]
                for h in range(2):
                    s = _dot_nt(qs[h], key_block(h, rows))
                    if masked:
                        s = jnp.where(_chunk_causal_mask(tq, tq), s, -jnp.inf)
                    m_old = m_refs[h][...]
                    m_new = jnp.maximum(m_old, jnp.max(s, axis=-1, keepdims=True))
                    alpha = jnp.exp(m_old - m_new)
                    p = jnp.exp(s - m_new)
                    l_refs[h][...] = alpha * l_refs[h][...] + jnp.sum(p, axis=-1, keepdims=True)
                    acc_refs[h][...] = alpha * acc_refs[h][...] + _dot(p.astype(BF16), v)
                    m_refs[h][...] = m_new

            def body(kb, c):
                block(kb, False)
                return c

            lax.fori_loop(0, qi, body, 0)
            block(qi, True)
            store_out(q_rows, *(acc_refs[h][...] / l_refs[h][...] for h in range(2)))
            return carry

        lax.fori_loop(0, n_q, q_tile, 0)


def _pair_scratch(tq):
    acc = pltpu.VMEM((tq, LANES), F32)
    stat = pltpu.VMEM((tq, 1), F32)
    return [acc, acc, stat, stat, stat, stat]


def _score_bound(q_gain, k_gain, dim):
    return (jnp.max(jnp.abs(q_gain)) * jnp.max(jnp.abs(k_gain)) * math.sqrt(dim)).reshape(1)


def _diff_attn_kernel(bound_ref, q1_ref, q2_ref, k1_ref, k2_ref, v_ref, lam_ref, sub_ref, o_ref,
                      a1_ref, a2_ref, l1_ref, l2_ref, m1_ref, m2_ref, *, tq, lambda_init):
    j = pl.program_id(1)
    mine = (_lane_iota((tq, LANES)) >= DIFF_HEAD_DIM).astype(jnp.int32) == j % 2
    lam_v = lam_ref[...]
    lam = (jnp.exp(jnp.sum(lam_v[0:1] * lam_v[1:2], axis=-1, keepdims=True))
           - jnp.exp(jnp.sum(lam_v[2:3] * lam_v[3:4], axis=-1, keepdims=True)) + lambda_init)

    def load_q(rows):
        return tuple(jnp.where(mine, q_ref[0, rows, :].astype(F32), 0.0).astype(BF16)
                     for q_ref in (q1_ref, q2_ref))

    def store_out(rows, o1, o2):
        o = _rms(o1 - lam * o2, sub_ref[...]) * (1.0 - lambda_init)
        o_ref[0, rows, :] = o.astype(BF16)

    whole = slice(None)
    _attend_pair(bound_ref[0], load_q, ((k1_ref, whole), (k2_ref, whole)), v_ref, store_out,
                 (a1_ref, a2_ref), (l1_ref, l2_ref), (m1_ref, m2_ref), q1_ref.shape[1] // tq, tq)


def _diff_attn(proj, bound, lam_rows, subln, lambda_init, *, tq=ATTN_TILE):
    b, s, _ = proj.shape
    k_off = 2 * DIFF_HEADS * DIFF_HEAD_DIM // LANES
    v_off = 2 * k_off
    half = DIFF_HEADS // 2
    kern = functools.partial(_diff_attn_kernel, tq=tq, lambda_init=lambda_init)
    cols = lambda f: pl.BlockSpec((1, s, LANES), lambda bi, j: (bi, 0, f(j)))
    return pl.pallas_call(
        kern,
        grid=(b, DIFF_HEADS),
        in_specs=[
            pl.BlockSpec(memory_space=pltpu.SMEM),
            cols(lambda j: j // 2),
            cols(lambda j: half + j // 2),
            cols(lambda j: k_off + j // 2),
            cols(lambda j: k_off + half + j // 2),
            cols(lambda j: v_off + j),
            _resident((4, DIFF_HEAD_DIM)),
            _resident((1, LANES)),
        ],
        out_specs=cols(lambda j: j),
        out_shape=jax.ShapeDtypeStruct((b, s, DIFF_HEADS * LANES), BF16),
        scratch_shapes=_pair_scratch(tq),
        compiler_params=_params(2, VMEM_MB["diff_attn"]),
        name="diff_attn",
    )(bound, proj, proj, proj, proj, proj, lam_rows, subln.reshape(1, LANES))


def _split2(x):
    hi = x.astype(BF16)
    return hi, (x - hi.astype(F32)).astype(BF16)


def _sb_log_terms(z):
    sign_bit = jnp.uint32(0x80000000)
    neg_abs = lax.bitcast_convert_type(lax.bitcast_convert_type(z, jnp.uint32) | sign_bit, F32)
    log_beta = jnp.minimum(z, 0.0) - jnp.log2(1.0 + jnp.exp2(neg_abs))
    return log_beta, log_beta - z


def _sb_block(q, k, v, later_than, c_in):
    log_beta, log_keep = _sb_log_terms(_dot_nt(q, k))
    hi, mid = _split2(log_keep)
    later = _dot(hi, later_than) + _dot(mid, later_than)
    w = jnp.exp2(log_beta + later + c_in)
    return _dot(w.astype(BF16), v), jnp.sum(log_keep, axis=-1, keepdims=True)


def _sb_attn_kernel(q_ref, k_ref, v_ref, o_ref, c_ref, acc_ref, *, tq, n_groups):
    qi = pl.program_id(2)
    low = _lane_iota((tq, LANES)) < SB_HEAD_DIM
    q_heads = []
    for g in range(n_groups):
        q = q_ref[0, :, g * LANES:(g + 1) * LANES].astype(F32)
        q_heads += [jnp.where(low, q, 0.0).astype(BF16), jnp.where(low, 0.0, q).astype(BF16)]
    row = lax.broadcasted_iota(jnp.int32, (tq, tq), 0)
    col = lax.broadcasted_iota(jnp.int32, (tq, tq), 1)
    strict = col < row
    later_than = (row > col).astype(BF16)

    def key_rows(kb):
        return pl.ds(pl.multiple_of(kb * tq, tq), tq)

    def group_block(rows, g):
        lanes = slice(g * LANES, (g + 1) * LANES)
        return k_ref[0, rows, lanes], v_ref[0, rows, lanes]

    has_prev = qi > 0
    rows_a, rows_b = key_rows(qi), key_rows(jnp.maximum(qi - 1, 0))
    v_wins, scores = [], []
    for g in range(n_groups):
        (k_a, v_a), (k_b, v_b) = group_block(rows_a, g), group_block(rows_b, g)
        k_win = jnp.concatenate([k_b, k_a], axis=0)
        v_wins.append(jnp.concatenate([jnp.where(has_prev, v_b, jnp.zeros_like(v_b)), v_a], axis=0))
        scores.append([_dot_nt(q_heads[h], k_win) for h in (2 * g, 2 * g + 1)])

    def gate_terms(g):
        log_beta, pieces, sums = [], [], []
        for z in scores[g]:
            lb, lk = _sb_log_terms(z)
            lk = jnp.concatenate([lk[:, :tq], jnp.where(strict, lk[:, tq:], 0.0)], axis=1)
            hi, mid = _split2(lk)
            pieces += [hi[:, :tq], hi[:, tq:], mid[:, :tq], mid[:, tq:]]
            sum_a = jnp.sum(lk[:, tq:], axis=-1, keepdims=True)
            sum_b = jnp.where(has_prev, jnp.sum(lk[:, :tq], axis=-1, keepdims=True), 0.0)
            log_beta.append(lb)
            sums.append((sum_a, sum_b))
        return log_beta, _dot(jnp.concatenate(pieces, axis=0), later_than), sums

    def weights_and_values(g, log_beta, later, sums):
        outs = []
        for i, h in enumerate((2 * g, 2 * g + 1)):
            blk = lambda j: later[(4 * i + j) * tq:(4 * i + j + 1) * tq]
            sum_a, sum_b = sums[i]
            w_b = jnp.exp2(log_beta[i][:, :tq] + (blk(0) + blk(2)) + sum_a)
            w_a = jnp.where(strict, jnp.exp2(log_beta[i][:, tq:] + (blk(1) + blk(3))), 0.0)
            outs.append(_dot(jnp.concatenate([w_b, w_a], axis=1).astype(BF16), v_wins[g]))
            c_ref[:, h:h + 1] = sum_a + sum_b
        acc_ref[:, g * LANES:(g + 1) * LANES] = jnp.where(low, outs[0], outs[1])

    staged = None
    for g in range(n_groups):
        current = gate_terms(g)
        if staged is not None:
            weights_and_values(g - 1, *staged)
        staged = current
    weights_and_values(n_groups - 1, *staged)

    def live():
        return jnp.max(c_ref[...]) > SB_LOG2_ZERO

    def cond(state):
        kb, go = state
        return jnp.logical_and(kb >= 0, go)

    def body(state):
        kb, _ = state
        rows = key_rows(kb)
        for g in range(n_groups):
            k, v = group_block(rows, g)
            outs = []
            for h in (2 * g, 2 * g + 1):
                c_old = c_ref[:, h:h + 1]
                pv, row_sum = _sb_block(q_heads[h], k, v, later_than, c_old)
                outs.append(pv)
                c_ref[:, h:h + 1] = c_old + row_sum
            acc_ref[:, g * LANES:(g + 1) * LANES] += jnp.where(low, outs[0], outs[1])
        return kb - 1, live()

    lax.while_loop(cond, body, (qi - 2, live()))
    o_ref[0] = acc_ref[...].astype(BF16)


def _sb_attn(proj, *, tq=SB_TILE, n_groups=SB_HEADS * SB_HEAD_DIM // LANES):
    b, s, _ = proj.shape
    width = n_groups * LANES
    q_off = (4 * DIFF_HEADS * DIFF_HEAD_DIM + DIFF_HEADS * 2 * DIFF_HEAD_DIM) // width
    n_steps = SB_HEADS * SB_HEAD_DIM // width
    kern = functools.partial(_sb_attn_kernel, tq=tq, n_groups=n_groups)
    return pl.pallas_call(
        kern,
        grid=(b, n_steps, s // tq),
        in_specs=[
            pl.BlockSpec((1, tq, width), lambda bi, p, qi: (bi, qi, q_off + p)),
            pl.BlockSpec((1, s, width), lambda bi, p, qi: (bi, 0, q_off + n_steps + p)),
            pl.BlockSpec((1, s, width), lambda bi, p, qi: (bi, 0, q_off + 2 * n_steps + p)),
        ],
        out_specs=pl.BlockSpec((1, tq, width), lambda bi, p, qi: (bi, qi, p)),
        out_shape=jax.ShapeDtypeStruct((b, s, n_steps * width), BF16),
        scratch_shapes=[pltpu.VMEM((tq, 2 * n_groups), F32), pltpu.VMEM((tq, width), F32)],
        compiler_params=_params(3, VMEM_MB["sb_attn"]),
        name="sb_attn",
    )(proj, proj, proj)


def _mla_proj_kernel(x_ref, g_ref, wdq_ref, qln_ref, wuq_ref, wdkv_ref, kvln_ref,
                     wuk_ref, wuv_ref, gkn_ref, qc_ref, qs_ref,
                     kc_ref, ksa_ref, ksb_ref, q_ref, k_ref, v_ref):
    tm = x_ref.shape[0]
    h = _rms(x_ref[...], g_ref[...]).astype(BF16)
    inv_dim = 1.0 / MLA_QK_DIM
    c_q = _rms(_dot(h, wdq_ref[...]), qln_ref[...]).astype(BF16)
    dkv = _dot(h, wdkv_ref[...])
    c_kv = _rms(dkv[:, :MLA_KV_RANK], kvln_ref[...]).astype(BF16)
    kr = dkv[:, MLA_KV_RANK:]
    kr_sq = 0.5 * jnp.sum(kr * kr, axis=-1, keepdims=True)
    half = MLA_ROPE_DIM // 2
    kr_rot = (kr * kc_ref[...] + pltpu.roll(kr, half, 1) * ksa_ref[...]
              + pltpu.roll(kr, LANES - half, 1) * ksb_ref[...])
    v_ref[...] = _dot(c_kv, wuv_ref[...]).astype(BF16)
    low = _lane_iota((tm, LANES)) < MLA_NOPE_DIM
    qc, qs, gkn = qc_ref[...], qs_ref[...], gkn_ref[...]

    def inv_rms(squares, extra=0.0):
        return lax.rsqrt((jnp.sum(squares, axis=-1, keepdims=True) + extra) * inv_dim + NORM_EPS)

    for p2 in range(0, MLA_HEADS // 2, 2):
        kn4 = _dot(c_kv, wuk_ref[:, p2 * LANES:(p2 + 2) * LANES])
        for i in range(2):
            p = p2 + i
            cols = slice(p * 2 * LANES, (p + 1) * 2 * LANES)
            q4 = _dot(c_q, wuq_ref[:, 2 * p * 2 * LANES:2 * (p + 1) * 2 * LANES])
            q2, q2s = q4[:, :2 * LANES], q4[:, 2 * LANES:]
            q_sq = q2 * q2
            rq = jnp.concatenate([jnp.broadcast_to(inv_rms(q_sq[:, :LANES]), (tm, LANES)),
                                  jnp.broadcast_to(inv_rms(q_sq[:, LANES:]), (tm, LANES))], axis=1)
            q_ref[:, cols] = ((q2 * qc + q2s * qs) * rq).astype(BF16)
            kn2 = kn4[:, i * LANES:(i + 1) * LANES]
            k_sq = kn2 * kn2
            kg = kn2 * gkn
            k_ref[:, p * 2 * LANES:p * 2 * LANES + LANES] = (
                jnp.where(low, kg, kr_rot) * inv_rms(jnp.where(low, k_sq, 0.0), kr_sq)).astype(BF16)
            k_ref[:, p * 2 * LANES + LANES:(p + 1) * 2 * LANES] = (
                jnp.where(low, kr_rot, kg) * inv_rms(jnp.where(low, 0.0, k_sq), kr_sq)).astype(BF16)


def _mla_pair_layout(even, odd_first, odd_last, n_rows):
    z = lambda w: jnp.zeros((n_rows, w), F32)
    return jnp.concatenate([even, z(LANES - MLA_QK_DIM), odd_first, z(MLA_ROPE_DIM), odd_last], axis=1)


def _mla_proj(x, gain, w_dq, q_lat_norm, w_uq, w_dkv, kv_lat_norm, w_ukv, qk_norm_q, qk_norm_k,
              seq, *, tm=PROJ_ROWS):
    t, d = x.shape
    q_rank = w_dq.shape[1]
    half = MLA_ROPE_DIM // 2
    n_pairs = MLA_HEADS // 2
    swap = lambda a: jnp.concatenate([a[..., half:], a[..., :half]], axis=-1)

    uq = w_uq.reshape(q_rank, MLA_HEADS, MLA_QK_DIM)
    nope, rope = uq[:, :, :MLA_NOPE_DIM], uq[:, :, MLA_NOPE_DIM:]
    rope_sw = jnp.concatenate([-rope[..., half:], rope[..., :half]], axis=-1)
    zeros = jnp.zeros_like
    ev_n, ev_r, od_r, od_n = nope[:, 0::2], rope[:, 0::2], rope[:, 1::2], nope[:, 1::2]
    wuq = jnp.concatenate(
        [ev_n, ev_r, zeros(ev_r), od_r, zeros(od_r), od_n,
         zeros(ev_n), rope_sw[:, 0::2], zeros(ev_r), rope_sw[:, 1::2], zeros(od_r), zeros(od_n)],
        axis=-1).reshape(q_rank, -1).astype(BF16)

    ukv = w_ukv.reshape(MLA_KV_RANK, MLA_HEADS, MLA_NOPE_DIM + MLA_V_DIM)
    wuk = ukv[:, :, :MLA_NOPE_DIM].reshape(MLA_KV_RANK, -1).astype(BF16)
    wuv = ukv[:, :, MLA_NOPE_DIM:].reshape(MLA_KV_RANK, -1).astype(BF16)
    kr_w = w_dkv[:, MLA_KV_RANK:]
    kr_pad = jnp.zeros((d, MLA_ROPE_DIM), F32)
    wdkv = jnp.concatenate([w_dkv[:, :MLA_KV_RANK], kr_w, kr_pad, kr_w, kr_pad], axis=1).astype(BF16)

    inv = 1.0 / (ROPE_THETA ** (jnp.arange(0, MLA_ROPE_DIM, 2, dtype=F32) / MLA_ROPE_DIM))
    ang = jnp.arange(seq, dtype=F32)[:, None] * inv[None, :]
    cos, sin = jnp.tile(jnp.cos(ang), (1, 2)), jnp.tile(jnp.sin(ang), (1, 2))
    scale = MLA_QK_DIM ** -0.5
    gq_n, gq_r = qk_norm_q[:MLA_NOPE_DIM] * scale, qk_norm_q[MLA_NOPE_DIM:] * scale
    gk_n, gk_r = qk_norm_k[:MLA_NOPE_DIM], qk_norm_k[MLA_NOPE_DIM:]
    bq_n = jnp.broadcast_to(gq_n, (seq, MLA_NOPE_DIM))
    qc = _mla_pair_layout(jnp.concatenate([bq_n, cos * gq_r], axis=1), cos * gq_r, bq_n, seq)
    qs = _mla_pair_layout(jnp.concatenate([zeros(bq_n), sin * swap(gq_r)], axis=1),
                          sin * swap(gq_r), zeros(bq_n), seq)
    first = jnp.arange(MLA_ROPE_DIM) < half
    k_cos = cos * gk_r
    k_sa = jnp.where(first, 0.0, sin * swap(gk_r))
    k_sb = jnp.where(first, -sin * swap(gk_r), 0.0)
    twice = lambda a: jnp.concatenate([a, zeros(a), a, zeros(a)], axis=1)
    n_seq_tiles = seq // tm
    tab = lambda w: pl.BlockSpec((tm, w), lambda i: (i % n_seq_tiles, 0))
    hw = MLA_HEADS * LANES
    vw = MLA_HEADS * MLA_V_DIM
    row = lambda w: pl.BlockSpec((tm, w), lambda i: (i, 0))
    return pl.pallas_call(
        _mla_proj_kernel,
        grid=(t // tm,),
        in_specs=[
            row(d), _resident((1, d)),
            _resident((d, q_rank)), _resident((1, q_rank)), _resident((q_rank, 2 * hw)),
            _resident((d, MLA_KV_RANK + LANES)), _resident((1, MLA_KV_RANK)),
            _resident((MLA_KV_RANK, n_pairs * LANES)), _resident((MLA_KV_RANK, vw)),
            _resident((1, LANES)),
            tab(2 * LANES), tab(2 * LANES), tab(LANES), tab(LANES), tab(LANES),
        ],
        out_specs=[row(hw), row(hw), row(vw)],
        out_shape=[jax.ShapeDtypeStruct((t, hw), BF16), jax.ShapeDtypeStruct((t, hw), BF16),
                   jax.ShapeDtypeStruct((t, vw), BF16)],
        compiler_params=_params(1, VMEM_MB["mla_proj"]),
        name="mla_proj",
    )(x, gain.reshape(1, d), w_dq.astype(BF16), q_lat_norm.reshape(1, q_rank), wuq, wdkv,
      kv_lat_norm.reshape(1, MLA_KV_RANK), wuk, wuv,
      jnp.tile(gk_n, 2).reshape(1, LANES), qc, qs, twice(k_cos), twice(k_sa), twice(k_sb))


def _mla_attn_kernel(bound_ref, q_ref, k_ref, v_ref, o_ref,
                     a0_ref, a1_ref, l0_ref, l1_ref, m0_ref, m1_ref, *, tq):
    low = _lane_iota((tq, LANES)) < MLA_V_DIM
    lanes = (slice(0, LANES), slice(LANES, 2 * LANES))

    def load_q(rows):
        return tuple(q_ref[0, rows, ln] for ln in lanes)

    def store_out(rows, o0, o1):
        o_ref[0, rows, :] = jnp.where(low, o0, o1).astype(BF16)

    _attend_pair(bound_ref[0], load_q, ((k_ref, lanes[0]), (k_ref, lanes[1])), v_ref, store_out,
                 (a0_ref, a1_ref), (l0_ref, l1_ref), (m0_ref, m1_ref), q_ref.shape[1] // tq, tq)


def _mla_attn(q, k, v, bound, *, tq=ATTN_TILE):
    b, s, _ = q.shape
    n_pairs = MLA_HEADS // 2
    kern = functools.partial(_mla_attn_kernel, tq=tq)
    return pl.pallas_call(
        kern,
        grid=(b, n_pairs),
        in_specs=[
            pl.BlockSpec(memory_space=pltpu.SMEM),
            pl.BlockSpec((1, s, 2 * LANES), lambda bi, p: (bi, 0, p)),
            pl.BlockSpec((1, s, 2 * LANES), lambda bi, p: (bi, 0, p)),
            pl.BlockSpec((1, s, LANES), lambda bi, p: (bi, 0, p)),
        ],
        out_specs=pl.BlockSpec((1, s, LANES), lambda bi, p: (bi, 0, p)),
        out_shape=jax.ShapeDtypeStruct((b, s, n_pairs * LANES), BF16),
        scratch_shapes=_pair_scratch(tq),
        compiler_params=_params(2, VMEM_MB["mla_attn"]),
        name="mla_attn",
    )(bound, q, k, v)


def _mem_kv_kernel(m_ref, g_ref, w_ref, kn_ref, k_ref, v_ref):
    h = _rms(m_ref[0], g_ref[...]).astype(BF16)
    kv = _dot(h, w_ref[...])
    width = XM_HEADS * XM_HEAD_DIM
    for hd in range(XM_HEADS):
        lanes = slice(hd * LANES, (hd + 1) * LANES)
        k_ref[0, :, lanes] = _rms(kv[:, lanes], kn_ref[...]).astype(BF16)
    v_ref[0] = kv[:, width:].astype(BF16)


def _mem_kv(mem, gain, w_kv, k_norm):
    b, m, d = mem.shape
    width = XM_HEADS * XM_HEAD_DIM
    out = jax.ShapeDtypeStruct((b, m, width), BF16)
    blk = pl.BlockSpec((1, m, width), lambda i: (i, 0, 0))
    return pl.pallas_call(
        _mem_kv_kernel,
        grid=(b,),
        in_specs=[pl.BlockSpec((1, m, d), lambda i: (i, 0, 0)), _resident((1, d)),
                  _resident((d, 2 * width)), _resident((1, XM_HEAD_DIM))],
        out_specs=[blk, blk],
        out_shape=[out, out],
        compiler_params=_params(1, VMEM_MB["mem_kv"]),
        name="mem_kv",
    )(mem, gain.reshape(1, d), w_kv.astype(BF16), k_norm.reshape(1, XM_HEAD_DIM))


def _post_kernel(*refs, n_mix):
    x_ref = refs[0]
    mix_refs = refs[1:1 + n_mix]
    wout_ref, g_ref, wq_ref, qn_ref, k_ref, v_ref, wo_ref, o_ref = refs[1 + n_mix:]
    x = x_ref[...]
    off = 0
    for a_ref in mix_refs:
        w = a_ref.shape[1]
        x = x + _dot(a_ref[...], wout_ref[off:off + w, :])
        off += w
    h = _rms(x, g_ref[...]).astype(BF16)
    q = _dot(h, wq_ref[...])
    lanes = [slice(hd * LANES, (hd + 1) * LANES) for hd in range(XM_HEADS)]
    scores = []
    for hd in range(XM_HEADS):
        qh = (_rms(q[:, lanes[hd]], qn_ref[...]) * (XM_HEAD_DIM ** -0.5)).astype(BF16)
        scores.append(_dot_nt(qh, k_ref[0, :, lanes[hd]]))

    def softmax(s):
        p = jnp.exp(s - jnp.max(s, axis=-1, keepdims=True))
        return p.astype(BF16), jnp.sum(p, axis=-1, keepdims=True)

    heads, staged = [], None
    for hd in range(XM_HEADS):
        current = softmax(scores[hd])
        if staged is not None:
            heads.append((_dot(staged[0], v_ref[0, :, lanes[hd - 1]]) / staged[1]).astype(BF16))
        staged = current
    heads.append((_dot(staged[0], v_ref[0, :, lanes[XM_HEADS - 1]]) / staged[1]).astype(BF16))
    o = jnp.concatenate(heads, axis=-1)
    o_ref[...] = x + _dot(o, wo_ref[...])


def _post(x, mixes, w_out, gain, w_q, q_norm, mem_k, mem_v, w_o, seq, *, tm=PROJ_ROWS):
    t, d = x.shape
    b, m, width = mem_k.shape
    n_seq_tiles = seq // tm
    row = lambda w: pl.BlockSpec((tm, w), lambda i: (i, 0))
    mem_spec = pl.BlockSpec((1, m, width), lambda i: (i // n_seq_tiles, 0, 0))
    kern = functools.partial(_post_kernel, n_mix=len(mixes))
    return pl.pallas_call(
        kern,
        grid=(t // tm,),
        in_specs=[row(d)] + [row(a.shape[1]) for a in mixes] + [
            _resident(w_out.shape), _resident((1, d)), _resident((d, width)),
            _resident((1, XM_HEAD_DIM)), mem_spec, mem_spec, _resident((width, d))],
        out_specs=row(d),
        out_shape=jax.ShapeDtypeStruct((t, d), F32),
        compiler_params=_params(1, VMEM_MB["post"]),
        name="post",
    )(x, *mixes, w_out.astype(BF16), gain.reshape(1, d), w_q.astype(BF16),
      q_norm.reshape(1, XM_HEAD_DIM), mem_k, mem_v, w_o.astype(BF16))


def _rope_tables(seq, dim, start, group):
    half = dim // 2
    inv = 1.0 / (ROPE_THETA ** (jnp.arange(0, dim, 2, dtype=F32) / dim))
    ang = jnp.arange(seq, dtype=F32)[:, None] * inv[None, :]
    cos, sin = jnp.cos(ang), jnp.sin(ang)
    zeros = jnp.zeros_like(sin)
    pad = lambda a, fill: jnp.concatenate(
        [jnp.full((seq, start), fill, F32), a, jnp.full((seq, group - start - dim), fill, F32)],
        axis=1)
    cos_t = pad(jnp.concatenate([cos, cos], axis=1), 1.0)
    sin_a = pad(jnp.concatenate([zeros, sin], axis=1), 0.0)
    sin_b = pad(jnp.concatenate([-sin, zeros], axis=1), 0.0)
    reps = LANES // group
    return tuple(jnp.tile(a, (1, reps)) for a in (cos_t, sin_a, sin_b))


def kernel(x, mem, ffn_norm, ffn_w_gate, ffn_w_up, ffn_w_down, mix_norm, ab_w_in, ab_w_out, diff_q_norm, diff_k_norm, diff_lambda_q1, diff_lambda_k1, diff_lambda_q2, diff_lambda_k2, diff_subln, mla_w_dq, mla_q_norm, mla_w_uq, mla_w_dkv, mla_kv_norm, mla_w_ukv, mla_qk_norm_q, mla_qk_norm_k, mla_w_o, xm_norm, xm_mem_norm, xm_w_q, xm_w_kv, xm_q_norm, xm_k_norm, xm_w_o):
    b, s, d = x.shape
    depth = ffn_norm.shape[0]
    x = x.reshape(b * s, d)
    diff_tables = _rope_tables(s, DIFF_HEAD_DIM, 0, DIFF_HEAD_DIM)
    ffn_w = tuple(w.astype(BF16) for w in (ffn_w_gate, ffn_w_up, ffn_w_down))
    for layer in range(depth):
        i = layer // 2
        x = _ffn(x, ffn_norm[layer, 0], *ffn_w, layer, 0)
        if layer % 2 == 0:
            lambda_init = 0.8 - 0.6 * math.exp(-0.3 * layer)
            proj = _ab_proj(x, mix_norm[layer], ab_w_in[i], diff_q_norm[i], diff_k_norm[i],
                            diff_tables, s).reshape(b, s, -1)
            lam_rows = jnp.stack([diff_lambda_q1[i], diff_lambda_k1[i],
                                  diff_lambda_q2[i], diff_lambda_k2[i]])
            bound = _score_bound(diff_q_norm[i], diff_k_norm[i], DIFF_HEAD_DIM)
            oa = _diff_attn(proj, bound, lam_rows, diff_subln[i], lambda_init)
            ob = _sb_attn(proj)
            mixes = [oa.reshape(b * s, -1), ob.reshape(b * s, -1)]
            w_out = ab_w_out[i]
        else:
            q, k, v = _mla_proj(x, mix_norm[layer], mla_w_dq[i], mla_q_norm[i], mla_w_uq[i],
                                mla_w_dkv[i], mla_kv_norm[i], mla_w_ukv[i], mla_qk_norm_q[i],
                                mla_qk_norm_k[i], s)
            bound = _score_bound(mla_qk_norm_q[i], mla_qk_norm_k[i], MLA_QK_DIM)
            o = _mla_attn(q.reshape(b, s, -1), k.reshape(b, s, -1), v.reshape(b, s, -1), bound)
            mixes = [o.reshape(b * s, -1)]
            w_out = mla_w_o[i]
        mem_k, mem_v = _mem_kv(mem, xm_mem_norm[layer], xm_w_kv[layer], xm_k_norm[layer])
        x = _post(x, mixes, w_out, xm_norm[layer], xm_w_q[layer], xm_q_norm[layer],
                  mem_k, mem_v, xm_w_o[layer], s)
        x = _ffn(x, ffn_norm[layer, 1], *ffn_w, layer, 1)
    return x.reshape(b, s, d)
```

```python
import functools
import math

import jax
import jax.numpy as jnp
from jax import lax
from jax.experimental import pallas as pl
from jax.experimental.pallas import tpu as pltpu

F32 = jnp.float32
BF16 = jnp.bfloat16

LANES = 128
V7X_MXU_DIM = 256
V7X_VMEM_LIMIT = 56 * 1024 * 1024

FFN_ROWS = 1024
PROJ_ROWS = 1024
ATTN_TILE = 512
SB_TILE = 256
VMEM_MB = dict(ffn=56, ab_proj=56, diff_attn=48, sb_attn=48, mla_proj=56, mla_attn=48,
               mem_kv=32, post=56)

CHUNK = 64
ROPE_THETA = 10000.0
NORM_EPS = 1e-6
SB_LOG2_ZERO = -151.0
LOG2_E = math.log2(math.e)
MAX_SOFTMAX_SHIFT = 40.0

DIFF_HEADS = 4
DIFF_HEAD_DIM = 64
SB_HEADS = 8
SB_HEAD_DIM = 64
MLA_HEADS = 16
MLA_KV_RANK = 256
MLA_NOPE_DIM = 64
MLA_ROPE_DIM = 32
MLA_V_DIM = 64
MLA_QK_DIM = MLA_NOPE_DIM + MLA_ROPE_DIM
XM_HEADS = 4
XM_HEAD_DIM = 128


def _params(n_grid, vmem_mb):
    return pltpu.CompilerParams(
        dimension_semantics=("arbitrary",) * n_grid,
        vmem_limit_bytes=min(vmem_mb * 1024 * 1024, V7X_VMEM_LIMIT))


def _resident(shape):
    nd = len(shape)
    return pl.BlockSpec(shape, lambda *_: (0,) * nd, pipeline_mode=pl.Buffered(1))


def _rms(x, gain):
    return x * lax.rsqrt(jnp.mean(x * x, axis=-1, keepdims=True) + NORM_EPS) * gain


def _dot(a, b):
    return jnp.dot(a, b, preferred_element_type=F32)


def _dot_nt(a, b):
    return lax.dot_general(a, b, (((1,), (1,)), ((), ())), preferred_element_type=F32)


def _lane_iota(shape):
    return lax.broadcasted_iota(jnp.int32, shape, len(shape) - 1)


def _rope(x, cos, sin_a, sin_b, half):
    return x * cos + pltpu.roll(x, half, 1) * sin_a + pltpu.roll(x, LANES - half, 1) * sin_b


def _ffn_kernel(x_ref, g_ref, wg_ref, wu_ref, wd_ref, o_ref, h_ref, acc_ref, *, tf):
    x = x_ref[...]
    h_ref[...] = _rms(x, g_ref[...]).astype(BF16)
    for j in range(wg_ref.shape[1] // tf):
        cols = slice(j * tf, (j + 1) * tf)
        h = h_ref[...]
        g = _dot(h, wg_ref[:, cols])
        u = _dot(h, wu_ref[:, cols])
        a = (g * jax.nn.sigmoid(g) * u).astype(BF16)
        down = _dot(a, wd_ref[cols, :])
        if j == 0:
            acc_ref[...] = down
        else:
            acc_ref[...] += down
    o_ref[...] = x + 0.5 * acc_ref[...]


def _ffn(x, gain, w_gate, w_up, w_down, layer, half, *, tm=FFN_ROWS, tf=V7X_MXU_DIM):
    t, d = x.shape
    d_ff = w_gate.shape[-1]
    pick = lambda r, c: pl.BlockSpec((None, None, r, c), lambda i: (layer, half, 0, 0),
                                     pipeline_mode=pl.Buffered(1))
    return pl.pallas_call(
        functools.partial(_ffn_kernel, tf=tf),
        grid=(t // tm,),
        in_specs=[
            pl.BlockSpec((tm, d), lambda i: (i, 0)),
            _resident((1, d)),
            pick(d, d_ff), pick(d, d_ff), pick(d_ff, d),
        ],
        out_specs=pl.BlockSpec((tm, d), lambda i: (i, 0)),
        out_shape=jax.ShapeDtypeStruct((t, d), F32),
        scratch_shapes=[pltpu.VMEM((tm, d), BF16), pltpu.VMEM((tm, d), F32)],
        compiler_params=_params(1, VMEM_MB["ffn"]),
        name="ffn",
    )(x, gain.reshape(1, d), w_gate, w_up, w_down)


def _ab_proj_kernel(x_ref, g_ref, w_ref, qn_ref, kn_ref, cos_ref, sa_ref, sb_ref, o_ref, *,
                    n_qk_groups, q_scales):
    h = _rms(x_ref[...], g_ref[...]).astype(BF16)
    cos, sa, sb = cos_ref[...], sa_ref[...], sb_ref[...]
    n_groups = w_ref.shape[1] // LANES
    lane = _lane_iota((x_ref.shape[0], LANES))
    low = lane < DIFF_HEAD_DIM
    for c in range(0, n_groups, 2):
        y2 = _dot(h, w_ref[:, c * LANES:(c + 2) * LANES])
        for k in range(2):
            g = c + k
            y = y2[:, k * LANES:(k + 1) * LANES]
            if g < 2 * n_qk_groups:
                gain = qn_ref[...] if g < n_qk_groups else kn_ref[...]
                sq = y * y
                ss_lo = jnp.sum(jnp.where(low, sq, 0.0), axis=-1, keepdims=True)
                ss_hi = jnp.sum(jnp.where(low, 0.0, sq), axis=-1, keepdims=True)
                ms = jnp.where(low, ss_lo, ss_hi) * (1.0 / DIFF_HEAD_DIM)
                y = y * lax.rsqrt(ms + NORM_EPS) * gain
                y = _rope(y, cos, sa, sb, DIFF_HEAD_DIM // 2)
            if g in q_scales:
                y = y * q_scales[g]
            o_ref[:, g * LANES:(g + 1) * LANES] = y.astype(BF16)


def _ab_proj(x, gain, w_in, q_norm, k_norm, tables, seq, *, tm=PROJ_ROWS):
    t, d = x.shape
    n = w_in.shape[1]
    n_qk_groups = 2 * DIFF_HEADS * DIFF_HEAD_DIM // LANES
    sb_q0 = (4 * DIFF_HEADS * DIFF_HEAD_DIM + DIFF_HEADS * 2 * DIFF_HEAD_DIM) // LANES
    q_scales = {g: DIFF_HEAD_DIM ** -0.5 for g in range(n_qk_groups)}
    q_scales.update({g: SB_HEAD_DIM ** -0.5 * LOG2_E
                     for g in range(sb_q0, sb_q0 + SB_HEADS * SB_HEAD_DIM // LANES)})
    cos, sa, sb = tables
    n_seq_tiles = seq // tm
    tab_spec = pl.BlockSpec((tm, LANES), lambda i: (i % n_seq_tiles, 0))
    kern = functools.partial(_ab_proj_kernel, n_qk_groups=n_qk_groups,
                             q_scales=q_scales)
    return pl.pallas_call(
        kern,
        grid=(t // tm,),
        in_specs=[
            pl.BlockSpec((tm, d), lambda i: (i, 0)),
            _resident((1, d)),
            _resident((d, n)),
            _resident((1, LANES)),
            _resident((1, LANES)),
            tab_spec, tab_spec, tab_spec,
        ],
        out_specs=pl.BlockSpec((tm, n), lambda i: (i, 0)),
        out_shape=jax.ShapeDtypeStruct((t, n), BF16),
        compiler_params=_params(1, VMEM_MB["ab_proj"]),
        name="ab_proj",
    )(x, gain.reshape(1, d), w_in.astype(BF16),
      jnp.tile(q_norm, 2).reshape(1, LANES), jnp.tile(k_norm, 2).reshape(1, LANES),
      cos, sa, sb)


def _chunk_causal_mask(tq, tk):
    qc = lax.broadcasted_iota(jnp.int32, (tq, tk), 0) // CHUNK
    kc = lax.broadcasted_iota(jnp.int32, (tq, tk), 1) // CHUNK
    return kc <= qc


def _fold_lanes(p):
    out = p[:, :LANES]
    for c in range(1, p.shape[1] // LANES):
        out = out + p[:, c * LANES:(c + 1) * LANES]
    return out


def _attend_pair(bound, load_q, k_srcs, v_ref, store_out, acc_refs, l_refs, m_refs, n_q, tq):
    def key_block(h, rows):
        ref, lanes = k_srcs[h]
        return ref[0, rows, lanes]

    @pl.when(bound <= MAX_SOFTMAX_SHIFT)
    def _():
        hq = tq // 2
        lower_rows = lambda a, b: jnp.concatenate([a[:hq], a[hq:] + b], axis=0)
        for qi in range(n_q):
            q_rows = slice(qi * tq, (qi + 1) * tq)
            qs = load_q(q_rows)
            acc, lsum = [None, None], [None, None]
            for kb in range(qi):
                rows = slice(kb * tq, (kb + 1) * tq)
                v = v_ref[0, rows, :]
                for h in range(2):
                    p = jnp.exp(_dot_nt(qs[h], key_block(h, rows)) - bound)
                    part, pv = _fold_lanes(p), _dot(p.astype(BF16), v)
                    lsum[h] = part if kb == 0 else lsum[h] + part
                    acc[h] = pv if kb == 0 else acc[h] + pv
            rows_a = slice(qi * tq, qi * tq + hq)
            rows_b = slice(qi * tq + hq, (qi + 1) * tq)
            v_a, v_b = v_ref[0, rows_a, :], v_ref[0, rows_b, :]
            for h in range(2):
                p_a = jnp.exp(_dot_nt(qs[h], key_block(h, rows_a)) - bound)
                p_a = jnp.where(_chunk_causal_mask(tq, hq), p_a, 0.0)
                p_b = jnp.exp(_dot_nt(qs[h][hq:], key_block(h, rows_b)) - bound)
                p_b = jnp.where(_chunk_causal_mask(hq, hq), p_b, 0.0)
                part = lower_rows(_fold_lanes(p_a), _fold_lanes(p_b))
                pv = lower_rows(_dot(p_a.astype(BF16), v_a), _dot(p_b.astype(BF16), v_b))
                lsum[h] = part if qi == 0 else lsum[h] + part
                acc[h] = pv if qi == 0 else acc[h] + pv
            store_out(q_rows, *(acc[h] / jnp.sum(lsum[h], axis=-1, keepdims=True)
                                for h in range(2)))

    @pl.when(bound > MAX_SOFTMAX_SHIFT)
    def _():
        def q_tile(qi, carry):
            q_rows = pl.ds(pl.multiple_of(qi * tq, tq), tq)
            qs = load_q(q_rows)
            for h in range(2):
                acc_refs[h][...] = jnp.zeros_like(acc_refs[h])
                l_refs[h][...] = jnp.zeros_like(l_refs[h])
                m_refs[h][...] = jnp.full_like(m_refs[h], -jnp.inf)

            def block(kb, masked):
                rows = pl.ds(pl.multiple_of(kb * tq, tq), tq)
                v = v_ref[0, rows, :]
                for h in range(2):
                    s = _dot_nt(qs[h], key_block(h, rows))
                    if masked:
                        s = jnp.where(_chunk_causal_mask(tq, tq), s, -jnp.inf)
                    m_old = m_refs[h][...]
                    m_new = jnp.maximum(m_old, jnp.max(s, axis=-1, keepdims=True))
                    alpha = jnp.exp(m_old - m_new)
                    p = jnp.exp(s - m_new)
                    l_refs[h][...] = alpha * l_refs[h][...] + jnp.sum(p, axis=-1, keepdims=True)
                    acc_refs[h][...] = alpha * acc_refs[h][...] + _dot(p.astype(BF16), v)
                    m_refs[h][...] = m_new

            def body(kb, c):
                block(kb, False)
                return c

            lax.fori_loop(0, qi, body, 0)
            block(qi, True)
            store_out(q_rows, *(acc_refs[h][...] / l_refs[h][...] for h in range(2)))
            return carry

        lax.fori_loop(0, n_q, q_tile, 0)


def _pair_scratch(tq):
    acc = pltpu.VMEM((tq, LANES), F32)
    stat = pltpu.VMEM((tq, 1), F32)
    return [acc, acc, stat, stat, stat, stat]


def _score_bound(q_gain, k_gain, dim):
    return (jnp.max(jnp.abs(q_gain)) * jnp.max(jnp.abs(k_gain)) * math.sqrt(dim)).reshape(1)


def _diff_attn_kernel(bound_ref, q1_ref, q2_ref, k1_ref, k2_ref, v_ref, lam_ref, sub_ref, o_ref,
                      a1_ref, a2_ref, l1_ref, l2_ref, m1_ref, m2_ref, *, tq, lambda_init):
    j = pl.program_id(1)
    mine = (_lane_iota((tq, LANES)) >= DIFF_HEAD_DIM).astype(jnp.int32) == j % 2
    lam_v = lam_ref[...]
    lam = (jnp.exp(jnp.sum(lam_v[0:1] * lam_v[1:2], axis=-1, keepdims=True))
           - jnp.exp(jnp.sum(lam_v[2:3] * lam_v[3:4], axis=-1, keepdims=True)) + lambda_init)

    def load_q(rows):
        return tuple(jnp.where(mine, q_ref[0, rows, :].astype(F32), 0.0).astype(BF16)
                     for q_ref in (q1_ref, q2_ref))

    def store_out(rows, o1, o2):
        o = _rms(o1 - lam * o2, sub_ref[...]) * (1.0 - lambda_init)
        o_ref[0, rows, :] = o.astype(BF16)

    whole = slice(None)
    _attend_pair(bound_ref[0], load_q, ((k1_ref, whole), (k2_ref, whole)), v_ref, store_out,
                 (a1_ref, a2_ref), (l1_ref, l2_ref), (m1_ref, m2_ref), q1_ref.shape[1] // tq, tq)


def _diff_attn(proj, bound, lam_rows, subln, lambda_init, *, tq=ATTN_TILE):
    b, s, _ = proj.shape
    k_off = 2 * DIFF_HEADS * DIFF_HEAD_DIM // LANES
    v_off = 2 * k_off
    half = DIFF_HEADS // 2
    kern = functools.partial(_diff_attn_kernel, tq=tq, lambda_init=lambda_init)
    cols = lambda f: pl.BlockSpec((1, s, LANES), lambda bi, j: (bi, 0, f(j)))
    return pl.pallas_call(
        kern,
        grid=(b, DIFF_HEADS),
        in_specs=[
            pl.BlockSpec(memory_space=pltpu.SMEM),
            cols(lambda j: j // 2),
            cols(lambda j: half + j // 2),
            cols(lambda j: k_off + j // 2),
            cols(lambda j: k_off + half + j // 2),
            cols(lambda j: v_off + j),
            _resident((4, DIFF_HEAD_DIM)),
            _resident((1, LANES)),
        ],
        out_specs=cols(lambda j: j),
        out_shape=jax.ShapeDtypeStruct((b, s, DIFF_HEADS * LANES), BF16),
        scratch_shapes=_pair_scratch(tq),
        compiler_params=_params(2, VMEM_MB["diff_attn"]),
        name="diff_attn",
    )(bound, proj, proj, proj, proj, proj, lam_rows, subln.reshape(1, LANES))


def _split2(x):
    hi = x.astype(BF16)
    return hi, (x - hi.astype(F32)).astype(BF16)


def _sb_log_terms(z):
    sign_bit = jnp.uint32(0x80000000)
    neg_abs = lax.bitcast_convert_type(lax.bitcast_convert_type(z, jnp.uint32) | sign_bit, F32)
    log_beta = jnp.minimum(z, 0.0) - jnp.log2(1.0 + jnp.exp2(neg_abs))
    return log_beta, log_beta - z


def _sb_block(q, k, v, later_than, c_in):
    log_beta, log_keep = _sb_log_terms(_dot_nt(q, k))
    hi, mid = _split2(log_keep)
    later = _dot(hi, later_than) + _dot(mid, later_than)
    w = jnp.exp2(log_beta + later + c_in)
    return _dot(w.astype(BF16), v), jnp.sum(log_keep, axis=-1, keepdims=True)


def _sb_attn_kernel(q_ref, k_ref, v_ref, o_ref, c_ref, acc_ref, *, tq, n_groups):
    qi = pl.program_id(2)
    low = _lane_iota((tq, LANES)) < SB_HEAD_DIM
    q_heads = []
    for g in range(n_groups):
        q = q_ref[0, :, g * LANES:(g + 1) * LANES].astype(F32)
        q_heads += [jnp.where(low, q, 0.0).astype(BF16), jnp.where(low, 0.0, q).astype(BF16)]
    row = lax.broadcasted_iota(jnp.int32, (tq, tq), 0)
    col = lax.broadcasted_iota(jnp.int32, (tq, tq), 1)
    strict = col < row
    later_than = (row > col).astype(BF16)

    def key_rows(kb):
        return pl.ds(pl.multiple_of(kb * tq, tq), tq)

    def group_block(rows, g):
        lanes = slice(g * LANES, (g + 1) * LANES)
        return k_ref[0, rows, lanes], v_ref[0, rows, lanes]

    has_prev = qi > 0
    rows_a, rows_b = key_rows(qi), key_rows(jnp.maximum(qi - 1, 0))
    v_wins, scores = {}, {}

    def score(g):
        (k_a, v_a), (k_b, v_b) = group_block(rows_a, g), group_block(rows_b, g)
        k_win = jnp.concatenate([k_b, k_a], axis=0)
        v_wins[g] = jnp.concatenate([jnp.where(has_prev, v_b, jnp.zeros_like(v_b)), v_a], axis=0)
        scores[g] = [_dot_nt(q_heads[h], k_win) for h in (2 * g, 2 * g + 1)]

    def gate_terms(g):
        log_beta, pieces, sums = [], [], []
        for z in scores[g]:
            lb, lk = _sb_log_terms(z)
            lk = jnp.concatenate([lk[:, :tq], jnp.where(strict, lk[:, tq:], 0.0)], axis=1)
            hi, mid = _split2(lk)
            pieces += [hi[:, :tq], hi[:, tq:], mid[:, :tq], mid[:, tq:]]
            sum_a = jnp.sum(lk[:, tq:], axis=-1, keepdims=True)
            sum_b = jnp.where(has_prev, jnp.sum(lk[:, :tq], axis=-1, keepdims=True), 0.0)
            log_beta.append(lb)
            sums.append((sum_a, sum_b))
        return log_beta, _dot(jnp.concatenate(pieces, axis=0), later_than), sums

    def weights_and_values(g, log_beta, later, sums):
        outs = []
        for i, h in enumerate((2 * g, 2 * g + 1)):
            blk = lambda j: later[(4 * i + j) * tq:(4 * i + j + 1) * tq]
            sum_a, sum_b = sums[i]
            w_b = jnp.exp2(log_beta[i][:, :tq] + (blk(0) + blk(2)) + sum_a)
            w_a = jnp.where(strict, jnp.exp2(log_beta[i][:, tq:] + (blk(1) + blk(3))), 0.0)
            outs.append(_dot(jnp.concatenate([w_b, w_a], axis=1).astype(BF16), v_wins[g]))
            c_ref[:, h:h + 1] = sum_a + sum_b
        acc_ref[:, g * LANES:(g + 1) * LANES] = jnp.where(low, outs[0], outs[1])

    staged = None
    score(0)
    for g in range(n_groups):
        if g + 1 < n_groups:
            score(g + 1)
        current = gate_terms(g)
        if staged is not None:
            weights_and_values(g - 1, *staged)
        staged = current
    weights_and_values(n_groups - 1, *staged)

    def live():
        return jnp.max(c_ref[...]) > SB_LOG2_ZERO

    def cond(state):
        kb, go = state
        return jnp.logical_and(kb >= 0, go)

    def body(state):
        kb, _ = state
        rows = key_rows(kb)
        for g in range(n_groups):
            k, v = group_block(rows, g)
            outs = []
            for h in (2 * g, 2 * g + 1):
                c_old = c_ref[:, h:h + 1]
                pv, row_sum = _sb_block(q_heads[h], k, v, later_than, c_old)
                outs.append(pv)
                c_ref[:, h:h + 1] = c_old + row_sum
            acc_ref[:, g * LANES:(g + 1) * LANES] += jnp.where(low, outs[0], outs[1])
        return kb - 1, live()

    lax.while_loop(cond, body, (qi - 2, live()))
    o_ref[0] = acc_ref[...].astype(BF16)


def _sb_attn(proj, *, tq=SB_TILE, n_groups=SB_HEADS * SB_HEAD_DIM // LANES):
    b, s, _ = proj.shape
    width = n_groups * LANES
    q_off = (4 * DIFF_HEADS * DIFF_HEAD_DIM + DIFF_HEADS * 2 * DIFF_HEAD_DIM) // width
    n_steps = SB_HEADS * SB_HEAD_DIM // width
    kern = functools.partial(_sb_attn_kernel, tq=tq, n_groups=n_groups)
    return pl.pallas_call(
        kern,
        grid=(b, n_steps, s // tq),
        in_specs=[
            pl.BlockSpec((1, tq, width), lambda bi, p, qi: (bi, qi, q_off + p)),
            pl.BlockSpec((1, s, width), lambda bi, p, qi: (bi, 0, q_off + n_steps + p)),
            pl.BlockSpec((1, s, width), lambda bi, p, qi: (bi, 0, q_off + 2 * n_steps + p)),
        ],
        out_specs=pl.BlockSpec((1, tq, width), lambda bi, p, qi: (bi, qi, p)),
        out_shape=jax.ShapeDtypeStruct((b, s, n_steps * width), BF16),
        scratch_shapes=[pltpu.VMEM((tq, 2 * n_groups), F32), pltpu.VMEM((tq, width), F32)],
        compiler_params=_params(3, VMEM_MB["sb_attn"]),
        name="sb_attn",
    )(proj, proj, proj)


def _mla_proj_kernel(x_ref, g_ref, wdq_ref, qln_ref, wuq_ref, wdkv_ref, kvln_ref,
                     wuk_ref, wuv_ref, gkn_ref, qc_ref, qs_ref,
                     kc_ref, ksa_ref, ksb_ref, q_ref, k_ref, v_ref):
    tm = x_ref.shape[0]
    h = _rms(x_ref[...], g_ref[...]).astype(BF16)
    inv_dim = 1.0 / MLA_QK_DIM
    c_q = _rms(_dot(h, wdq_ref[...]), qln_ref[...]).astype(BF16)
    dkv = _dot(h, wdkv_ref[...])
    c_kv = _rms(dkv[:, :MLA_KV_RANK], kvln_ref[...]).astype(BF16)
    kr = dkv[:, MLA_KV_RANK:]
    kr_sq = 0.5 * jnp.sum(kr * kr, axis=-1, keepdims=True)
    half = MLA_ROPE_DIM // 2
    kr_rot = (kr * kc_ref[...] + pltpu.roll(kr, half, 1) * ksa_ref[...]
              + pltpu.roll(kr, LANES - half, 1) * ksb_ref[...])
    v_ref[...] = _dot(c_kv, wuv_ref[...]).astype(BF16)
    low = _lane_iota((tm, LANES)) < MLA_NOPE_DIM
    qc, qs, gkn = qc_ref[...], qs_ref[...], gkn_ref[...]

    def inv_rms(squares, extra=0.0):
        return lax.rsqrt((jnp.sum(squares, axis=-1, keepdims=True) + extra) * inv_dim + NORM_EPS)

    for p2 in range(0, MLA_HEADS // 2, 2):
        kn4 = _dot(c_kv, wuk_ref[:, p2 * LANES:(p2 + 2) * LANES])
        for i in range(2):
            p = p2 + i
            cols = slice(p * 2 * LANES, (p + 1) * 2 * LANES)
            q4 = _dot(c_q, wuq_ref[:, 2 * p * 2 * LANES:2 * (p + 1) * 2 * LANES])
            q2, q2s = q4[:, :2 * LANES], q4[:, 2 * LANES:]
            q_sq = q2 * q2
            rq = jnp.concatenate([jnp.broadcast_to(inv_rms(q_sq[:, :LANES]), (tm, LANES)),
                                  jnp.broadcast_to(inv_rms(q_sq[:, LANES:]), (tm, LANES))], axis=1)
            q_ref[:, cols] = ((q2 * qc + q2s * qs) * rq).astype(BF16)
            kn2 = kn4[:, i * LANES:(i + 1) * LANES]
            k_sq = kn2 * kn2
            kg = kn2 * gkn
            k_ref[:, p * 2 * LANES:p * 2 * LANES + LANES] = (
                jnp.where(low, kg, kr_rot) * inv_rms(jnp.where(low, k_sq, 0.0), kr_sq)).astype(BF16)
            k_ref[:, p * 2 * LANES + LANES:(p + 1) * 2 * LANES] = (
                jnp.where(low, kr_rot, kg) * inv_rms(jnp.where(low, 0.0, k_sq), kr_sq)).astype(BF16)


def _mla_pair_layout(even, odd_first, odd_last, n_rows):
    z = lambda w: jnp.zeros((n_rows, w), F32)
    return jnp.concatenate([even, z(LANES - MLA_QK_DIM), odd_first, z(MLA_ROPE_DIM), odd_last], axis=1)


def _mla_proj(x, gain, w_dq, q_lat_norm, w_uq, w_dkv, kv_lat_norm, w_ukv, qk_norm_q, qk_norm_k,
              seq, *, tm=PROJ_ROWS):
    t, d = x.shape
    q_rank = w_dq.shape[1]
    half = MLA_ROPE_DIM // 2
    n_pairs = MLA_HEADS // 2
    swap = lambda a: jnp.concatenate([a[..., half:], a[..., :half]], axis=-1)

    uq = w_uq.reshape(q_rank, MLA_HEADS, MLA_QK_DIM)
    nope, rope = uq[:, :, :MLA_NOPE_DIM], uq[:, :, MLA_NOPE_DIM:]
    rope_sw = jnp.concatenate([-rope[..., half:], rope[..., :half]], axis=-1)
    zeros = jnp.zeros_like
    ev_n, ev_r, od_r, od_n = nope[:, 0::2], rope[:, 0::2], rope[:, 1::2], nope[:, 1::2]
    wuq = jnp.concatenate(
        [ev_n, ev_r, zeros(ev_r), od_r, zeros(od_r), od_n,
         zeros(ev_n), rope_sw[:, 0::2], zeros(ev_r), rope_sw[:, 1::2], zeros(od_r), zeros(od_n)],
        axis=-1).reshape(q_rank, -1).astype(BF16)

    ukv = w_ukv.reshape(MLA_KV_RANK, MLA_HEADS, MLA_NOPE_DIM + MLA_V_DIM)
    wuk = ukv[:, :, :MLA_NOPE_DIM].reshape(MLA_KV_RANK, -1).astype(BF16)
    wuv = ukv[:, :, MLA_NOPE_DIM:].reshape(MLA_KV_RANK, -1).astype(BF16)
    kr_w = w_dkv[:, MLA_KV_RANK:]
    kr_pad = jnp.zeros((d, MLA_ROPE_DIM), F32)
    wdkv = jnp.concatenate([w_dkv[:, :MLA_KV_RANK], kr_w, kr_pad, kr_w, kr_pad], axis=1).astype(BF16)

    inv = 1.0 / (ROPE_THETA ** (jnp.arange(0, MLA_ROPE_DIM, 2, dtype=F32) / MLA_ROPE_DIM))
    ang = jnp.arange(seq, dtype=F32)[:, None] * inv[None, :]
    cos, sin = jnp.tile(jnp.cos(ang), (1, 2)), jnp.tile(jnp.sin(ang), (1, 2))
    scale = MLA_QK_DIM ** -0.5
    gq_n, gq_r = qk_norm_q[:MLA_NOPE_DIM] * scale, qk_norm_q[MLA_NOPE_DIM:] * scale
    gk_n, gk_r = qk_norm_k[:MLA_NOPE_DIM], qk_norm_k[MLA_NOPE_DIM:]
    bq_n = jnp.broadcast_to(gq_n, (seq, MLA_NOPE_DIM))
    qc = _mla_pair_layout(jnp.concatenate([bq_n, cos * gq_r], axis=1), cos * gq_r, bq_n, seq)
    qs = _mla_pair_layout(jnp.concatenate([zeros(bq_n), sin * swap(gq_r)], axis=1),
                          sin * swap(gq_r), zeros(bq_n), seq)
    first = jnp.arange(MLA_ROPE_DIM) < half
    k_cos = cos * gk_r
    k_sa = jnp.where(first, 0.0, sin * swap(gk_r))
    k_sb = jnp.where(first, -sin * swap(gk_r), 0.0)
    twice = lambda a: jnp.concatenate([a, zeros(a), a, zeros(a)], axis=1)
    n_seq_tiles = seq // tm
    tab = lambda w: pl.BlockSpec((tm, w), lambda i: (i % n_seq_tiles, 0))
    hw = MLA_HEADS * LANES
    vw = MLA_HEADS * MLA_V_DIM
    row = lambda w: pl.BlockSpec((tm, w), lambda i: (i, 0))
    return pl.pallas_call(
        _mla_proj_kernel,
        grid=(t // tm,),
        in_specs=[
            row(d), _resident((1, d)),
            _resident((d, q_rank)), _resident((1, q_rank)), _resident((q_rank, 2 * hw)),
            _resident((d, MLA_KV_RANK + LANES)), _resident((1, MLA_KV_RANK)),
            _resident((MLA_KV_RANK, n_pairs * LANES)), _resident((MLA_KV_RANK, vw)),
            _resident((1, LANES)),
            tab(2 * LANES), tab(2 * LANES), tab(LANES), tab(LANES), tab(LANES),
        ],
        out_specs=[row(hw), row(hw), row(vw)],
        out_shape=[jax.ShapeDtypeStruct((t, hw), BF16), jax.ShapeDtypeStruct((t, hw), BF16),
                   jax.ShapeDtypeStruct((t, vw), BF16)],
        compiler_params=_params(1, VMEM_MB["mla_proj"]),
        name="mla_proj",
    )(x, gain.reshape(1, d), w_dq.astype(BF16), q_lat_norm.reshape(1, q_rank), wuq, wdkv,
      kv_lat_norm.reshape(1, MLA_KV_RANK), wuk, wuv,
      jnp.tile(gk_n, 2).reshape(1, LANES), qc, qs, twice(k_cos), twice(k_sa), twice(k_sb))


def _mla_attn_kernel(bound_ref, q_ref, k_ref, v_ref, o_ref,
                     a0_ref, a1_ref, l0_ref, l1_ref, m0_ref, m1_ref, *, tq):
    low = _lane_iota((tq, LANES)) < MLA_V_DIM
    lanes = (slice(0, LANES), slice(LANES, 2 * LANES))

    def load_q(rows):
        return tuple(q_ref[0, rows, ln] for ln in lanes)

    def store_out(rows, o0, o1):
        o_ref[0, rows, :] = jnp.where(low, o0, o1).astype(BF16)

    _attend_pair(bound_ref[0], load_q, ((k_ref, lanes[0]), (k_ref, lanes[1])), v_ref, store_out,
                 (a0_ref, a1_ref), (l0_ref, l1_ref), (m0_ref, m1_ref), q_ref.shape[1] // tq, tq)


def _mla_attn(q, k, v, bound, *, tq=ATTN_TILE):
    b, s, _ = q.shape
    n_pairs = MLA_HEADS // 2
    kern = functools.partial(_mla_attn_kernel, tq=tq)
    return pl.pallas_call(
        kern,
        grid=(b, n_pairs),
        in_specs=[
            pl.BlockSpec(memory_space=pltpu.SMEM),
            pl.BlockSpec((1, s, 2 * LANES), lambda bi, p: (bi, 0, p)),
            pl.BlockSpec((1, s, 2 * LANES), lambda bi, p: (bi, 0, p)),
            pl.BlockSpec((1, s, LANES), lambda bi, p: (bi, 0, p)),
        ],
        out_specs=pl.BlockSpec((1, s, LANES), lambda bi, p: (bi, 0, p)),
        out_shape=jax.ShapeDtypeStruct((b, s, n_pairs * LANES), BF16),
        scratch_shapes=_pair_scratch(tq),
        compiler_params=_params(2, VMEM_MB["mla_attn"]),
        name="mla_attn",
    )(bound, q, k, v)


def _mem_kv_kernel(m_ref, g_ref, w_ref, kn_ref, k_ref, v_ref):
    h = _rms(m_ref[0], g_ref[...]).astype(BF16)
    kv = _dot(h, w_ref[...])
    width = XM_HEADS * XM_HEAD_DIM
    for hd in range(XM_HEADS):
        lanes = slice(hd * LANES, (hd + 1) * LANES)
        k_ref[0, :, lanes] = _rms(kv[:, lanes], kn_ref[...]).astype(BF16)
    v_ref[0] = kv[:, width:].astype(BF16)


def _mem_kv(mem, gain, w_kv, k_norm):
    b, m, d = mem.shape
    width = XM_HEADS * XM_HEAD_DIM
    out = jax.ShapeDtypeStruct((b, m, width), BF16)
    blk = pl.BlockSpec((1, m, width), lambda i: (i, 0, 0))
    return pl.pallas_call(
        _mem_kv_kernel,
        grid=(b,),
        in_specs=[pl.BlockSpec((1, m, d), lambda i: (i, 0, 0)), _resident((1, d)),
                  _resident((d, 2 * width)), _resident((1, XM_HEAD_DIM))],
        out_specs=[blk, blk],
        out_shape=[out, out],
        compiler_params=_params(1, VMEM_MB["mem_kv"]),
        name="mem_kv",
    )(mem, gain.reshape(1, d), w_kv.astype(BF16), k_norm.reshape(1, XM_HEAD_DIM))


def _post_kernel(*refs, n_mix):
    x_ref = refs[0]
    mix_refs = refs[1:1 + n_mix]
    wout_ref, g_ref, wq_ref, qn_ref, k_ref, v_ref, wo_ref, o_ref = refs[1 + n_mix:]
    x = x_ref[...]
    off = 0
    for a_ref in mix_refs:
        w = a_ref.shape[1]
        x = x + _dot(a_ref[...], wout_ref[off:off + w, :])
        off += w
    h = _rms(x, g_ref[...]).astype(BF16)
    q = _dot(h, wq_ref[...])
    lanes = [slice(hd * LANES, (hd + 1) * LANES) for hd in range(XM_HEADS)]
    scores = []
    for hd in range(XM_HEADS):
        qh = (_rms(q[:, lanes[hd]], qn_ref[...]) * (XM_HEAD_DIM ** -0.5)).astype(BF16)
        scores.append(_dot_nt(qh, k_ref[0, :, lanes[hd]]))

    def softmax(s):
        p = jnp.exp(s - jnp.max(s, axis=-1, keepdims=True))
        return p.astype(BF16), jnp.sum(p, axis=-1, keepdims=True)

    heads, staged = [], None
    for hd in range(XM_HEADS):
        current = softmax(scores[hd])
        if staged is not None:
            heads.append((_dot(staged[0], v_ref[0, :, lanes[hd - 1]]) / staged[1]).astype(BF16))
        staged = current
    heads.append((_dot(staged[0], v_ref[0, :, lanes[XM_HEADS - 1]]) / staged[1]).astype(BF16))
    o = jnp.concatenate(heads, axis=-1)
    o_ref[...] = x + _dot(o, wo_ref[...])


def _post(x, mixes, w_out, gain, w_q, q_norm, mem_k, mem_v, w_o, seq, *, tm=PROJ_ROWS):
    t, d = x.shape
    b, m, width = mem_k.shape
    n_seq_tiles = seq // tm
    row = lambda w: pl.BlockSpec((tm, w), lambda i: (i, 0))
    mem_spec = pl.BlockSpec((1, m, width), lambda i: (i // n_seq_tiles, 0, 0))
    kern = functools.partial(_post_kernel, n_mix=len(mixes))
    return pl.pallas_call(
        kern,
        grid=(t // tm,),
        in_specs=[row(d)] + [row(a.shape[1]) for a in mixes] + [
            _resident(w_out.shape), _resident((1, d)), _resident((d, width)),
            _resident((1, XM_HEAD_DIM)), mem_spec, mem_spec, _resident((width, d))],
        out_specs=row(d),
        out_shape=jax.ShapeDtypeStruct((t, d), F32),
        compiler_params=_params(1, VMEM_MB["post"]),
        name="post",
    )(x, *mixes, w_out.astype(BF16), gain.reshape(1, d), w_q.astype(BF16),
      q_norm.reshape(1, XM_HEAD_DIM), mem_k, mem_v, w_o.astype(BF16))


def _rope_tables(seq, dim, start, group):
    half = dim // 2
    inv = 1.0 / (ROPE_THETA ** (jnp.arange(0, dim, 2, dtype=F32) / dim))
    ang = jnp.arange(seq, dtype=F32)[:, None] * inv[None, :]
    cos, sin = jnp.cos(ang), jnp.sin(ang)
    zeros = jnp.zeros_like(sin)
    pad = lambda a, fill: jnp.concatenate(
        [jnp.full((seq, start), fill, F32), a, jnp.full((seq, group - start - dim), fill, F32)],
        axis=1)
    cos_t = pad(jnp.concatenate([cos, cos], axis=1), 1.0)
    sin_a = pad(jnp.concatenate([zeros, sin], axis=1), 0.0)
    sin_b = pad(jnp.concatenate([-sin, zeros], axis=1), 0.0)
    reps = LANES // group
    return tuple(jnp.tile(a, (1, reps)) for a in (cos_t, sin_a, sin_b))


def kernel(x, mem, ffn_norm, ffn_w_gate, ffn_w_up, ffn_w_down, mix_norm, ab_w_in, ab_w_out, diff_q_norm, diff_k_norm, diff_lambda_q1, diff_lambda_k1, diff_lambda_q2, diff_lambda_k2, diff_subln, mla_w_dq, mla_q_norm, mla_w_uq, mla_w_dkv, mla_kv_norm, mla_w_ukv, mla_qk_norm_q, mla_qk_norm_k, mla_w_o, xm_norm, xm_mem_norm, xm_w_q, xm_w_kv, xm_q_norm, xm_k_norm, xm_w_o):
    b, s, d = x.shape
    depth = ffn_norm.shape[0]
    x = x.reshape(b * s, d)
    diff_tables = _rope_tables(s, DIFF_HEAD_DIM, 0, DIFF_HEAD_DIM)
    ffn_w = tuple(w.astype(BF16) for w in (ffn_w_gate, ffn_w_up, ffn_w_down))
    for layer in range(depth):
        i = layer // 2
        x = _ffn(x, ffn_norm[layer, 0], *ffn_w, layer, 0)
        if layer % 2 == 0:
            lambda_init = 0.8 - 0.6 * math.exp(-0.3 * layer)
            proj = _ab_proj(x, mix_norm[layer], ab_w_in[i], diff_q_norm[i], diff_k_norm[i],
                            diff_tables, s).reshape(b, s, -1)
            lam_rows = jnp.stack([diff_lambda_q1[i], diff_lambda_k1[i],
                                  diff_lambda_q2[i], diff_lambda_k2[i]])
            bound = _score_bound(diff_q_norm[i], diff_k_norm[i], DIFF_HEAD_DIM)
            oa = _diff_attn(proj, bound, lam_rows, diff_subln[i], lambda_init)
            ob = _sb_attn(proj)
            mixes = [oa.reshape(b * s, -1), ob.reshape(b * s, -1)]
            w_out = ab_w_out[i]
        else:
            q, k, v = _mla_proj(x, mix_norm[layer], mla_w_dq[i], mla_q_norm[i], mla_w_uq[i],
                                mla_w_dkv[i], mla_kv_norm[i], mla_w_ukv[i], mla_qk_norm_q[i],
                                mla_qk_norm_k[i], s)
            bound = _score_bound(mla_qk_norm_q[i], mla_qk_norm_k[i], MLA_QK_DIM)
            o = _mla_attn(q.reshape(b, s, -1), k.reshape(b, s, -1), v.reshape(b, s, -1), bound)
            mixes = [o.reshape(b * s, -1)]
            w_out = mla_w_o[i]
        mem_k, mem_v = _mem_kv(mem, xm_mem_norm[layer], xm_w_kv[layer], xm_k_norm[layer])
        x = _post(x, mixes, w_out, xm_norm[layer], xm_w_q[layer], xm_q_norm[layer],
                  mem_k, mem_v, xm_w_o[layer], s)
        x = _ffn(x, ffn_norm[layer, 1], *ffn_w, layer, 1)
    return x.reshape(b, s, d)
```

```python
import functools
import math

import jax
import jax.numpy as jnp
from jax import lax
from jax.experimental import pallas as pl
from jax.experimental.pallas import tpu as pltpu

F32 = jnp.float32
BF16 = jnp.bfloat16

LANES = 128
V7X_MXU_DIM = 256
V7X_VMEM_LIMIT = 56 * 1024 * 1024

FFN_ROWS = 1024
PROJ_ROWS = 1024
ATTN_TILE = 512
SB_TILE = 256
VMEM_MB = dict(ffn=56, ab_proj=56, diff_attn=56, sb_attn=48, mla_proj=56, mla_attn=48,
               mem_kv=32, post=56)

CHUNK = 64
ROPE_THETA = 10000.0
NORM_EPS = 1e-6
SB_LOG2_ZERO = -151.0
LOG2_E = math.log2(math.e)
MAX_SOFTMAX_SHIFT = 40.0

DIFF_HEADS = 4
DIFF_HEAD_DIM = 64
SB_HEADS = 8
SB_HEAD_DIM = 64
MLA_HEADS = 16
MLA_KV_RANK = 256
MLA_NOPE_DIM = 64
MLA_ROPE_DIM = 32
MLA_V_DIM = 64
MLA_QK_DIM = MLA_NOPE_DIM + MLA_ROPE_DIM
XM_HEADS = 4
XM_HEAD_DIM = 128


def _params(n_grid, vmem_mb):
    return pltpu.CompilerParams(
        dimension_semantics=("arbitrary",) * n_grid,
        vmem_limit_bytes=min(vmem_mb * 1024 * 1024, V7X_VMEM_LIMIT))


def _resident(shape):
    nd = len(shape)
    return pl.BlockSpec(shape, lambda *_: (0,) * nd, pipeline_mode=pl.Buffered(1))


def _rms(x, gain):
    return x * lax.rsqrt(jnp.mean(x * x, axis=-1, keepdims=True) + NORM_EPS) * gain


def _dot(a, b):
    return jnp.dot(a, b, preferred_element_type=F32)


def _dot_nt(a, b):
    return lax.dot_general(a, b, (((1,), (1,)), ((), ())), preferred_element_type=F32)


def _lane_iota(shape):
    return lax.broadcasted_iota(jnp.int32, shape, len(shape) - 1)


def _rope(x, cos, sin_a, sin_b, half):
    return x * cos + pltpu.roll(x, half, 1) * sin_a + pltpu.roll(x, LANES - half, 1) * sin_b


def _ffn_kernel(x_ref, g_ref, wg_ref, wu_ref, wd_ref, o_ref, h_ref, acc_ref, *, tf):
    x = x_ref[...]
    h_ref[...] = _rms(x, g_ref[...]).astype(BF16)
    for j in range(wg_ref.shape[1] // tf):
        cols = slice(j * tf, (j + 1) * tf)
        h = h_ref[...]
        g = _dot(h, wg_ref[:, cols])
        u = _dot(h, wu_ref[:, cols])
        a = (g * jax.nn.sigmoid(g) * u).astype(BF16)
        down = _dot(a, wd_ref[cols, :])
        if j == 0:
            acc_ref[...] = down
        else:
            acc_ref[...] += down
    o_ref[...] = x + 0.5 * acc_ref[...]


def _ffn(x, gain, w_gate, w_up, w_down, layer, half, *, tm=FFN_ROWS, tf=V7X_MXU_DIM):
    t, d = x.shape
    d_ff = w_gate.shape[-1]
    pick = lambda r, c: pl.BlockSpec((None, None, r, c), lambda i: (layer, half, 0, 0),
                                     pipeline_mode=pl.Buffered(1))
    return pl.pallas_call(
        functools.partial(_ffn_kernel, tf=tf),
        grid=(t // tm,),
        in_specs=[
            pl.BlockSpec((tm, d), lambda i: (i, 0)),
            _resident((1, d)),
            pick(d, d_ff), pick(d, d_ff), pick(d_ff, d),
        ],
        out_specs=pl.BlockSpec((tm, d), lambda i: (i, 0)),
        out_shape=jax.ShapeDtypeStruct((t, d), F32),
        scratch_shapes=[pltpu.VMEM((tm, d), BF16), pltpu.VMEM((tm, d), F32)],
        compiler_params=_params(1, VMEM_MB["ffn"]),
        name="ffn",
    )(x, gain.reshape(1, d), w_gate, w_up, w_down)


def _ab_proj_kernel(x_ref, g_ref, w_ref, qn_ref, kn_ref, cos_ref, sa_ref, sb_ref, o_ref, *,
                    n_qk_groups, q_scales):
    h = _rms(x_ref[...], g_ref[...]).astype(BF16)
    cos, sa, sb = cos_ref[...], sa_ref[...], sb_ref[...]
    n_groups = w_ref.shape[1] // LANES
    lane = _lane_iota((x_ref.shape[0], LANES))
    low = lane < DIFF_HEAD_DIM
    for c in range(0, n_groups, 2):
        y2 = _dot(h, w_ref[:, c * LANES:(c + 2) * LANES])
        for k in range(2):
            g = c + k
            y = y2[:, k * LANES:(k + 1) * LANES]
            if g < 2 * n_qk_groups:
                gain = qn_ref[...] if g < n_qk_groups else kn_ref[...]
                sq = y * y
                ss_lo = jnp.sum(jnp.where(low, sq, 0.0), axis=-1, keepdims=True)
                ss_hi = jnp.sum(jnp.where(low, 0.0, sq), axis=-1, keepdims=True)
                ms = jnp.where(low, ss_lo, ss_hi) * (1.0 / DIFF_HEAD_DIM)
                y = y * lax.rsqrt(ms + NORM_EPS) * gain
                y = _rope(y, cos, sa, sb, DIFF_HEAD_DIM // 2)
            if g in q_scales:
                y = y * q_scales[g]
            o_ref[:, g * LANES:(g + 1) * LANES] = y.astype(BF16)


def _ab_proj(x, gain, w_in, q_norm, k_norm, tables, seq, *, tm=PROJ_ROWS):
    t, d = x.shape
    n = w_in.shape[1]
    n_qk_groups = 2 * DIFF_HEADS * DIFF_HEAD_DIM // LANES
    sb_q0 = (4 * DIFF_HEADS * DIFF_HEAD_DIM + DIFF_HEADS * 2 * DIFF_HEAD_DIM) // LANES
    q_scales = {g: DIFF_HEAD_DIM ** -0.5 for g in range(n_qk_groups)}
    q_scales.update({g: SB_HEAD_DIM ** -0.5 * LOG2_E
                     for g in range(sb_q0, sb_q0 + SB_HEADS * SB_HEAD_DIM // LANES)})
    cos, sa, sb = tables
    n_seq_tiles = seq // tm
    tab_spec = pl.BlockSpec((tm, LANES), lambda i: (i % n_seq_tiles, 0))
    kern = functools.partial(_ab_proj_kernel, n_qk_groups=n_qk_groups,
                             q_scales=q_scales)
    return pl.pallas_call(
        kern,
        grid=(t // tm,),
        in_specs=[
            pl.BlockSpec((tm, d), lambda i: (i, 0)),
            _resident((1, d)),
            _resident((d, n)),
            _resident((1, LANES)),
            _resident((1, LANES)),
            tab_spec, tab_spec, tab_spec,
        ],
        out_specs=pl.BlockSpec((tm, n), lambda i: (i, 0)),
        out_shape=jax.ShapeDtypeStruct((t, n), BF16),
        compiler_params=_params(1, VMEM_MB["ab_proj"]),
        name="ab_proj",
    )(x, gain.reshape(1, d), w_in.astype(BF16),
      jnp.tile(q_norm, 2).reshape(1, LANES), jnp.tile(k_norm, 2).reshape(1, LANES),
      cos, sa, sb)


def _chunk_causal_mask(tq, tk):
    qc = lax.broadcasted_iota(jnp.int32, (tq, tk), 0) // CHUNK
    kc = lax.broadcasted_iota(jnp.int32, (tq, tk), 1) // CHUNK
    return kc <= qc


def _fold_lanes(p):
    out = p[:, :LANES]
    for c in range(1, p.shape[1] // LANES):
        out = out + p[:, c * LANES:(c + 1) * LANES]
    return out


def _attend_pair(bound, load_q, k_srcs, v_ref, store_out, acc_refs, l_refs, m_refs, n_q, tq,
                 side_work=None):
    def key_block(h, rows):
        ref, lanes = k_srcs[h]
        return ref[0, rows, lanes]

    @pl.when(bound <= MAX_SOFTMAX_SHIFT)
    def _():
        hq = tq // 2
        lower_rows = lambda a, b: jnp.concatenate([a[:hq], a[hq:] + b], axis=0)
        for qi in range(n_q):
            q_rows = slice(qi * tq, (qi + 1) * tq)
            qs = load_q(q_rows)
            if side_work is not None:
                side_work(qi, n_q)
            acc, lsum = [None, None], [None, None]
            for kb in range(qi):
                rows = slice(kb * tq, (kb + 1) * tq)
                v = v_ref[0, rows, :]
                for h in range(2):
                    p = jnp.exp(_dot_nt(qs[h], key_block(h, rows)) - bound)
                    part, pv = _fold_lanes(p), _dot(p.astype(BF16), v)
                    lsum[h] = part if kb == 0 else lsum[h] + part
                    acc[h] = pv if kb == 0 else acc[h] + pv
            rows_a = slice(qi * tq, qi * tq + hq)
            rows_b = slice(qi * tq + hq, (qi + 1) * tq)
            v_a, v_b = v_ref[0, rows_a, :], v_ref[0, rows_b, :]
            for h in range(2):
                p_a = jnp.exp(_dot_nt(qs[h], key_block(h, rows_a)) - bound)
                p_a = jnp.where(_chunk_causal_mask(tq, hq), p_a, 0.0)
                p_b = jnp.exp(_dot_nt(qs[h][hq:], key_block(h, rows_b)) - bound)
                p_b = jnp.where(_chunk_causal_mask(hq, hq), p_b, 0.0)
                part = lower_rows(_fold_lanes(p_a), _fold_lanes(p_b))
                pv = lower_rows(_dot(p_a.astype(BF16), v_a), _dot(p_b.astype(BF16), v_b))
                lsum[h] = part if qi == 0 else lsum[h] + part
                acc[h] = pv if qi == 0 else acc[h] + pv
            store_out(q_rows, *(acc[h] / jnp.sum(lsum[h], axis=-1, keepdims=True)
                                for h in range(2)))

    @pl.when(bound > MAX_SOFTMAX_SHIFT)
    def _():
        if side_work is not None:
            for i in range(n_q):
                side_work(i, n_q)

        def q_tile(qi, carry):
            q_rows = pl.ds(pl.multiple_of(qi * tq, tq), tq)
            qs = load_q(q_rows)
            for h in range(2):
                acc_refs[h][...] = jnp.zeros_like(acc_refs[h])
                l_refs[h][...] = jnp.zeros_like(l_refs[h])
                m_refs[h][...] = jnp.full_like(m_refs[h], -jnp.inf)

            def block(kb, masked):
                rows = pl.ds(pl.multiple_of(kb * tq, tq), tq)
                v = v_ref[0, rows, :]
                for h in range(2):
                    s = _dot_nt(qs[h], key_block(h, rows))
                    if masked:
                        s = jnp.where(_chunk_causal_mask(tq, tq), s, -jnp.inf)
                    m_old = m_refs[h][...]
                    m_new = jnp.maximum(m_old, jnp.max(s, axis=-1, keepdims=True))
                    alpha = jnp.exp(m_old - m_new)
                    p = jnp.exp(s - m_new)
                    l_refs[h][...] = alpha * l_refs[h][...] + jnp.sum(p, axis=-1, keepdims=True)
                    acc_refs[h][...] = alpha * acc_refs[h][...] + _dot(p.astype(BF16), v)
                    m_refs[h][...] = m_new

            def body(kb, c):
                block(kb, False)
                return c

            lax.fori_loop(0, qi, body, 0)
            block(qi, True)
            store_out(q_rows, *(acc_refs[h][...] / l_refs[h][...] for h in range(2)))
            return carry

        lax.fori_loop(0, n_q, q_tile, 0)


def _pair_scratch(tq):
    acc = pltpu.VMEM((tq, LANES), F32)
    stat = pltpu.VMEM((tq, 1), F32)
    return [acc, acc, stat, stat, stat, stat]


def _score_bound(q_gain, k_gain, dim):
    return (jnp.max(jnp.abs(q_gain)) * jnp.max(jnp.abs(k_gain)) * math.sqrt(dim)).reshape(1)


def _diff_attn_kernel(bound_ref, q1_ref, q2_ref, k1_ref, k2_ref, v_ref, lam_ref, sub_ref,
                      wg_ref, wu_ref, wd_ref, o_ref, wg16_ref, wu16_ref, wd16_ref,
                      a1_ref, a2_ref, l1_ref, l2_ref, m1_ref, m2_ref, *, tq, lambda_init):
    j = pl.program_id(1)

    def cast_ffn_weights(i, n):
        for src, dst in ((wg_ref, wg16_ref), (wu_ref, wu16_ref)):
            r = src.shape[0] // n
            dst[i * r:(i + 1) * r, :] = src[i * r:(i + 1) * r, :].astype(BF16)
        c = wd_ref.shape[1] // n
        wd16_ref[:, i * c:(i + 1) * c] = wd_ref[:, i * c:(i + 1) * c].astype(BF16)

    mine = (_lane_iota((tq, LANES)) >= DIFF_HEAD_DIM).astype(jnp.int32) == j % 2
    lam_v = lam_ref[...]
    lam = (jnp.exp(jnp.sum(lam_v[0:1] * lam_v[1:2], axis=-1, keepdims=True))
           - jnp.exp(jnp.sum(lam_v[2:3] * lam_v[3:4], axis=-1, keepdims=True)) + lambda_init)

    def load_q(rows):
        return tuple(jnp.where(mine, q_ref[0, rows, :].astype(F32), 0.0).astype(BF16)
                     for q_ref in (q1_ref, q2_ref))

    def store_out(rows, o1, o2):
        o = _rms(o1 - lam * o2, sub_ref[...]) * (1.0 - lambda_init)
        o_ref[0, rows, :] = o.astype(BF16)

    whole = slice(None)
    _attend_pair(bound_ref[0], load_q, ((k1_ref, whole), (k2_ref, whole)), v_ref, store_out,
                 (a1_ref, a2_ref), (l1_ref, l2_ref), (m1_ref, m2_ref), q1_ref.shape[1] // tq, tq,
                 side_work=cast_ffn_weights)


def _diff_attn(proj, bound, lam_rows, subln, lambda_init, ffn_w, *, tq=ATTN_TILE):
    b, s, _ = proj.shape
    k_off = 2 * DIFF_HEADS * DIFF_HEAD_DIM // LANES
    v_off = 2 * k_off
    half = DIFF_HEADS // 2
    kern = functools.partial(_diff_attn_kernel, tq=tq, lambda_init=lambda_init)
    cols = lambda f: pl.BlockSpec((1, s, LANES), lambda bi, j: (bi, 0, f(j)))
    n_steps = b * DIFF_HEADS
    flat = [w.reshape(-1, w.shape[-1]) for w in ffn_w]
    slabs = [pl.BlockSpec((w.shape[0] // n_steps, w.shape[1]),
                          lambda bi, j: (bi * DIFF_HEADS + j, 0)) for w in flat]
    assert all(w.shape[0] % n_steps == 0 for w in flat)
    outs = pl.pallas_call(
        kern,
        grid=(b, DIFF_HEADS),
        in_specs=[
            pl.BlockSpec(memory_space=pltpu.SMEM),
            cols(lambda j: j // 2),
            cols(lambda j: half + j // 2),
            cols(lambda j: k_off + j // 2),
            cols(lambda j: k_off + half + j // 2),
            cols(lambda j: v_off + j),
            _resident((4, DIFF_HEAD_DIM)),
            _resident((1, LANES)),
            *slabs,
        ],
        out_specs=[cols(lambda j: j), *slabs],
        out_shape=[jax.ShapeDtypeStruct((b, s, DIFF_HEADS * LANES), BF16)]
        + [jax.ShapeDtypeStruct(w.shape, BF16) for w in flat],
        scratch_shapes=_pair_scratch(tq),
        compiler_params=_params(2, VMEM_MB["diff_attn"]),
        name="diff_attn",
    )(bound, proj, proj, proj, proj, proj, lam_rows, subln.reshape(1, LANES), *flat)
    return outs[0], tuple(o.reshape(w.shape) for o, w in zip(outs[1:], ffn_w))


def _split2(x):
    hi = x.astype(BF16)
    return hi, (x - hi.astype(F32)).astype(BF16)


def _sb_log_terms(z):
    sign_bit = jnp.uint32(0x80000000)
    neg_abs = lax.bitcast_convert_type(lax.bitcast_convert_type(z, jnp.uint32) | sign_bit, F32)
    log_beta = jnp.minimum(z, 0.0) - jnp.log2(1.0 + jnp.exp2(neg_abs))
    return log_beta, log_beta - z


def _sb_block(q, k, v, later_than, c_in):
    log_beta, log_keep = _sb_log_terms(_dot_nt(q, k))
    hi, mid = _split2(log_keep)
    later = _dot(hi, later_than) + _dot(mid, later_than)
    w = jnp.exp2(log_beta + later + c_in)
    return _dot(w.astype(BF16), v), jnp.sum(log_keep, axis=-1, keepdims=True)


def _sb_attn_kernel(q_ref, k_ref, v_ref, o_ref, c_ref, acc_ref, *, tq, n_groups):
    qi = pl.program_id(2)
    low = _lane_iota((tq, LANES)) < SB_HEAD_DIM
    q_heads = []
    for g in range(n_groups):
        q = q_ref[0, :, g * LANES:(g + 1) * LANES].astype(F32)
        q_heads += [jnp.where(low, q, 0.0).astype(BF16), jnp.where(low, 0.0, q).astype(BF16)]
    row = lax.broadcasted_iota(jnp.int32, (tq, tq), 0)
    col = lax.broadcasted_iota(jnp.int32, (tq, tq), 1)
    strict = col < row
    later_than = (row > col).astype(BF16)

    def key_rows(kb):
        return pl.ds(pl.multiple_of(kb * tq, tq), tq)

    def group_block(rows, g):
        lanes = slice(g * LANES, (g + 1) * LANES)
        return k_ref[0, rows, lanes], v_ref[0, rows, lanes]

    has_prev = qi > 0
    rows_a, rows_b = key_rows(qi), key_rows(jnp.maximum(qi - 1, 0))
    v_wins, scores = {}, {}

    def score(g):
        (k_a, v_a), (k_b, v_b) = group_block(rows_a, g), group_block(rows_b, g)
        k_win = jnp.concatenate([k_b, k_a], axis=0)
        v_wins[g] = jnp.concatenate([jnp.where(has_prev, v_b, jnp.zeros_like(v_b)), v_a], axis=0)
        scores[g] = [_dot_nt(q_heads[h], k_win) for h in (2 * g, 2 * g + 1)]

    def gate_terms(g):
        log_beta, pieces, sums = [], [], []
        for z in scores[g]:
            lb, lk = _sb_log_terms(z)
            lk = jnp.concatenate([lk[:, :tq], jnp.where(strict, lk[:, tq:], 0.0)], axis=1)
            hi, mid = _split2(lk)
            pieces += [hi[:, :tq], hi[:, tq:], mid[:, :tq], mid[:, tq:]]
            sum_a = jnp.sum(lk[:, tq:], axis=-1, keepdims=True)
            sum_b = jnp.where(has_prev, jnp.sum(lk[:, :tq], axis=-1, keepdims=True), 0.0)
            log_beta.append(lb)
            sums.append((sum_a, sum_b))
        return log_beta, _dot(jnp.concatenate(pieces, axis=0), later_than), sums

    def weights_and_values(g, log_beta, later, sums):
        outs = []
        for i, h in enumerate((2 * g, 2 * g + 1)):
            blk = lambda j: later[(4 * i + j) * tq:(4 * i + j + 1) * tq]
            sum_a, sum_b = sums[i]
            w_b = jnp.exp2(log_beta[i][:, :tq] + (blk(0) + blk(2)) + sum_a)
            w_a = jnp.where(strict, jnp.exp2(log_beta[i][:, tq:] + (blk(1) + blk(3))), 0.0)
            outs.append(_dot(jnp.concatenate([w_b, w_a], axis=1).astype(BF16), v_wins[g]))
            c_ref[:, h:h + 1] = sum_a + sum_b
        acc_ref[:, g * LANES:(g + 1) * LANES] = jnp.where(low, outs[0], outs[1])

    staged = None
    score(0)
    for g in range(n_groups):
        if g + 1 < n_groups:
            score(g + 1)
        current = gate_terms(g)
        if staged is not None:
            weights_and_values(g - 1, *staged)
        staged = current
    weights_and_values(n_groups - 1, *staged)

    def live():
        return jnp.max(c_ref[...]) > SB_LOG2_ZERO

    def cond(state):
        kb, go = state
        return jnp.logical_and(kb >= 0, go)

    def body(state):
        kb, _ = state
        rows = key_rows(kb)
        for g in range(n_groups):
            k, v = group_block(rows, g)
            outs = []
            for h in (2 * g, 2 * g + 1):
                c_old = c_ref[:, h:h + 1]
                pv, row_sum = _sb_block(q_heads[h], k, v, later_than, c_old)
                outs.append(pv)
                c_ref[:, h:h + 1] = c_old + row_sum
            acc_ref[:, g * LANES:(g + 1) * LANES] += jnp.where(low, outs[0], outs[1])
        return kb - 1, live()

    lax.while_loop(cond, body, (qi - 2, live()))
    o_ref[0] = acc_ref[...].astype(BF16)


def _sb_attn(proj, *, tq=SB_TILE, n_groups=SB_HEADS * SB_HEAD_DIM // LANES):
    b, s, _ = proj.shape
    width = n_groups * LANES
    q_off = (4 * DIFF_HEADS * DIFF_HEAD_DIM + DIFF_HEADS * 2 * DIFF_HEAD_DIM) // width
    n_steps = SB_HEADS * SB_HEAD_DIM // width
    kern = functools.partial(_sb_attn_kernel, tq=tq, n_groups=n_groups)
    return pl.pallas_call(
        kern,
        grid=(b, n_steps, s // tq),
        in_specs=[
            pl.BlockSpec((1, tq, width), lambda bi, p, qi: (bi, qi, q_off + p)),
            pl.BlockSpec((1, s, width), lambda bi, p, qi: (bi, 0, q_off + n_steps + p)),
            pl.BlockSpec((1, s, width), lambda bi, p, qi: (bi, 0, q_off + 2 * n_steps + p)),
        ],
        out_specs=pl.BlockSpec((1, tq, width), lambda bi, p, qi: (bi, qi, p)),
        out_shape=jax.ShapeDtypeStruct((b, s, n_steps * width), BF16),
        scratch_shapes=[pltpu.VMEM((tq, 2 * n_groups), F32), pltpu.VMEM((tq, width), F32)],
        compiler_params=_params(3, VMEM_MB["sb_attn"]),
        name="sb_attn",
    )(proj, proj, proj)


def _mla_proj_kernel(x_ref, g_ref, wdq_ref, qln_ref, wuq_ref, wdkv_ref, kvln_ref,
                     wuk_ref, wuv_ref, gkn_ref, qc_ref, qs_ref,
                     kc_ref, ksa_ref, ksb_ref, q_ref, k_ref, v_ref):
    tm = x_ref.shape[0]
    h = _rms(x_ref[...], g_ref[...]).astype(BF16)
    inv_dim = 1.0 / MLA_QK_DIM
    c_q = _rms(_dot(h, wdq_ref[...]), qln_ref[...]).astype(BF16)
    dkv = _dot(h, wdkv_ref[...])
    c_kv = _rms(dkv[:, :MLA_KV_RANK], kvln_ref[...]).astype(BF16)
    kr = dkv[:, MLA_KV_RANK:]
    kr_sq = 0.5 * jnp.sum(kr * kr, axis=-1, keepdims=True)
    half = MLA_ROPE_DIM // 2
    kr_rot = (kr * kc_ref[...] + pltpu.roll(kr, half, 1) * ksa_ref[...]
              + pltpu.roll(kr, LANES - half, 1) * ksb_ref[...])
    v_ref[...] = _dot(c_kv, wuv_ref[...]).astype(BF16)
    low = _lane_iota((tm, LANES)) < MLA_NOPE_DIM
    qc, qs, gkn = qc_ref[...], qs_ref[...], gkn_ref[...]

    def inv_rms(squares, extra=0.0):
        return lax.rsqrt((jnp.sum(squares, axis=-1, keepdims=True) + extra) * inv_dim + NORM_EPS)

    for p2 in range(0, MLA_HEADS // 2, 2):
        kn4 = _dot(c_kv, wuk_ref[:, p2 * LANES:(p2 + 2) * LANES])
        for i in range(2):
            p = p2 + i
            cols = slice(p * 2 * LANES, (p + 1) * 2 * LANES)
            q4 = _dot(c_q, wuq_ref[:, 2 * p * 2 * LANES:2 * (p + 1) * 2 * LANES])
            q2, q2s = q4[:, :2 * LANES], q4[:, 2 * LANES:]
            q_sq = q2 * q2
            rq = jnp.concatenate([jnp.broadcast_to(inv_rms(q_sq[:, :LANES]), (tm, LANES)),
                                  jnp.broadcast_to(inv_rms(q_sq[:, LANES:]), (tm, LANES))], axis=1)
            q_ref[:, cols] = ((q2 * qc + q2s * qs) * rq).astype(BF16)
            kn2 = kn4[:, i * LANES:(i + 1) * LANES]
            k_sq = kn2 * kn2
            kg = kn2 * gkn
            k_ref[:, p * 2 * LANES:p * 2 * LANES + LANES] = (
                jnp.where(low, kg, kr_rot) * inv_rms(jnp.where(low, k_sq, 0.0), kr_sq)).astype(BF16)
            k_ref[:, p * 2 * LANES + LANES:(p + 1) * 2 * LANES] = (
                jnp.where(low, kr_rot, kg) * inv_rms(jnp.where(low, 0.0, k_sq), kr_sq)).astype(BF16)


def _mla_pair_layout(even, odd_first, odd_last, n_rows):
    z = lambda w: jnp.zeros((n_rows, w), F32)
    return jnp.concatenate([even, z(LANES - MLA_QK_DIM), odd_first, z(MLA_ROPE_DIM), odd_last], axis=1)


def _mla_proj(x, gain, w_dq, q_lat_norm, w_uq, w_dkv, kv_lat_norm, w_ukv, qk_norm_q, qk_norm_k,
              seq, *, tm=PROJ_ROWS):
    t, d = x.shape
    q_rank = w_dq.shape[1]
    half = MLA_ROPE_DIM // 2
    n_pairs = MLA_HEADS // 2
    swap = lambda a: jnp.concatenate([a[..., half:], a[..., :half]], axis=-1)

    uq = w_uq.reshape(q_rank, MLA_HEADS, MLA_QK_DIM)
    nope, rope = uq[:, :, :MLA_NOPE_DIM], uq[:, :, MLA_NOPE_DIM:]
    rope_sw = jnp.concatenate([-rope[..., half:], rope[..., :half]], axis=-1)
    zeros = jnp.zeros_like
    ev_n, ev_r, od_r, od_n = nope[:, 0::2], rope[:, 0::2], rope[:, 1::2], nope[:, 1::2]
    wuq = jnp.concatenate(
        [ev_n, ev_r, zeros(ev_r), od_r, zeros(od_r), od_n,
         zeros(ev_n), rope_sw[:, 0::2], zeros(ev_r), rope_sw[:, 1::2], zeros(od_r), zeros(od_n)],
        axis=-1).reshape(q_rank, -1).astype(BF16)

    ukv = w_ukv.reshape(MLA_KV_RANK, MLA_HEADS, MLA_NOPE_DIM + MLA_V_DIM)
    wuk = ukv[:, :, :MLA_NOPE_DIM].reshape(MLA_KV_RANK, -1).astype(BF16)
    wuv = ukv[:, :, MLA_NOPE_DIM:].reshape(MLA_KV_RANK, -1).astype(BF16)
    kr_w = w_dkv[:, MLA_KV_RANK:]
    kr_pad = jnp.zeros((d, MLA_ROPE_DIM), F32)
    wdkv = jnp.concatenate([w_dkv[:, :MLA_KV_RANK], kr_w, kr_pad, kr_w, kr_pad], axis=1).astype(BF16)

    inv = 1.0 / (ROPE_THETA ** (jnp.arange(0, MLA_ROPE_DIM, 2, dtype=F32) / MLA_ROPE_DIM))
    ang = jnp.arange(seq, dtype=F32)[:, None] * inv[None, :]
    cos, sin = jnp.tile(jnp.cos(ang), (1, 2)), jnp.tile(jnp.sin(ang), (1, 2))
    scale = MLA_QK_DIM ** -0.5
    gq_n, gq_r = qk_norm_q[:MLA_NOPE_DIM] * scale, qk_norm_q[MLA_NOPE_DIM:] * scale
    gk_n, gk_r = qk_norm_k[:MLA_NOPE_DIM], qk_norm_k[MLA_NOPE_DIM:]
    bq_n = jnp.broadcast_to(gq_n, (seq, MLA_NOPE_DIM))
    qc = _mla_pair_layout(jnp.concatenate([bq_n, cos * gq_r], axis=1), cos * gq_r, bq_n, seq)
    qs = _mla_pair_layout(jnp.concatenate([zeros(bq_n), sin * swap(gq_r)], axis=1),
                          sin * swap(gq_r), zeros(bq_n), seq)
    first = jnp.arange(MLA_ROPE_DIM) < half
    k_cos = cos * gk_r
    k_sa = jnp.where(first, 0.0, sin * swap(gk_r))
    k_sb = jnp.where(first, -sin * swap(gk_r), 0.0)
    twice = lambda a: jnp.concatenate([a, zeros(a), a, zeros(a)], axis=1)
    n_seq_tiles = seq // tm
    tab = lambda w: pl.BlockSpec((tm, w), lambda i: (i % n_seq_tiles, 0))
    hw = MLA_HEADS * LANES
    vw = MLA_HEADS * MLA_V_DIM
    row = lambda w: pl.BlockSpec((tm, w), lambda i: (i, 0))
    return pl.pallas_call(
        _mla_proj_kernel,
        grid=(t // tm,),
        in_specs=[
            row(d), _resident((1, d)),
            _resident((d, q_rank)), _resident((1, q_rank)), _resident((q_rank, 2 * hw)),
            _resident((d, MLA_KV_RANK + LANES)), _resident((1, MLA_KV_RANK)),
            _resident((MLA_KV_RANK, n_pairs * LANES)), _resident((MLA_KV_RANK, vw)),
            _resident((1, LANES)),
            tab(2 * LANES), tab(2 * LANES), tab(LANES), tab(LANES), tab(LANES),
        ],
        out_specs=[row(hw), row(hw), row(vw)],
        out_shape=[jax.ShapeDtypeStruct((t, hw), BF16), jax.ShapeDtypeStruct((t, hw), BF16),
                   jax.ShapeDtypeStruct((t, vw), BF16)],
        compiler_params=_params(1, VMEM_MB["mla_proj"]),
        name="mla_proj",
    )(x, gain.reshape(1, d), w_dq.astype(BF16), q_lat_norm.reshape(1, q_rank), wuq, wdkv,
      kv_lat_norm.reshape(1, MLA_KV_RANK), wuk, wuv,
      jnp.tile(gk_n, 2).reshape(1, LANES), qc, qs, twice(k_cos), twice(k_sa), twice(k_sb))


def _mla_attn_kernel(bound_ref, q_ref, k_ref, v_ref, o_ref,
                     a0_ref, a1_ref, l0_ref, l1_ref, m0_ref, m1_ref, *, tq):
    low = _lane_iota((tq, LANES)) < MLA_V_DIM
    lanes = (slice(0, LANES), slice(LANES, 2 * LANES))

    def load_q(rows):
        return tuple(q_ref[0, rows, ln] for ln in lanes)

    def store_out(rows, o0, o1):
        o_ref[0, rows, :] = jnp.where(low, o0, o1).astype(BF16)

    _attend_pair(bound_ref[0], load_q, ((k_ref, lanes[0]), (k_ref, lanes[1])), v_ref, store_out,
                 (a0_ref, a1_ref), (l0_ref, l1_ref), (m0_ref, m1_ref), q_ref.shape[1] // tq, tq)


def _mla_attn(q, k, v, bound, *, tq=ATTN_TILE):
    b, s, _ = q.shape
    n_pairs = MLA_HEADS // 2
    kern = functools.partial(_mla_attn_kernel, tq=tq)
    return pl.pallas_call(
        kern,
        grid=(b, n_pairs),
        in_specs=[
            pl.BlockSpec(memory_space=pltpu.SMEM),
            pl.BlockSpec((1, s, 2 * LANES), lambda bi, p: (bi, 0, p)),
            pl.BlockSpec((1, s, 2 * LANES), lambda bi, p: (bi, 0, p)),
            pl.BlockSpec((1, s, LANES), lambda bi, p: (bi, 0, p)),
        ],
        out_specs=pl.BlockSpec((1, s, LANES), lambda bi, p: (bi, 0, p)),
        out_shape=jax.ShapeDtypeStruct((b, s, n_pairs * LANES), BF16),
        scratch_shapes=_pair_scratch(tq),
        compiler_params=_params(2, VMEM_MB["mla_attn"]),
        name="mla_attn",
    )(bound, q, k, v)


def _mem_kv_kernel(m_ref, g_ref, w_ref, kn_ref, k_ref, v_ref):
    h = _rms(m_ref[0], g_ref[...]).astype(BF16)
    kv = _dot(h, w_ref[...])
    width = XM_HEADS * XM_HEAD_DIM
    for hd in range(XM_HEADS):
        lanes = slice(hd * LANES, (hd + 1) * LANES)
        k_ref[0, :, lanes] = _rms(kv[:, lanes], kn_ref[...]).astype(BF16)
    v_ref[0] = kv[:, width:].astype(BF16)


def _mem_kv(mem, gain, w_kv, k_norm):
    b, m, d = mem.shape
    width = XM_HEADS * XM_HEAD_DIM
    out = jax.ShapeDtypeStruct((b, m, width), BF16)
    blk = pl.BlockSpec((1, m, width), lambda i: (i, 0, 0))
    return pl.pallas_call(
        _mem_kv_kernel,
        grid=(b,),
        in_specs=[pl.BlockSpec((1, m, d), lambda i: (i, 0, 0)), _resident((1, d)),
                  _resident((d, 2 * width)), _resident((1, XM_HEAD_DIM))],
        out_specs=[blk, blk],
        out_shape=[out, out],
        compiler_params=_params(1, VMEM_MB["mem_kv"]),
        name="mem_kv",
    )(mem, gain.reshape(1, d), w_kv.astype(BF16), k_norm.reshape(1, XM_HEAD_DIM))


def _post_kernel(*refs, n_mix):
    x_ref = refs[0]
    mix_refs = refs[1:1 + n_mix]
    wout_ref, g_ref, wq_ref, qn_ref, k_ref, v_ref, wo_ref, o_ref = refs[1 + n_mix:]
    x = x_ref[...]
    off = 0
    for a_ref in mix_refs:
        w = a_ref.shape[1]
        x = x + _dot(a_ref[...], wout_ref[off:off + w, :])
        off += w
    h = _rms(x, g_ref[...]).astype(BF16)
    q = _dot(h, wq_ref[...])
    lanes = [slice(hd * LANES, (hd + 1) * LANES) for hd in range(XM_HEADS)]
    scores = []
    for hd in range(XM_HEADS):
        qh = (_rms(q[:, lanes[hd]], qn_ref[...]) * (XM_HEAD_DIM ** -0.5)).astype(BF16)
        scores.append(_dot_nt(qh, k_ref[0, :, lanes[hd]]))

    def softmax(s):
        p = jnp.exp(s - jnp.max(s, axis=-1, keepdims=True))
        return p.astype(BF16), jnp.sum(p, axis=-1, keepdims=True)

    heads, staged = [], None
    for hd in range(XM_HEADS):
        current = softmax(scores[hd])
        if staged is not None:
            heads.append((_dot(staged[0], v_ref[0, :, lanes[hd - 1]]) / staged[1]).astype(BF16))
        staged = current
    heads.append((_dot(staged[0], v_ref[0, :, lanes[XM_HEADS - 1]]) / staged[1]).astype(BF16))
    o = jnp.concatenate(heads, axis=-1)
    o_ref[...] = x + _dot(o, wo_ref[...])


def _post(x, mixes, w_out, gain, w_q, q_norm, mem_k, mem_v, w_o, seq, *, tm=PROJ_ROWS):
    t, d = x.shape
    b, m, width = mem_k.shape
    n_seq_tiles = seq // tm
    row = lambda w: pl.BlockSpec((tm, w), lambda i: (i, 0))
    mem_spec = pl.BlockSpec((1, m, width), lambda i: (i // n_seq_tiles, 0, 0))
    kern = functools.partial(_post_kernel, n_mix=len(mixes))
    return pl.pallas_call(
        kern,
        grid=(t // tm,),
        in_specs=[row(d)] + [row(a.shape[1]) for a in mixes] + [
            _resident(w_out.shape), _resident((1, d)), _resident((d, width)),
            _resident((1, XM_HEAD_DIM)), mem_spec, mem_spec, _resident((width, d))],
        out_specs=row(d),
        out_shape=jax.ShapeDtypeStruct((t, d), F32),
        compiler_params=_params(1, VMEM_MB["post"]),
        name="post",
    )(x, *mixes, w_out.astype(BF16), gain.reshape(1, d), w_q.astype(BF16),
      q_norm.reshape(1, XM_HEAD_DIM), mem_k, mem_v, w_o.astype(BF16))


def _rope_tables(seq, dim, start, group):
    half = dim // 2
    inv = 1.0 / (ROPE_THETA ** (jnp.arange(0, dim, 2, dtype=F32) / dim))
    ang = jnp.arange(seq, dtype=F32)[:, None] * inv[None, :]
    cos, sin = jnp.cos(ang), jnp.sin(ang)
    zeros = jnp.zeros_like(sin)
    pad = lambda a, fill: jnp.concatenate(
        [jnp.full((seq, start), fill, F32), a, jnp.full((seq, group - start - dim), fill, F32)],
        axis=1)
    cos_t = pad(jnp.concatenate([cos, cos], axis=1), 1.0)
    sin_a = pad(jnp.concatenate([zeros, sin], axis=1), 0.0)
    sin_b = pad(jnp.concatenate([-sin, zeros], axis=1), 0.0)
    reps = LANES // group
    return tuple(jnp.tile(a, (1, reps)) for a in (cos_t, sin_a, sin_b))


def kernel(x, mem, ffn_norm, ffn_w_gate, ffn_w_up, ffn_w_down, mix_norm, ab_w_in, ab_w_out, diff_q_norm, diff_k_norm, diff_lambda_q1, diff_lambda_k1, diff_lambda_q2, diff_lambda_k2, diff_subln, mla_w_dq, mla_q_norm, mla_w_uq, mla_w_dkv, mla_kv_norm, mla_w_ukv, mla_qk_norm_q, mla_qk_norm_k, mla_w_o, xm_norm, xm_mem_norm, xm_w_q, xm_w_kv, xm_q_norm, xm_k_norm, xm_w_o):
    b, s, d = x.shape
    depth = ffn_norm.shape[0]
    x = x.reshape(b * s, d)
    diff_tables = _rope_tables(s, DIFF_HEAD_DIM, 0, DIFF_HEAD_DIM)
    ffn_w_f32 = (ffn_w_gate, ffn_w_up, ffn_w_down)
    ffn_w = tuple(w[:1, :1].astype(BF16) for w in ffn_w_f32)
    for layer in range(depth):
        i = layer // 2
        x = _ffn(x, ffn_norm[layer, 0], *ffn_w, layer, 0)
        if layer % 2 == 0:
            lambda_init = 0.8 - 0.6 * math.exp(-0.3 * layer)
            proj = _ab_proj(x, mix_norm[layer], ab_w_in[i], diff_q_norm[i], diff_k_norm[i],
                            diff_tables, s).reshape(b, s, -1)
            lam_rows = jnp.stack([diff_lambda_q1[i], diff_lambda_k1[i],
                                  diff_lambda_q2[i], diff_lambda_k2[i]])
            bound = _score_bound(diff_q_norm[i], diff_k_norm[i], DIFF_HEAD_DIM)
            oa, ffn_w = _diff_attn(proj, bound, lam_rows, diff_subln[i], lambda_init, ffn_w_f32)
            ob = _sb_attn(proj)
            mixes = [oa.reshape(b * s, -1), ob.reshape(b * s, -1)]
            w_out = ab_w_out[i]
        else:
            q, k, v = _mla_proj(x, mix_norm[layer], mla_w_dq[i], mla_q_norm[i], mla_w_uq[i],
                                mla_w_dkv[i], mla_kv_norm[i], mla_w_ukv[i], mla_qk_norm_q[i],
                                mla_qk_norm_k[i], s)
            bound = _score_bound(mla_qk_norm_q[i], mla_qk_norm_k[i], MLA_QK_DIM)
            o = _mla_attn(q.reshape(b, s, -1), k.reshape(b, s, -1), v.reshape(b, s, -1), bound)
            mixes = [o.reshape(b * s, -1)]
            w_out = mla_w_o[i]
        mem_k, mem_v = _mem_kv(mem, xm_mem_norm[layer], xm_w_kv[layer], xm_k_norm[layer])
        x = _post(x, mixes, w_out, xm_norm[layer], xm_w_q[layer], xm_q_norm[layer],
                  mem_k, mem_v, xm_w_o[layer], s)
        x = _ffn(x, ffn_norm[layer, 1], *ffn_w, layer, 1)
    return x.reshape(b, s, d)
```

```python
import functools
import math

import jax
import jax.numpy as jnp
from jax import lax
from jax.experimental import pallas as pl
from jax.experimental.pallas import tpu as pltpu

F32 = jnp.float32
BF16 = jnp.bfloat16

LANES = 128
V7X_MXU_DIM = 256
V7X_VMEM_LIMIT = 56 * 1024 * 1024

FFN_ROWS = 1024
PROJ_ROWS = 1024
ATTN_TILE = 512
SB_TILE = 256
VMEM_MB = dict(ffn=56, ab_proj=56, diff_attn=56, sb_attn=48, mla_proj=56, mla_attn=48,
               mem_kv=32, post=56, cast=32)

CHUNK = 64
ROPE_THETA = 10000.0
NORM_EPS = 1e-6
SB_LOG2_ZERO = -151.0
LOG2_E = math.log2(math.e)
MAX_SOFTMAX_SHIFT = 40.0

DIFF_HEADS = 4
DIFF_HEAD_DIM = 64
SB_HEADS = 8
SB_HEAD_DIM = 64
MLA_HEADS = 16
MLA_KV_RANK = 256
MLA_NOPE_DIM = 64
MLA_ROPE_DIM = 32
MLA_V_DIM = 64
MLA_QK_DIM = MLA_NOPE_DIM + MLA_ROPE_DIM
XM_HEADS = 4
XM_HEAD_DIM = 128


def _params(n_grid, vmem_mb):
    return pltpu.CompilerParams(
        dimension_semantics=("arbitrary",) * n_grid,
        vmem_limit_bytes=min(vmem_mb * 1024 * 1024, V7X_VMEM_LIMIT))


def _resident(shape):
    nd = len(shape)
    return pl.BlockSpec(shape, lambda *_: (0,) * nd, pipeline_mode=pl.Buffered(1))


def _rms(x, gain):
    return x * lax.rsqrt(jnp.mean(x * x, axis=-1, keepdims=True) + NORM_EPS) * gain


def _dot(a, b):
    return jnp.dot(a, b, preferred_element_type=F32)


def _dot_nt(a, b):
    return lax.dot_general(a, b, (((1,), (1,)), ((), ())), preferred_element_type=F32)


def _lane_iota(shape):
    return lax.broadcasted_iota(jnp.int32, shape, len(shape) - 1)


def _rope(x, cos, sin_a, sin_b, half):
    return x * cos + pltpu.roll(x, half, 1) * sin_a + pltpu.roll(x, LANES - half, 1) * sin_b


def _ffn_kernel(x_ref, g_ref, wg_ref, wu_ref, wd_ref, o_ref, h_ref, acc_ref, *, tf):
    x = x_ref[...]
    h_ref[...] = _rms(x, g_ref[...]).astype(BF16)
    for j in range(wg_ref.shape[1] // tf):
        cols = slice(j * tf, (j + 1) * tf)
        h = h_ref[...]
        g = _dot(h, wg_ref[:, cols])
        u = _dot(h, wu_ref[:, cols])
        a = (g * jax.nn.sigmoid(g) * u).astype(BF16)
        down = _dot(a, wd_ref[cols, :])
        if j == 0:
            acc_ref[...] = down
        else:
            acc_ref[...] += down
    o_ref[...] = x + 0.5 * acc_ref[...]


def _ffn(x, gain, w_gate, w_up, w_down, layer, half, *, tm=FFN_ROWS, tf=V7X_MXU_DIM):
    t, d = x.shape
    d_ff = w_gate.shape[-1]
    pick = lambda r, c: pl.BlockSpec((None, None, r, c), lambda i: (layer, half, 0, 0),
                                     pipeline_mode=pl.Buffered(1))
    return pl.pallas_call(
        functools.partial(_ffn_kernel, tf=tf),
        grid=(t // tm,),
        in_specs=[
            pl.BlockSpec((tm, d), lambda i: (i, 0)),
            _resident((1, d)),
            pick(d, d_ff), pick(d, d_ff), pick(d_ff, d),
        ],
        out_specs=pl.BlockSpec((tm, d), lambda i: (i, 0)),
        out_shape=jax.ShapeDtypeStruct((t, d), F32),
        scratch_shapes=[pltpu.VMEM((tm, d), BF16), pltpu.VMEM((tm, d), F32)],
        compiler_params=_params(1, VMEM_MB["ffn"]),
        name="ffn",
    )(x, gain.reshape(1, d), w_gate, w_up, w_down)


def _cast_kernel(w_ref, o_ref):
    o_ref[...] = w_ref[...].astype(BF16)


def _cast_first(w, *, n_steps=8):
    r, c = w.shape[-2:]
    blk = pl.BlockSpec((None, None, r // n_steps, c), lambda i: (0, 0, i, 0))
    return pl.pallas_call(
        _cast_kernel,
        grid=(n_steps,),
        in_specs=[blk],
        out_specs=blk,
        out_shape=jax.ShapeDtypeStruct((1, 1, r, c), BF16),
        compiler_params=_params(1, VMEM_MB["cast"]),
        name="cast_first",
    )(w)


def _ab_proj_kernel(x_ref, g_ref, w_ref, qn_ref, kn_ref, cos_ref, sa_ref, sb_ref, o_ref, *,
                    n_qk_groups, q_scales):
    h = _rms(x_ref[...], g_ref[...]).astype(BF16)
    cos, sa, sb = cos_ref[...], sa_ref[...], sb_ref[...]
    n_groups = w_ref.shape[1] // LANES
    lane = _lane_iota((x_ref.shape[0], LANES))
    low = lane < DIFF_HEAD_DIM
    for c in range(0, n_groups, 2):
        y2 = _dot(h, w_ref[:, c * LANES:(c + 2) * LANES])
        for k in range(2):
            g = c + k
            y = y2[:, k * LANES:(k + 1) * LANES]
            if g < 2 * n_qk_groups:
                gain = qn_ref[...] if g < n_qk_groups else kn_ref[...]
                sq = y * y
                ss_lo = jnp.sum(jnp.where(low, sq, 0.0), axis=-1, keepdims=True)
                ss_hi = jnp.sum(jnp.where(low, 0.0, sq), axis=-1, keepdims=True)
                ms = jnp.where(low, ss_lo, ss_hi) * (1.0 / DIFF_HEAD_DIM)
                y = y * lax.rsqrt(ms + NORM_EPS) * gain
                y = _rope(y, cos, sa, sb, DIFF_HEAD_DIM // 2)
            if g in q_scales:
                y = y * q_scales[g]
            o_ref[:, g * LANES:(g + 1) * LANES] = y.astype(BF16)


def _ab_proj(x, gain, w_in, q_norm, k_norm, tables, seq, *, tm=PROJ_ROWS):
    t, d = x.shape
    n = w_in.shape[1]
    n_qk_groups = 2 * DIFF_HEADS * DIFF_HEAD_DIM // LANES
    sb_q0 = (4 * DIFF_HEADS * DIFF_HEAD_DIM + DIFF_HEADS * 2 * DIFF_HEAD_DIM) // LANES
    q_scales = {g: DIFF_HEAD_DIM ** -0.5 for g in range(n_qk_groups)}
    q_scales.update({g: SB_HEAD_DIM ** -0.5 * LOG2_E
                     for g in range(sb_q0, sb_q0 + SB_HEADS * SB_HEAD_DIM // LANES)})
    cos, sa, sb = tables
    n_seq_tiles = seq // tm
    tab_spec = pl.BlockSpec((tm, LANES), lambda i: (i % n_seq_tiles, 0))
    kern = functools.partial(_ab_proj_kernel, n_qk_groups=n_qk_groups,
                             q_scales=q_scales)
    return pl.pallas_call(
        kern,
        grid=(t // tm,),
        in_specs=[
            pl.BlockSpec((tm, d), lambda i: (i, 0)),
            _resident((1, d)),
            _resident((d, n)),
            _resident((1, LANES)),
            _resident((1, LANES)),
            tab_spec, tab_spec, tab_spec,
        ],
        out_specs=pl.BlockSpec((tm, n), lambda i: (i, 0)),
        out_shape=jax.ShapeDtypeStruct((t, n), BF16),
        compiler_params=_params(1, VMEM_MB["ab_proj"]),
        name="ab_proj",
    )(x, gain.reshape(1, d), w_in.astype(BF16),
      jnp.tile(q_norm, 2).reshape(1, LANES), jnp.tile(k_norm, 2).reshape(1, LANES),
      cos, sa, sb)


def _chunk_causal_mask(tq, tk):
    qc = lax.broadcasted_iota(jnp.int32, (tq, tk), 0) // CHUNK
    kc = lax.broadcasted_iota(jnp.int32, (tq, tk), 1) // CHUNK
    return kc <= qc


def _fold_lanes(p):
    out = p[:, :LANES]
    for c in range(1, p.shape[1] // LANES):
        out = out + p[:, c * LANES:(c + 1) * LANES]
    return out


def _attend_pair(bound, load_q, k_srcs, v_ref, store_out, acc_refs, l_refs, m_refs, n_q, tq,
                 side_work=None):
    def key_block(h, rows):
        ref, lanes = k_srcs[h]
        return ref[0, rows, lanes]

    @pl.when(bound <= MAX_SOFTMAX_SHIFT)
    def _():
        hq = tq // 2
        lower_rows = lambda a, b: jnp.concatenate([a[:hq], a[hq:] + b], axis=0)
        for qi in range(n_q):
            q_rows = slice(qi * tq, (qi + 1) * tq)
            qs = load_q(q_rows)
            if side_work is not None:
                side_work(qi, n_q)
            acc, lsum = [None, None], [None, None]
            for kb in range(qi):
                rows = slice(kb * tq, (kb + 1) * tq)
                v = v_ref[0, rows, :]
                for h in range(2):
                    p = jnp.exp(_dot_nt(qs[h], key_block(h, rows)) - bound)
                    part, pv = _fold_lanes(p), _dot(p.astype(BF16), v)
                    lsum[h] = part if kb == 0 else lsum[h] + part
                    acc[h] = pv if kb == 0 else acc[h] + pv
            rows_a = slice(qi * tq, qi * tq + hq)
            rows_b = slice(qi * tq + hq, (qi + 1) * tq)
            v_a, v_b = v_ref[0, rows_a, :], v_ref[0, rows_b, :]
            for h in range(2):
                p_a = jnp.exp(_dot_nt(qs[h], key_block(h, rows_a)) - bound)
                p_a = jnp.where(_chunk_causal_mask(tq, hq), p_a, 0.0)
                p_b = jnp.exp(_dot_nt(qs[h][hq:], key_block(h, rows_b)) - bound)
                p_b = jnp.where(_chunk_causal_mask(hq, hq), p_b, 0.0)
                part = lower_rows(_fold_lanes(p_a), _fold_lanes(p_b))
                pv = lower_rows(_dot(p_a.astype(BF16), v_a), _dot(p_b.astype(BF16), v_b))
                lsum[h] = part if qi == 0 else lsum[h] + part
                acc[h] = pv if qi == 0 else acc[h] + pv
            store_out(q_rows, *(acc[h] / jnp.sum(lsum[h], axis=-1, keepdims=True)
                                for h in range(2)))

    @pl.when(bound > MAX_SOFTMAX_SHIFT)
    def _():
        if side_work is not None:
            for i in range(n_q):
                side_work(i, n_q)

        def q_tile(qi, carry):
            q_rows = pl.ds(pl.multiple_of(qi * tq, tq), tq)
            qs = load_q(q_rows)
            for h in range(2):
                acc_refs[h][...] = jnp.zeros_like(acc_refs[h])
                l_refs[h][...] = jnp.zeros_like(l_refs[h])
                m_refs[h][...] = jnp.full_like(m_refs[h], -jnp.inf)

            def block(kb, masked):
                rows = pl.ds(pl.multiple_of(kb * tq, tq), tq)
                v = v_ref[0, rows, :]
                for h in range(2):
                    s = _dot_nt(qs[h], key_block(h, rows))
                    if masked:
                        s = jnp.where(_chunk_causal_mask(tq, tq), s, -jnp.inf)
                    m_old = m_refs[h][...]
                    m_new = jnp.maximum(m_old, jnp.max(s, axis=-1, keepdims=True))
                    alpha = jnp.exp(m_old - m_new)
                    p = jnp.exp(s - m_new)
                    l_refs[h][...] = alpha * l_refs[h][...] + jnp.sum(p, axis=-1, keepdims=True)
                    acc_refs[h][...] = alpha * acc_refs[h][...] + _dot(p.astype(BF16), v)
                    m_refs[h][...] = m_new

            def body(kb, c):
                block(kb, False)
                return c

            lax.fori_loop(0, qi, body, 0)
            block(qi, True)
            store_out(q_rows, *(acc_refs[h][...] / l_refs[h][...] for h in range(2)))
            return carry

        lax.fori_loop(0, n_q, q_tile, 0)


def _pair_scratch(tq):
    acc = pltpu.VMEM((tq, LANES), F32)
    stat = pltpu.VMEM((tq, 1), F32)
    return [acc, acc, stat, stat, stat, stat]


def _score_bound(q_gain, k_gain, dim):
    return (jnp.max(jnp.abs(q_gain)) * jnp.max(jnp.abs(k_gain)) * math.sqrt(dim)).reshape(1)


def _diff_attn_kernel(bound_ref, q1_ref, q2_ref, k1_ref, k2_ref, v_ref, lam_ref, sub_ref,
                      wg_ref, wu_ref, wd_ref, o_ref, wg16_ref, wu16_ref, wd16_ref,
                      a1_ref, a2_ref, l1_ref, l2_ref, m1_ref, m2_ref, *, tq, lambda_init):
    j = pl.program_id(1)

    def cast_ffn_weights(i, n):
        for src, dst in ((wg_ref, wg16_ref), (wu_ref, wu16_ref)):
            r = src.shape[0] // n
            dst[i * r:(i + 1) * r, :] = src[i * r:(i + 1) * r, :].astype(BF16)
        c = wd_ref.shape[1] // n
        wd16_ref[:, i * c:(i + 1) * c] = wd_ref[:, i * c:(i + 1) * c].astype(BF16)

    mine = (_lane_iota((tq, LANES)) >= DIFF_HEAD_DIM).astype(jnp.int32) == j % 2
    lam_v = lam_ref[...]
    lam = (jnp.exp(jnp.sum(lam_v[0:1] * lam_v[1:2], axis=-1, keepdims=True))
           - jnp.exp(jnp.sum(lam_v[2:3] * lam_v[3:4], axis=-1, keepdims=True)) + lambda_init)

    def load_q(rows):
        return tuple(jnp.where(mine, q_ref[0, rows, :].astype(F32), 0.0).astype(BF16)
                     for q_ref in (q1_ref, q2_ref))

    def store_out(rows, o1, o2):
        o = _rms(o1 - lam * o2, sub_ref[...]) * (1.0 - lambda_init)
        o_ref[0, rows, :] = o.astype(BF16)

    whole = slice(None)
    _attend_pair(bound_ref[0], load_q, ((k1_ref, whole), (k2_ref, whole)), v_ref, store_out,
                 (a1_ref, a2_ref), (l1_ref, l2_ref), (m1_ref, m2_ref), q1_ref.shape[1] // tq, tq,
                 side_work=cast_ffn_weights)


def _diff_attn(proj, bound, lam_rows, subln, lambda_init, ffn_w, *, tq=ATTN_TILE):
    b, s, _ = proj.shape
    k_off = 2 * DIFF_HEADS * DIFF_HEAD_DIM // LANES
    v_off = 2 * k_off
    half = DIFF_HEADS // 2
    kern = functools.partial(_diff_attn_kernel, tq=tq, lambda_init=lambda_init)
    cols = lambda f: pl.BlockSpec((1, s, LANES), lambda bi, j: (bi, 0, f(j)))
    n_steps = b * DIFF_HEADS
    flat = [w.reshape(-1, w.shape[-1]) for w in ffn_w]
    slabs = [pl.BlockSpec((w.shape[0] // n_steps, w.shape[1]),
                          lambda bi, j: (bi * DIFF_HEADS + j, 0)) for w in flat]
    assert all(w.shape[0] % n_steps == 0 for w in flat)
    outs = pl.pallas_call(
        kern,
        grid=(b, DIFF_HEADS),
        in_specs=[
            pl.BlockSpec(memory_space=pltpu.SMEM),
            cols(lambda j: j // 2),
            cols(lambda j: half + j // 2),
            cols(lambda j: k_off + j // 2),
            cols(lambda j: k_off + half + j // 2),
            cols(lambda j: v_off + j),
            _resident((4, DIFF_HEAD_DIM)),
            _resident((1, LANES)),
            *slabs,
        ],
        out_specs=[cols(lambda j: j), *slabs],
        out_shape=[jax.ShapeDtypeStruct((b, s, DIFF_HEADS * LANES), BF16)]
        + [jax.ShapeDtypeStruct(w.shape, BF16) for w in flat],
        scratch_shapes=_pair_scratch(tq),
        compiler_params=_params(2, VMEM_MB["diff_attn"]),
        name="diff_attn",
    )(bound, proj, proj, proj, proj, proj, lam_rows, subln.reshape(1, LANES), *flat)
    return outs[0], tuple(o.reshape(w.shape) for o, w in zip(outs[1:], ffn_w))


def _split2(x):
    hi = x.astype(BF16)
    return hi, (x - hi.astype(F32)).astype(BF16)


def _sb_log_terms(z):
    sign_bit = jnp.uint32(0x80000000)
    neg_abs = lax.bitcast_convert_type(lax.bitcast_convert_type(z, jnp.uint32) | sign_bit, F32)
    log_beta = jnp.minimum(z, 0.0) - jnp.log2(1.0 + jnp.exp2(neg_abs))
    return log_beta, log_beta - z


def _sb_block(q, k, v, later_than, c_in):
    log_beta, log_keep = _sb_log_terms(_dot_nt(q, k))
    hi, mid = _split2(log_keep)
    later = _dot(hi, later_than) + _dot(mid, later_than)
    w = jnp.exp2(log_beta + later + c_in)
    return _dot(w.astype(BF16), v), jnp.sum(log_keep, axis=-1, keepdims=True)


def _sb_attn_kernel(q_ref, k_ref, v_ref, o_ref, c_ref, acc_ref, *, tq, n_groups):
    qi = pl.program_id(2)
    low = _lane_iota((tq, LANES)) < SB_HEAD_DIM
    q_heads = []
    for g in range(n_groups):
        q = q_ref[0, :, g * LANES:(g + 1) * LANES].astype(F32)
        q_heads += [jnp.where(low, q, 0.0).astype(BF16), jnp.where(low, 0.0, q).astype(BF16)]
    row = lax.broadcasted_iota(jnp.int32, (tq, tq), 0)
    col = lax.broadcasted_iota(jnp.int32, (tq, tq), 1)
    strict = col < row
    later_than = (row > col).astype(BF16)

    def key_rows(kb):
        return pl.ds(pl.multiple_of(kb * tq, tq), tq)

    def group_block(rows, g):
        lanes = slice(g * LANES, (g + 1) * LANES)
        return k_ref[0, rows, lanes], v_ref[0, rows, lanes]

    has_prev = qi > 0
    rows_a, rows_b = key_rows(qi), key_rows(jnp.maximum(qi - 1, 0))
    v_wins, scores = {}, {}

    def score(g):
        (k_a, v_a), (k_b, v_b) = group_block(rows_a, g), group_block(rows_b, g)
        k_win = jnp.concatenate([k_b, k_a], axis=0)
        v_wins[g] = jnp.concatenate([jnp.where(has_prev, v_b, jnp.zeros_like(v_b)), v_a], axis=0)
        scores[g] = [_dot_nt(q_heads[h], k_win) for h in (2 * g, 2 * g + 1)]

    def gate_terms(g):
        log_beta, pieces, sums = [], [], []
        for z in scores[g]:
            lb, lk = _sb_log_terms(z)
            lk = jnp.concatenate([lk[:, :tq], jnp.where(strict, lk[:, tq:], 0.0)], axis=1)
            hi, mid = _split2(lk)
            pieces += [hi[:, :tq], hi[:, tq:], mid[:, :tq], mid[:, tq:]]
            sum_a = jnp.sum(lk[:, tq:], axis=-1, keepdims=True)
            sum_b = jnp.where(has_prev, jnp.sum(lk[:, :tq], axis=-1, keepdims=True), 0.0)
            log_beta.append(lb)
            sums.append((sum_a, sum_b))
        return log_beta, _dot(jnp.concatenate(pieces, axis=0), later_than), sums

    def weights_and_values(g, log_beta, later, sums):
        outs = []
        for i, h in enumerate((2 * g, 2 * g + 1)):
            blk = lambda j: later[(4 * i + j) * tq:(4 * i + j + 1) * tq]
            sum_a, sum_b = sums[i]
            w_b = jnp.exp2(log_beta[i][:, :tq] + (blk(0) + blk(2)) + sum_a)
            w_a = jnp.where(strict, jnp.exp2(log_beta[i][:, tq:] + (blk(1) + blk(3))), 0.0)
            outs.append(_dot(jnp.concatenate([w_b, w_a], axis=1).astype(BF16), v_wins[g]))
            c_ref[:, h:h + 1] = sum_a + sum_b
        acc_ref[:, g * LANES:(g + 1) * LANES] = jnp.where(low, outs[0], outs[1])

    staged = None
    score(0)
    for g in range(n_groups):
        if g + 1 < n_groups:
            score(g + 1)
        current = gate_terms(g)
        if staged is not None:
            weights_and_values(g - 1, *staged)
        staged = current
    weights_and_values(n_groups - 1, *staged)

    def live():
        return jnp.max(c_ref[...]) > SB_LOG2_ZERO

    def cond(state):
        kb, go = state
        return jnp.logical_and(kb >= 0, go)

    def body(state):
        kb, _ = state
        rows = key_rows(kb)
        for g in range(n_groups):
            k, v = group_block(rows, g)
            outs = []
            for h in (2 * g, 2 * g + 1):
                c_old = c_ref[:, h:h + 1]
                pv, row_sum = _sb_block(q_heads[h], k, v, later_than, c_old)
                outs.append(pv)
                c_ref[:, h:h + 1] = c_old + row_sum
            acc_ref[:, g * LANES:(g + 1) * LANES] += jnp.where(low, outs[0], outs[1])
        return kb - 1, live()

    lax.while_loop(cond, body, (qi - 2, live()))
    o_ref[0] = acc_ref[...].astype(BF16)


def _sb_attn(proj, *, tq=SB_TILE, n_groups=SB_HEADS * SB_HEAD_DIM // LANES):
    b, s, _ = proj.shape
    width = n_groups * LANES
    q_off = (4 * DIFF_HEADS * DIFF_HEAD_DIM + DIFF_HEADS * 2 * DIFF_HEAD_DIM) // width
    n_steps = SB_HEADS * SB_HEAD_DIM // width
    kern = functools.partial(_sb_attn_kernel, tq=tq, n_groups=n_groups)
    return pl.pallas_call(
        kern,
        grid=(b, n_steps, s // tq),
        in_specs=[
            pl.BlockSpec((1, tq, width), lambda bi, p, qi: (bi, qi, q_off + p)),
            pl.BlockSpec((1, s, width), lambda bi, p, qi: (bi, 0, q_off + n_steps + p)),
            pl.BlockSpec((1, s, width), lambda bi, p, qi: (bi, 0, q_off + 2 * n_steps + p)),
        ],
        out_specs=pl.BlockSpec((1, tq, width), lambda bi, p, qi: (bi, qi, p)),
        out_shape=jax.ShapeDtypeStruct((b, s, n_steps * width), BF16),
        scratch_shapes=[pltpu.VMEM((tq, 2 * n_groups), F32), pltpu.VMEM((tq, width), F32)],
        compiler_params=_params(3, VMEM_MB["sb_attn"]),
        name="sb_attn",
    )(proj, proj, proj)


def _mla_proj_kernel(x_ref, g_ref, wdq_ref, qln_ref, wuq_ref, wdkv_ref, kvln_ref,
                     wuk_ref, wuv_ref, gkn_ref, qc_ref, qs_ref,
                     kc_ref, ksa_ref, ksb_ref, q_ref, k_ref, v_ref):
    tm = x_ref.shape[0]
    h = _rms(x_ref[...], g_ref[...]).astype(BF16)
    inv_dim = 1.0 / MLA_QK_DIM
    c_q = _rms(_dot(h, wdq_ref[...]), qln_ref[...]).astype(BF16)
    dkv = _dot(h, wdkv_ref[...])
    c_kv = _rms(dkv[:, :MLA_KV_RANK], kvln_ref[...]).astype(BF16)
    kr = dkv[:, MLA_KV_RANK:]
    kr_sq = 0.5 * jnp.sum(kr * kr, axis=-1, keepdims=True)
    half = MLA_ROPE_DIM // 2
    kr_rot = (kr * kc_ref[...] + pltpu.roll(kr, half, 1) * ksa_ref[...]
              + pltpu.roll(kr, LANES - half, 1) * ksb_ref[...])
    v_ref[...] = _dot(c_kv, wuv_ref[...]).astype(BF16)
    low = _lane_iota((tm, LANES)) < MLA_NOPE_DIM
    qc, qs, gkn = qc_ref[...], qs_ref[...], gkn_ref[...]

    def inv_rms(squares, extra=0.0):
        return lax.rsqrt((jnp.sum(squares, axis=-1, keepdims=True) + extra) * inv_dim + NORM_EPS)

    for p2 in range(0, MLA_HEADS // 2, 2):
        kn4 = _dot(c_kv, wuk_ref[:, p2 * LANES:(p2 + 2) * LANES])
        for i in range(2):
            p = p2 + i
            cols = slice(p * 2 * LANES, (p + 1) * 2 * LANES)
            q4 = _dot(c_q, wuq_ref[:, 2 * p * 2 * LANES:2 * (p + 1) * 2 * LANES])
            q2, q2s = q4[:, :2 * LANES], q4[:, 2 * LANES:]
            q_sq = q2 * q2
            rq = jnp.concatenate([jnp.broadcast_to(inv_rms(q_sq[:, :LANES]), (tm, LANES)),
                                  jnp.broadcast_to(inv_rms(q_sq[:, LANES:]), (tm, LANES))], axis=1)
            q_ref[:, cols] = ((q2 * qc + q2s * qs) * rq).astype(BF16)
            kn2 = kn4[:, i * LANES:(i + 1) * LANES]
            k_sq = kn2 * kn2
            kg = kn2 * gkn
            k_ref[:, p * 2 * LANES:p * 2 * LANES + LANES] = (
                jnp.where(low, kg, kr_rot) * inv_rms(jnp.where(low, k_sq, 0.0), kr_sq)).astype(BF16)
            k_ref[:, p * 2 * LANES + LANES:(p + 1) * 2 * LANES] = (
                jnp.where(low, kr_rot, kg) * inv_rms(jnp.where(low, 0.0, k_sq), kr_sq)).astype(BF16)


def _mla_pair_layout(even, odd_first, odd_last, n_rows):
    z = lambda w: jnp.zeros((n_rows, w), F32)
    return jnp.concatenate([even, z(LANES - MLA_QK_DIM), odd_first, z(MLA_ROPE_DIM), odd_last], axis=1)


def _mla_proj(x, gain, w_dq, q_lat_norm, w_uq, w_dkv, kv_lat_norm, w_ukv, qk_norm_q, qk_norm_k,
              seq, *, tm=PROJ_ROWS):
    t, d = x.shape
    q_rank = w_dq.shape[1]
    half = MLA_ROPE_DIM // 2
    n_pairs = MLA_HEADS // 2
    swap = lambda a: jnp.concatenate([a[..., half:], a[..., :half]], axis=-1)

    uq = w_uq.reshape(q_rank, MLA_HEADS, MLA_QK_DIM)
    nope, rope = uq[:, :, :MLA_NOPE_DIM], uq[:, :, MLA_NOPE_DIM:]
    rope_sw = jnp.concatenate([-rope[..., half:], rope[..., :half]], axis=-1)
    zeros = jnp.zeros_like
    ev_n, ev_r, od_r, od_n = nope[:, 0::2], rope[:, 0::2], rope[:, 1::2], nope[:, 1::2]
    wuq = jnp.concatenate(
        [ev_n, ev_r, zeros(ev_r), od_r, zeros(od_r), od_n,
         zeros(ev_n), rope_sw[:, 0::2], zeros(ev_r), rope_sw[:, 1::2], zeros(od_r), zeros(od_n)],
        axis=-1).reshape(q_rank, -1).astype(BF16)

    ukv = w_ukv.reshape(MLA_KV_RANK, MLA_HEADS, MLA_NOPE_DIM + MLA_V_DIM)
    wuk = ukv[:, :, :MLA_NOPE_DIM].reshape(MLA_KV_RANK, -1).astype(BF16)
    wuv = ukv[:, :, MLA_NOPE_DIM:].reshape(MLA_KV_RANK, -1).astype(BF16)
    kr_w = w_dkv[:, MLA_KV_RANK:]
    kr_pad = jnp.zeros((d, MLA_ROPE_DIM), F32)
    wdkv = jnp.concatenate([w_dkv[:, :MLA_KV_RANK], kr_w, kr_pad, kr_w, kr_pad], axis=1).astype(BF16)

    inv = 1.0 / (ROPE_THETA ** (jnp.arange(0, MLA_ROPE_DIM, 2, dtype=F32) / MLA_ROPE_DIM))
    ang = jnp.arange(seq, dtype=F32)[:, None] * inv[None, :]
    cos, sin = jnp.tile(jnp.cos(ang), (1, 2)), jnp.tile(jnp.sin(ang), (1, 2))
    scale = MLA_QK_DIM ** -0.5
    gq_n, gq_r = qk_norm_q[:MLA_NOPE_DIM] * scale, qk_norm_q[MLA_NOPE_DIM:] * scale
    gk_n, gk_r = qk_norm_k[:MLA_NOPE_DIM], qk_norm_k[MLA_NOPE_DIM:]
    bq_n = jnp.broadcast_to(gq_n, (seq, MLA_NOPE_DIM))
    qc = _mla_pair_layout(jnp.concatenate([bq_n, cos * gq_r], axis=1), cos * gq_r, bq_n, seq)
    qs = _mla_pair_layout(jnp.concatenate([zeros(bq_n), sin * swap(gq_r)], axis=1),
                          sin * swap(gq_r), zeros(bq_n), seq)
    first = jnp.arange(MLA_ROPE_DIM) < half
    k_cos = cos * gk_r
    k_sa = jnp.where(first, 0.0, sin * swap(gk_r))
    k_sb = jnp.where(first, -sin * swap(gk_r), 0.0)
    twice = lambda a: jnp.concatenate([a, zeros(a), a, zeros(a)], axis=1)
    n_seq_tiles = seq // tm
    tab = lambda w: pl.BlockSpec((tm, w), lambda i: (i % n_seq_tiles, 0))
    hw = MLA_HEADS * LANES
    vw = MLA_HEADS * MLA_V_DIM
    row = lambda w: pl.BlockSpec((tm, w), lambda i: (i, 0))
    return pl.pallas_call(
        _mla_proj_kernel,
        grid=(t // tm,),
        in_specs=[
            row(d), _resident((1, d)),
            _resident((d, q_rank)), _resident((1, q_rank)), _resident((q_rank, 2 * hw)),
            _resident((d, MLA_KV_RANK + LANES)), _resident((1, MLA_KV_RANK)),
            _resident((MLA_KV_RANK, n_pairs * LANES)), _resident((MLA_KV_RANK, vw)),
            _resident((1, LANES)),
            tab(2 * LANES), tab(2 * LANES), tab(LANES), tab(LANES), tab(LANES),
        ],
        out_specs=[row(hw), row(hw), row(vw)],
        out_shape=[jax.ShapeDtypeStruct((t, hw), BF16), jax.ShapeDtypeStruct((t, hw), BF16),
                   jax.ShapeDtypeStruct((t, vw), BF16)],
        compiler_params=_params(1, VMEM_MB["mla_proj"]),
        name="mla_proj",
    )(x, gain.reshape(1, d), w_dq.astype(BF16), q_lat_norm.reshape(1, q_rank), wuq, wdkv,
      kv_lat_norm.reshape(1, MLA_KV_RANK), wuk, wuv,
      jnp.tile(gk_n, 2).reshape(1, LANES), qc, qs, twice(k_cos), twice(k_sa), twice(k_sb))


def _mla_attn_kernel(bound_ref, q_ref, k_ref, v_ref, o_ref,
                     a0_ref, a1_ref, l0_ref, l1_ref, m0_ref, m1_ref, *, tq):
    low = _lane_iota((tq, LANES)) < MLA_V_DIM
    lanes = (slice(0, LANES), slice(LANES, 2 * LANES))

    def load_q(rows):
        return tuple(q_ref[0, rows, ln] for ln in lanes)

    def store_out(rows, o0, o1):
        o_ref[0, rows, :] = jnp.where(low, o0, o1).astype(BF16)

    _attend_pair(bound_ref[0], load_q, ((k_ref, lanes[0]), (k_ref, lanes[1])), v_ref, store_out,
                 (a0_ref, a1_ref), (l0_ref, l1_ref), (m0_ref, m1_ref), q_ref.shape[1] // tq, tq)


def _mla_attn(q, k, v, bound, *, tq=ATTN_TILE):
    b, s, _ = q.shape
    n_pairs = MLA_HEADS // 2
    kern = functools.partial(_mla_attn_kernel, tq=tq)
    return pl.pallas_call(
        kern,
        grid=(b, n_pairs),
        in_specs=[
            pl.BlockSpec(memory_space=pltpu.SMEM),
            pl.BlockSpec((1, s, 2 * LANES), lambda bi, p: (bi, 0, p)),
            pl.BlockSpec((1, s, 2 * LANES), lambda bi, p: (bi, 0, p)),
            pl.BlockSpec((1, s, LANES), lambda bi, p: (bi, 0, p)),
        ],
        out_specs=pl.BlockSpec((1, s, LANES), lambda bi, p: (bi, 0, p)),
        out_shape=jax.ShapeDtypeStruct((b, s, n_pairs * LANES), BF16),
        scratch_shapes=_pair_scratch(tq),
        compiler_params=_params(2, VMEM_MB["mla_attn"]),
        name="mla_attn",
    )(bound, q, k, v)


def _mem_kv_kernel(m_ref, g_ref, w_ref, kn_ref, k_ref, v_ref):
    h = _rms(m_ref[0], g_ref[...]).astype(BF16)
    kv = _dot(h, w_ref[...])
    width = XM_HEADS * XM_HEAD_DIM
    for hd in range(XM_HEADS):
        lanes = slice(hd * LANES, (hd + 1) * LANES)
        k_ref[0, :, lanes] = _rms(kv[:, lanes], kn_ref[...]).astype(BF16)
    v_ref[0] = kv[:, width:].astype(BF16)


def _mem_kv(mem, gain, w_kv, k_norm):
    b, m, d = mem.shape
    width = XM_HEADS * XM_HEAD_DIM
    out = jax.ShapeDtypeStruct((b, m, width), BF16)
    blk = pl.BlockSpec((1, m, width), lambda i: (i, 0, 0))
    return pl.pallas_call(
        _mem_kv_kernel,
        grid=(b,),
        in_specs=[pl.BlockSpec((1, m, d), lambda i: (i, 0, 0)), _resident((1, d)),
                  _resident((d, 2 * width)), _resident((1, XM_HEAD_DIM))],
        out_specs=[blk, blk],
        out_shape=[out, out],
        compiler_params=_params(1, VMEM_MB["mem_kv"]),
        name="mem_kv",
    )(mem, gain.reshape(1, d), w_kv.astype(BF16), k_norm.reshape(1, XM_HEAD_DIM))


def _post_kernel(*refs, n_mix):
    x_ref = refs[0]
    mix_refs = refs[1:1 + n_mix]
    wout_ref, g_ref, wq_ref, qn_ref, k_ref, v_ref, wo_ref, o_ref = refs[1 + n_mix:]
    x = x_ref[...]
    off = 0
    for a_ref in mix_refs:
        w = a_ref.shape[1]
        x = x + _dot(a_ref[...], wout_ref[off:off + w, :])
        off += w
    h = _rms(x, g_ref[...]).astype(BF16)
    q = _dot(h, wq_ref[...])
    lanes = [slice(hd * LANES, (hd + 1) * LANES) for hd in range(XM_HEADS)]
    scores = []
    for hd in range(XM_HEADS):
        qh = (_rms(q[:, lanes[hd]], qn_ref[...]) * (XM_HEAD_DIM ** -0.5)).astype(BF16)
        scores.append(_dot_nt(qh, k_ref[0, :, lanes[hd]]))

    def softmax(s):
        p = jnp.exp(s - jnp.max(s, axis=-1, keepdims=True))
        return p.astype(BF16), jnp.sum(p, axis=-1, keepdims=True)

    heads, staged = [], None
    for hd in range(XM_HEADS):
        current = softmax(scores[hd])
        if staged is not None:
            heads.append((_dot(staged[0], v_ref[0, :, lanes[hd - 1]]) / staged[1]).astype(BF16))
        staged = current
    heads.append((_dot(staged[0], v_ref[0, :, lanes[XM_HEADS - 1]]) / staged[1]).astype(BF16))
    o = jnp.concatenate(heads, axis=-1)
    o_ref[...] = x + _dot(o, wo_ref[...])


def _post(x, mixes, w_out, gain, w_q, q_norm, mem_k, mem_v, w_o, seq, *, tm=PROJ_ROWS):
    t, d = x.shape
    b, m, width = mem_k.shape
    n_seq_tiles = seq // tm
    row = lambda w: pl.BlockSpec((tm, w), lambda i: (i, 0))
    mem_spec = pl.BlockSpec((1, m, width), lambda i: (i // n_seq_tiles, 0, 0))
    kern = functools.partial(_post_kernel, n_mix=len(mixes))
    return pl.pallas_call(
        kern,
        grid=(t // tm,),
        in_specs=[row(d)] + [row(a.shape[1]) for a in mixes] + [
            _resident(w_out.shape), _resident((1, d)), _resident((d, width)),
            _resident((1, XM_HEAD_DIM)), mem_spec, mem_spec, _resident((width, d))],
        out_specs=row(d),
        out_shape=jax.ShapeDtypeStruct((t, d), F32),
        compiler_params=_params(1, VMEM_MB["post"]),
        name="post",
    )(x, *mixes, w_out.astype(BF16), gain.reshape(1, d), w_q.astype(BF16),
      q_norm.reshape(1, XM_HEAD_DIM), mem_k, mem_v, w_o.astype(BF16))


def _rope_tables(seq, dim, start, group):
    half = dim // 2
    inv = 1.0 / (ROPE_THETA ** (jnp.arange(0, dim, 2, dtype=F32) / dim))
    ang = jnp.arange(seq, dtype=F32)[:, None] * inv[None, :]
    cos, sin = jnp.cos(ang), jnp.sin(ang)
    zeros = jnp.zeros_like(sin)
    pad = lambda a, fill: jnp.concatenate(
        [jnp.full((seq, start), fill, F32), a, jnp.full((seq, group - start - dim), fill, F32)],
        axis=1)
    cos_t = pad(jnp.concatenate([cos, cos], axis=1), 1.0)
    sin_a = pad(jnp.concatenate([zeros, sin], axis=1), 0.0)
    sin_b = pad(jnp.concatenate([-sin, zeros], axis=1), 0.0)
    reps = LANES // group
    return tuple(jnp.tile(a, (1, reps)) for a in (cos_t, sin_a, sin_b))


def kernel(x, mem, ffn_norm, ffn_w_gate, ffn_w_up, ffn_w_down, mix_norm, ab_w_in, ab_w_out, diff_q_norm, diff_k_norm, diff_lambda_q1, diff_lambda_k1, diff_lambda_q2, diff_lambda_k2, diff_subln, mla_w_dq, mla_q_norm, mla_w_uq, mla_w_dkv, mla_kv_norm, mla_w_ukv, mla_qk_norm_q, mla_qk_norm_k, mla_w_o, xm_norm, xm_mem_norm, xm_w_q, xm_w_kv, xm_q_norm, xm_k_norm, xm_w_o):
    b, s, d = x.shape
    depth = ffn_norm.shape[0]
    x = x.reshape(b * s, d)
    diff_tables = _rope_tables(s, DIFF_HEAD_DIM, 0, DIFF_HEAD_DIM)
    ffn_w_f32 = (ffn_w_gate, ffn_w_up, ffn_w_down)
    ffn_w = tuple(_cast_first(w) for w in ffn_w_f32)
    for layer in range(depth):
        i = layer // 2
        x = _ffn(x, ffn_norm[layer, 0], *ffn_w, layer, 0)
        if layer % 2 == 0:
            lambda_init = 0.8 - 0.6 * math.exp(-0.3 * layer)
            proj = _ab_proj(x, mix_norm[layer], ab_w_in[i], diff_q_norm[i], diff_k_norm[i],
                            diff_tables, s).reshape(b, s, -1)
            lam_rows = jnp.stack([diff_lambda_q1[i], diff_lambda_k1[i],
                                  diff_lambda_q2[i], diff_lambda_k2[i]])
            bound = _score_bound(diff_q_norm[i], diff_k_norm[i], DIFF_HEAD_DIM)
            oa, ffn_w = _diff_attn(proj, bound, lam_rows, diff_subln[i], lambda_init, ffn_w_f32)
            ob = _sb_attn(proj)
            mixes = [oa.reshape(b * s, -1), ob.reshape(b * s, -1)]
            w_out = ab_w_out[i]
        else:
            q, k, v = _mla_proj(x, mix_norm[layer], mla_w_dq[i], mla_q_norm[i], mla_w_uq[i],
                                mla_w_dkv[i], mla_kv_norm[i], mla_w_ukv[i], mla_qk_norm_q[i],
                                mla_qk_norm_k[i], s)
            bound = _score_bound(mla_qk_norm_q[i], mla_qk_norm_k[i], MLA_QK_DIM)
            o = _mla_attn(q.reshape(b, s, -1), k.reshape(b, s, -1), v.reshape(b, s, -1), bound)
            mixes = [o.reshape(b * s, -1)]
            w_out = mla_w_o[i]
        mem_k, mem_v = _mem_kv(mem, xm_mem_norm[layer], xm_w_kv[layer], xm_k_norm[layer])
        x = _post(x, mixes, w_out, xm_norm[layer], xm_w_q[layer], xm_q_norm[layer],
                  mem_k, mem_v, xm_w_o[layer], s)
        x = _ffn(x, ffn_norm[layer, 1], *ffn_w, layer, 1)
    return x.reshape(b, s, d)
```
